```python
import jax, jax.numpy as jnp
from jax import lax
import numpy as np

D_MODEL = 1024
BATCH = 8
SEQ = 2048
DEPTH = 2
DEC_BATCH = 128
DEC_SEQ = 8
PAST_LEN = 16384
PAGE_SIZE = 128

N_EVEN = (DEPTH + 1) // 2
N_ODD = DEPTH // 2
EPS = 1e-6

SGU_CHUNK = 128
SGU_GROUPS = 4
SGU_WIDTH = D_MODEL
SGU_GROUP_DIM = SGU_WIDTH // SGU_GROUPS
RET_HEADS = 4
RET_DK = D_MODEL // 8
RET_DV = D_MODEL // 4
RET_CHUNK = 128
RET_ROPE_BASE = 10000.0
C_HEADS = 16
C_KV_HEADS = 4
C_HEAD_DIM = D_MODEL // C_HEADS
C_GROUP = C_HEADS // C_KV_HEADS
WINDOW = 128
C_ROPE_BASE = 150000.0
N_MEM = 256
MEM_HEADS = 4
MEM_HEAD_DIM = D_MODEL // MEM_HEADS
D_FF = 4 * D_MODEL

EVEN_SPLITS = (SGU_WIDTH, 2 * SGU_WIDTH, 2 * SGU_WIDTH + RET_HEADS * RET_DK,
               2 * SGU_WIDTH + 2 * RET_HEADS * RET_DK,
               2 * SGU_WIDTH + 2 * RET_HEADS * RET_DK + RET_HEADS * RET_DV)
EVEN_IN = 2 * SGU_WIDTH + 2 * RET_HEADS * RET_DK + 2 * RET_HEADS * RET_DV
EVEN_OUT = SGU_WIDTH + RET_HEADS * RET_DV
ODD_SPLITS = (C_HEADS * C_HEAD_DIM, (C_HEADS + C_KV_HEADS) * C_HEAD_DIM)
ODD_IN = (C_HEADS + 2 * C_KV_HEADS) * C_HEAD_DIM

kernel_name = 'hybrid_sgu_retention_swa_decoder_step'


def rms_norm(x, g):
    xf = x.astype(jnp.float32)
    y = xf * lax.rsqrt(jnp.mean(xf * xf, axis=-1, keepdims=True) + EPS)
    return (y * g.astype(jnp.float32)).astype(x.dtype)


def layer_norm(x, g, b):
    xf = x.astype(jnp.float32)
    mu = jnp.mean(xf, axis=-1, keepdims=True)
    xc = xf - mu
    y = xc * lax.rsqrt(jnp.mean(xc * xc, axis=-1, keepdims=True) + EPS)
    return (y * g.astype(jnp.float32) + b.astype(jnp.float32)).astype(x.dtype)


def rotary(x, pos, base):
    half = x.shape[-1] // 2
    inv = base ** (-jnp.arange(half, dtype=jnp.float32) / half)
    ang = pos.astype(jnp.float32)[:, None] * inv[None, :]
    cos = jnp.cos(ang)[None, :, None, :]
    sin = jnp.sin(ang)[None, :, None, :]
    xf = x.astype(jnp.float32)
    x1, x2 = xf[..., :half], xf[..., half:]
    return jnp.concatenate([x1 * cos - x2 * sin, x1 * sin + x2 * cos], axis=-1).astype(x.dtype)


def sgu_mix(u, v, ln_g, ln_b, w_s, b_s):
    b, l, _ = u.shape
    vn = layer_norm(v, ln_g, ln_b)
    c = SGU_CHUNK if l % SGU_CHUNK == 0 else l
    nc = l // c
    mask = jnp.tril(jnp.ones((c, c), dtype=bool))
    w = jnp.where(mask[None], w_s[:, :c, :c], 0).astype(v.dtype)
    vc = vn.reshape(b, nc, c, SGU_GROUPS, SGU_GROUP_DIM)
    mixed = jnp.einsum('gts,bnsgd->bntgd', w, vc) + b_s[:, :c].T[None, None, :, :, None].astype(v.dtype)
    out = u * mixed.reshape(b, l, SGU_WIDTH)
    return out, vc.reshape(b, l, SGU_GROUPS, SGU_GROUP_DIM)


def retention(q, k, v, s0):
    b, l, h, _ = q.shape
    c = RET_CHUNK if l % RET_CHUNK == 0 else l
    nc = l // c
    lg = jnp.log1p(-jnp.exp2(-5.0 - jnp.arange(h, dtype=jnp.float32)))
    idx = jnp.arange(c, dtype=jnp.float32)
    diff = idx[:, None] - idx[None, :]
    decay = jnp.where(diff >= 0, jnp.exp(jnp.maximum(diff, 0.0)[None] * lg[:, None, None]), 0.0)
    xi = jnp.exp((idx[:, None] + 1.0) * lg[None, :])
    zeta = jnp.exp((c - 1.0 - idx)[:, None] * lg[None, :])
    chunk_decay = jnp.exp(c * lg)

    def to_chunks(t):
        return t.astype(jnp.float32).reshape(b, nc, c, h, t.shape[-1]).transpose(1, 0, 2, 3, 4)

    def step(s, inp):
        qc, kc, vc = inp
        sc = jnp.einsum('bihd,bjhd->bhij', qc, kc) * decay
        o = (jnp.einsum('bhij,bjhe->bihe', sc, vc)
             + jnp.einsum('bihd,bhde->bihe', qc, s) * xi[None, :, :, None])
        s = s * chunk_decay[:, None, None] + jnp.einsum('bjhd,bjhe->bhde', kc * zeta[None, :, :, None], vc)
        return s, o

    s, o = lax.scan(step, s0.astype(jnp.float32), (to_chunks(q), to_chunks(k), to_chunks(v)))
    o = o.transpose(1, 0, 2, 3, 4).reshape(b, l, h, v.shape[-1])
    return o, s


def even_mixer(h, pos, s0, w_in, ln_g, ln_b, w_s, b_s, w_out):
    b, l, _ = h.shape
    z = h @ w_in
    u, v, q, k, vr, gate = jnp.split(z, EVEN_SPLITS, axis=-1)
    a_out, v_rows = sgu_mix(jax.nn.gelu(u), jax.nn.gelu(v), ln_g, ln_b, w_s, b_s)
    q = rotary(q.reshape(b, l, RET_HEADS, RET_DK), pos, RET_ROPE_BASE)
    k = rotary(k.reshape(b, l, RET_HEADS, RET_DK), pos, RET_ROPE_BASE) * (RET_DK ** -0.5)
    o, s = retention(q, k, vr.reshape(b, l, RET_HEADS, RET_DV), s0)
    o = o * lax.rsqrt(jnp.mean(o * o, axis=-1, keepdims=True) + EPS)
    b_out = (o.reshape(b, l, RET_HEADS * RET_DV) * jax.nn.silu(gate.astype(jnp.float32))).astype(h.dtype)
    y = jnp.concatenate([a_out, b_out], axis=-1) @ w_out
    return y, v_rows, s.astype(h.dtype)


def odd_qkv(h, pos, w_qkv, b_qkv):
    b, l, _ = h.shape
    q, k, v = jnp.split(h @ w_qkv + b_qkv, ODD_SPLITS, axis=-1)
    q = rotary(q.reshape(b, l, C_HEADS, C_HEAD_DIM), pos, C_ROPE_BASE)
    k = rotary(k.reshape(b, l, C_KV_HEADS, C_HEAD_DIM), pos, C_ROPE_BASE)
    return q, k, v.reshape(b, l, C_KV_HEADS, C_HEAD_DIM)


def sink_attention(q, k, v, valid, sinks):
    s = jnp.einsum('...qkgd,...skd->...kgqs', q, k).astype(jnp.float32) * (C_HEAD_DIM ** -0.5)
    s = jnp.where(valid, s, -jnp.inf)
    sink = jnp.broadcast_to(sinks.astype(jnp.float32).reshape(C_KV_HEADS, C_GROUP, 1, 1), s.shape[:-1] + (1,))
    p = jax.nn.softmax(jnp.concatenate([s, sink], axis=-1), axis=-1)[..., :-1]
    return jnp.einsum('...kgqs,...skd->...qkgd', p.astype(v.dtype), v)


def window_attn_prompt(q, k, v, sinks):
    b, l = q.shape[:2]
    nb = l // WINDOW
    qb = q.reshape(b, nb, WINDOW, C_KV_HEADS, C_GROUP, C_HEAD_DIM)

    def with_prev(t):
        tb = t.reshape(b, nb, WINDOW, C_KV_HEADS, C_HEAD_DIM)
        prev = jnp.concatenate([jnp.zeros_like(tb[:, :1]), tb[:, :-1]], axis=1)
        return jnp.concatenate([prev, tb], axis=2)

    qpos = jnp.arange(l, dtype=jnp.int32).reshape(nb, WINDOW)
    kpos = qpos[:, :1] - WINDOW + jnp.arange(2 * WINDOW, dtype=jnp.int32)[None, :]
    dist = qpos[:, :, None] - kpos[:, None, :]
    valid = (dist >= 0) & (dist < WINDOW) & (kpos[:, None, :] >= 0)
    o = sink_attention(qb, with_prev(k), with_prev(v), valid[:, None, None], sinks)
    return o.reshape(b, l, C_HEADS * C_HEAD_DIM)


def window_attn_sample(q, k, v, buf_k, buf_v, sinks):
    b, l = q.shape[:2]
    kk = jnp.concatenate([buf_k.astype(k.dtype), k], axis=1)
    vv = jnp.concatenate([buf_v.astype(v.dtype), v], axis=1)
    qpos = jnp.arange(l, dtype=jnp.int32)[:, None] + WINDOW
    kpos = jnp.arange(WINDOW + l, dtype=jnp.int32)[None, :]
    dist = qpos - kpos
    valid = (dist >= 0) & (dist < WINDOW)
    o = sink_attention(q.reshape(b, l, C_KV_HEADS, C_GROUP, C_HEAD_DIM), kk, vv, valid, sinks)
    return o.reshape(b, l, C_HEADS * C_HEAD_DIM), kk[:, -WINDOW:], vv[:, -WINDOW:]


def mem_kv(mem, g_mem, w_mk, w_mv):
    b = mem.shape[0]
    m = rms_norm(mem, g_mem)
    k = (m @ w_mk).reshape(b, N_MEM, MEM_HEADS, MEM_HEAD_DIM)
    v = (m @ w_mv).reshape(b, N_MEM, MEM_HEADS, MEM_HEAD_DIM)
    return k, v


def cross_attend(h, mk, mv, w_mq, w_mo):
    b, l, _ = h.shape
    q = (h @ w_mq).reshape(b, l, MEM_HEADS, MEM_HEAD_DIM)
    s = jnp.einsum('blhd,bmhd->bhlm', q, mk.astype(q.dtype)).astype(jnp.float32) * (MEM_HEAD_DIM ** -0.5)
    p = jax.nn.softmax(s, axis=-1)
    o = jnp.einsum('bhlm,bmhd->blhd', p.astype(q.dtype), mv.astype(q.dtype))
    return o.reshape(b, l, MEM_HEADS * MEM_HEAD_DIM) @ w_mo


def squared_relu_mlp(h, w_up, w_down):
    return jnp.square(jax.nn.relu(h @ w_up)) @ w_down


def _normal(k, shape, scale):
    return scale * jax.random.normal(k, shape, jnp.float32)


def _gain(k, shape):
    return 1.0 + _normal(k, shape, 0.02)


def setup_inputs(seed: int = 0) -> dict:
    key = jax.random.key(seed)
    ks = jax.random.split(key, 32)
    return {
        'x_prompt': _normal(ks[0], (BATCH, SEQ, D_MODEL), 1.0),
        'x_sample': _normal(ks[1], (DEC_BATCH, DEC_SEQ, D_MODEL), 1.0),
        'state_ret': _normal(ks[2], (N_EVEN, DEC_BATCH, RET_HEADS, RET_DK, RET_DV), 0.5),
        'cache_win_k': _normal(ks[3], (N_ODD, DEC_BATCH, WINDOW, C_KV_HEADS, C_HEAD_DIM), 1.0),
        'cache_win_v': _normal(ks[4], (N_ODD, DEC_BATCH, WINDOW, C_KV_HEADS, C_HEAD_DIM), 1.0),
        'cache_mem_k': _normal(ks[5], (DEPTH, DEC_BATCH, N_MEM, MEM_HEADS, MEM_HEAD_DIM), 1.0),
        'cache_mem_v': _normal(ks[6], (DEPTH, DEC_BATCH, N_MEM, MEM_HEADS, MEM_HEAD_DIM), 1.0),
        'mem_prompt': _normal(ks[7], (BATCH, N_MEM, D_MODEL), 1.0),
        'g_mix': _gain(ks[8], (DEPTH, D_MODEL)),
        'w_in_e': _normal(ks[9], (N_EVEN, D_MODEL, EVEN_IN), D_MODEL ** -0.5),
        'sgu_ln_g': _gain(ks[10], (N_EVEN, SGU_WIDTH)),
        'sgu_ln_b': _normal(ks[11], (N_EVEN, SGU_WIDTH), 0.02),
        'w_spatial': _normal(ks[12], (N_EVEN, SGU_GROUPS, SGU_CHUNK, SGU_CHUNK), SGU_CHUNK ** -0.5),
        'b_spatial': _gain(ks[13], (N_EVEN, SGU_GROUPS, SGU_CHUNK)),
        'w_out_e': _normal(ks[14], (N_EVEN, EVEN_OUT, D_MODEL), EVEN_OUT ** -0.5),
        'w_qkv_o': _normal(ks[15], (N_ODD, D_MODEL, ODD_IN), D_MODEL ** -0.5),
        'b_qkv_o': _normal(ks[16], (N_ODD, ODD_IN), 0.02),
        'sinks': _normal(ks[17], (N_ODD, C_HEADS), 0.5),
        'w_out_o': _normal(ks[18], (N_ODD, C_HEADS * C_HEAD_DIM, D_MODEL), (C_HEADS * C_HEAD_DIM) ** -0.5),
        'b_out_o': _normal(ks[19], (N_ODD, D_MODEL), 0.02),
        'g_cross': _gain(ks[20], (DEPTH, D_MODEL)),
        'g_mem': _gain(ks[21], (DEPTH, D_MODEL)),
        'w_mq': _normal(ks[22], (DEPTH, D_MODEL, MEM_HEADS * MEM_HEAD_DIM), D_MODEL ** -0.5),
        'w_mk': _normal(ks[23], (DEPTH, D_MODEL, MEM_HEADS * MEM_HEAD_DIM), D_MODEL ** -0.5),
        'w_mv': _normal(ks[24], (DEPTH, D_MODEL, MEM_HEADS * MEM_HEAD_DIM), D_MODEL ** -0.5),
        'w_mo': _normal(ks[25], (DEPTH, MEM_HEADS * MEM_HEAD_DIM, D_MODEL), (MEM_HEADS * MEM_HEAD_DIM) ** -0.5),
        'g_ffn': _gain(ks[26], (DEPTH, D_MODEL)),
        'w_up': _normal(ks[27], (DEPTH, D_MODEL, D_FF), D_MODEL ** -0.5),
        'w_down': _normal(ks[28], (DEPTH, D_FF, D_MODEL), D_FF ** -0.5),
        'g_final': _gain(ks[29], (D_MODEL,)),
    }


def reference(x_prompt, x_sample, state_ret, cache_win_k, cache_win_v, cache_mem_k, cache_mem_v,
              mem_prompt, g_mix, w_in_e, sgu_ln_g, sgu_ln_b, w_spatial, b_spatial, w_out_e,
              w_qkv_o, b_qkv_o, sinks, w_out_o, b_out_o, g_cross, g_mem, w_mq, w_mk, w_mv, w_mo,
              g_ffn, w_up, w_down, g_final):
    bp, lp, _ = x_prompt.shape
    ls = x_sample.shape[1]
    pos_p = jnp.arange(lp, dtype=jnp.int32)
    pos_s = PAST_LEN + jnp.arange(ls, dtype=jnp.int32)
    xp, xs = x_prompt, x_sample
    mem_k_p, mem_v_p = [], []
    ret_p, ret_s, sgu_v_s = [], [], []
    wk_p, wv_p, wk_s, wv_s = [], [], [], []
    for layer in range(DEPTH):
        j = layer // 2
        hp = rms_norm(xp, g_mix[layer])
        hs = rms_norm(xs, g_mix[layer])
        if layer % 2 == 0:
            s0 = jnp.zeros((bp, RET_HEADS, RET_DK, RET_DV), jnp.float32)
            op, _, sp = even_mixer(hp, pos_p, s0, w_in_e[j], sgu_ln_g[j], sgu_ln_b[j],
                                   w_spatial[j], b_spatial[j], w_out_e[j])
            o_s, v_rows, ss = even_mixer(hs, pos_s, state_ret[j], w_in_e[j], sgu_ln_g[j], sgu_ln_b[j],
                                         w_spatial[j], b_spatial[j], w_out_e[j])
            ret_p.append(sp)
            ret_s.append(ss)
            sgu_v_s.append(v_rows)
        else:
            qp, kp, vp = odd_qkv(hp, pos_p, w_qkv_o[j], b_qkv_o[j])
            op = window_attn_prompt(qp, kp, vp, sinks[j]) @ w_out_o[j] + b_out_o[j]
            qs, ks_, vs_ = odd_qkv(hs, pos_s, w_qkv_o[j], b_qkv_o[j])
            att_s, kbuf, vbuf = window_attn_sample(qs, ks_, vs_, cache_win_k[j], cache_win_v[j], sinks[j])
            o_s = att_s @ w_out_o[j] + b_out_o[j]
            wk_p.append(kp[:, -WINDOW:])
            wv_p.append(vp[:, -WINDOW:])
            wk_s.append(kbuf)
            wv_s.append(vbuf)
        xp = xp + op
        xs = xs + o_s
        mk, mv = mem_kv(mem_prompt, g_mem[layer], w_mk[layer], w_mv[layer])
        mem_k_p.append(mk)
        mem_v_p.append(mv)
        xp = xp + cross_attend(rms_norm(xp, g_cross[layer]), mk, mv, w_mq[layer], w_mo[layer])
        xs = xs + cross_attend(rms_norm(xs, g_cross[layer]), cache_mem_k[layer], cache_mem_v[layer],
                               w_mq[layer], w_mo[layer])
        xp = xp + squared_relu_mlp(rms_norm(xp, g_ffn[layer]), w_up[layer], w_down[layer])
        xs = xs + squared_relu_mlp(rms_norm(xs, g_ffn[layer]), w_up[layer], w_down[layer])
    y_prompt = rms_norm(xp, g_final)
    y_sample = rms_norm(xs, g_final)
    return (y_prompt, y_sample, jnp.stack(mem_k_p), jnp.stack(mem_v_p), jnp.stack(ret_p), jnp.stack(ret_s),
            jnp.stack(sgu_v_s), jnp.stack(wk_p), jnp.stack(wv_p), jnp.stack(wk_s), jnp.stack(wv_s))
```

```python
import functools
import math

import numpy as np
import jax
import jax.numpy as jnp
from jax import lax
from jax.experimental import pallas as pl
from jax.experimental.pallas import tpu as pltpu

F32 = jnp.float32
BF16 = jnp.bfloat16

D_MODEL = 1024
DEPTH = 2
PAST_LEN = 16384
EPS = 1e-6

SGU_GROUPS = 4
SGU_GROUP_DIM = 256
RET_HEADS = 4
RET_DK = 128
RET_DV = 256
RET_ROPE_BASE = 10000.0
CHUNK = 128
C_HEADS = 16
C_KV_HEADS = 4
C_HEAD_DIM = 64
C_ROPE_BASE = 150000.0
N_MEM = 256
MEM_HEADS = 4
MEM_HEAD_DIM = 256
D_FF = 4096
EVEN_IN = 5120
EVEN_OUT = 2048
ODD_IN = 1536

LANES = 128
VMEM_LIMIT = 56 * 2 ** 20
NEG = -1e30

TM_PROMPT = 512
BT_SAMPLE = 8
TM_SAMPLE_MLP = 256


def _params(n_axes):
    return pltpu.CompilerParams(dimension_semantics=("arbitrary",) * n_axes,
                                vmem_limit_bytes=VMEM_LIMIT)


def _const(shape):
    nd = len(shape)
    return pl.BlockSpec(shape, lambda *_: (0,) * nd, pipeline_mode=pl.Buffered(1))


def _rms(x, g):
    return x * lax.rsqrt(jnp.mean(x * x, axis=-1, keepdims=True) + EPS) * g


def _dot(a, b):
    return jnp.dot(a, b, preferred_element_type=F32)


def _dot_nt(a, b):
    return lax.dot_general(a, b, (((1,), (1,)), ((), ())), preferred_element_type=F32)


def _dot_tn(a, b):
    return lax.dot_general(a, b, (((0,), (0,)), ((), ())), preferred_element_type=F32)


def _ret_rope_tables(pos):
    half = RET_DK // 2
    inv = RET_ROPE_BASE ** (-np.arange(half, dtype=np.float64) / half)
    ang = pos.astype(np.float64)[:, None] * inv[None, :]
    cos, sin = np.cos(ang), np.sin(ang)
    return (jnp.asarray(np.concatenate([cos, cos], -1), F32),
            jnp.asarray(np.concatenate([-sin, sin], -1), F32))


def _win_rope_tables(pos):
    half = C_HEAD_DIM // 2
    inv = C_ROPE_BASE ** (-np.arange(half, dtype=np.float64) / half)
    ang = pos.astype(np.float64)[:, None] * inv[None, :]
    lane = np.arange(LANES)
    cos = np.cos(ang)[:, lane % half]
    sin = np.sin(ang)[:, lane % half]
    low = (lane % C_HEAD_DIM) < half
    return (jnp.asarray(cos, F32),
            jnp.asarray(np.where(low[None], -sin, 0.0), F32),
            jnp.asarray(np.where(low[None], 0.0, sin), F32))


def _ret_rotate(x, cosf, sinf):
    return x * cosf + pltpu.roll(x, RET_DK // 2, 1) * sinf


def _win_rotate(x, cos, sin_lo, sin_hi):
    return (x * cos + pltpu.roll(x, LANES - C_HEAD_DIM // 2, 1) * sin_lo
            + pltpu.roll(x, C_HEAD_DIM // 2, 1) * sin_hi)


def _log_gamma(h):
    return math.log1p(-2.0 ** (-5.0 - h))


def _memkv_kernel(mem_ref, g_ref, w_ref, k_ref, v_ref):
    m = _rms(mem_ref[0], g_ref[0]).astype(BF16)
    kv = _dot(m, w_ref[0])
    k_ref[0, 0] = kv[:, :D_MODEL]
    v_ref[0, 0] = kv[:, D_MODEL:]


def _memkv(mem, g_mem, w_mkv):
    b = mem.shape[0]
    out = jax.ShapeDtypeStruct((DEPTH, b, N_MEM, D_MODEL), F32)
    return pl.pallas_call(
        _memkv_kernel,
        grid=(DEPTH, b),
        in_specs=[pl.BlockSpec((1, N_MEM, D_MODEL), lambda l, i: (i, 0, 0)),
                  pl.BlockSpec((1, 1, D_MODEL), lambda l, i: (l, 0, 0)),
                  pl.BlockSpec((1, D_MODEL, 2 * D_MODEL), lambda l, i: (l, 0, 0))],
        out_specs=[pl.BlockSpec((1, 1, N_MEM, D_MODEL), lambda l, i: (l, i, 0, 0)),
                   pl.BlockSpec((1, 1, N_MEM, D_MODEL), lambda l, i: (l, i, 0, 0))],
        out_shape=[out, out],
        compiler_params=_params(2),
        name="memkv",
    )(mem, g_mem.reshape(DEPTH, 1, D_MODEL), w_mkv)


def _even_front(x, g_ref, win_ref, lng_ref, lnb_ref, cos_ref, sin_ref):
    h = _rms(x, g_ref[...]).astype(BF16)

    def proj(lo, hi):
        return _dot(h, win_ref[:, lo:hi])

    u = jax.nn.gelu(proj(0, 1024))
    v = jax.nn.gelu(proj(1024, 2048))
    vc = v - jnp.mean(v, axis=-1, keepdims=True)
    vn = vc * lax.rsqrt(jnp.mean(vc * vc, axis=-1, keepdims=True) + EPS) * lng_ref[...] + lnb_ref[...]
    q = proj(2048, 2560)
    k = proj(2560, 3072)
    vr = proj(3072, 4096)
    gate = proj(4096, 5120)
    cosf, sinf = cos_ref[...], sin_ref[...]
    qs, ks = [], []
    for hd in range(RET_HEADS):
        cols = slice(hd * RET_DK, (hd + 1) * RET_DK)
        qs.append(_ret_rotate(q[:, cols], cosf, sinf))
        ks.append(_ret_rotate(k[:, cols], cosf, sinf) * (RET_DK ** -0.5))
    return u, vn, qs, ks, vr, gate


def _head_norm_gate(o, gate):
    on = o * lax.rsqrt(jnp.mean(o * o, axis=-1, keepdims=True) + EPS)
    return on * jax.nn.silu(gate)


def _even_prompt_kernel(x_ref, g_ref, win_ref, lng_ref, lnb_ref, ws_ref, bs_ref, wout_ref,
                        cos_ref, sin_ref, xo_ref, sout_ref,
                        state, u_s, vn_s, q_s, k_s, vr_s, gate_s, cat_s):
    t = pl.program_id(1)
    tm = x_ref.shape[1]

    @pl.when(t == 0)
    def _():
        state[...] = jnp.zeros_like(state)

    x = x_ref[0]
    u, vn, qs, ks, vr, gate = _even_front(x, g_ref, win_ref, lng_ref, lnb_ref, cos_ref, sin_ref)
    u_s[...] = u
    vn_s[...] = vn.astype(BF16)
    for hd in range(RET_HEADS):
        cols = slice(hd * RET_DK, (hd + 1) * RET_DK)
        q_s[:, cols] = qs[hd].astype(BF16)
        k_s[:, cols] = ks[hd]
    vr_s[...] = vr.astype(BF16)
    gate_s[...] = gate

    row = lax.broadcasted_iota(jnp.int32, (CHUNK, CHUNK), 0)
    col = lax.broadcasted_iota(jnp.int32, (CHUNK, CHUNK), 1)
    diff = (row - col).astype(F32)
    ridx = lax.broadcasted_iota(jnp.int32, (CHUNK, 1), 0).astype(F32)
    decay, xi, zeta, cdec = [], [], [], []
    for hd in range(RET_HEADS):
        lg = _log_gamma(hd)
        decay.append(jnp.where(diff >= 0, jnp.exp(jnp.maximum(diff, 0.0) * lg), 0.0))
        xi.append(jnp.exp((ridx + 1.0) * lg))
        zeta.append(jnp.exp((CHUNK - 1.0 - ridx) * lg))
        cdec.append(math.exp(CHUNK * lg))

    def chunk(c, carry):
        rows = pl.ds(pl.multiple_of(c * CHUNK, CHUNK), CHUNK)
        for g in range(SGU_GROUPS):
            cols = slice(g * SGU_GROUP_DIM, (g + 1) * SGU_GROUP_DIM)
            mixed = _dot(ws_ref[g], vn_s[rows, cols]) + bs_ref[g]
            cat_s[rows, cols] = (u_s[rows, cols] * mixed).astype(BF16)
        for hd in range(RET_HEADS):
            kcols = slice(hd * RET_DK, (hd + 1) * RET_DK)
            vcols = slice(hd * RET_DV, (hd + 1) * RET_DV)
            qc = q_s[rows, kcols]
            kf = k_s[rows, kcols]
            vc = vr_s[rows, vcols]
            s_prev = state[hd]
            sc = _dot_nt(qc, kf.astype(BF16)) * decay[hd]
            o = _dot(sc.astype(BF16), vc) + _dot(qc, s_prev.astype(BF16)) * xi[hd]
            state[hd] = s_prev * cdec[hd] + _dot_tn((kf * zeta[hd]).astype(BF16), vc)
            ocols = slice(D_MODEL + hd * RET_DV, D_MODEL + (hd + 1) * RET_DV)
            cat_s[rows, ocols] = _head_norm_gate(o, gate_s[rows, vcols]).astype(BF16)
        return carry

    lax.fori_loop(0, tm // CHUNK, chunk, 0)
    xo_ref[0] = x + _dot(cat_s[...], wout_ref[...])

    @pl.when(t == pl.num_programs(1) - 1)
    def _():
        sout_ref[0] = state[...]


def _even_prompt(x, g, w_in, ln_g, ln_b, w_s, b_s, w_out, cosf, sinf):
    b, l, d = x.shape
    tm = TM_PROMPT
    return pl.pallas_call(
        _even_prompt_kernel,
        grid=(b, l // tm),
        in_specs=[pl.BlockSpec((1, tm, d), lambda i, t: (i, t, 0)),
                  _const((1, d)), _const((d, EVEN_IN)), _const((1, d)), _const((1, d)),
                  _const((SGU_GROUPS, CHUNK, CHUNK)), _const((SGU_GROUPS, CHUNK, 1)),
                  _const((EVEN_OUT, d)),
                  pl.BlockSpec((tm, RET_DK), lambda i, t: (t, 0)),
                  pl.BlockSpec((tm, RET_DK), lambda i, t: (t, 0))],
        out_specs=[pl.BlockSpec((1, tm, d), lambda i, t: (i, t, 0)),
                   pl.BlockSpec((1, RET_HEADS, RET_DK, RET_DV), lambda i, t: (i, 0, 0, 0))],
        out_shape=[jax.ShapeDtypeStruct((b, l, d), F32),
                   jax.ShapeDtypeStruct((b, RET_HEADS, RET_DK, RET_DV), F32)],
        scratch_shapes=[pltpu.VMEM((RET_HEADS, RET_DK, RET_DV), F32),
                        pltpu.VMEM((tm, d), F32), pltpu.VMEM((tm, d), BF16),
                        pltpu.VMEM((tm, RET_HEADS * RET_DK), BF16),
                        pltpu.VMEM((tm, RET_HEADS * RET_DK), F32),
                        pltpu.VMEM((tm, d), BF16), pltpu.VMEM((tm, d), F32),
                        pltpu.VMEM((tm, EVEN_OUT), BF16)],
        compiler_params=_params(2),
        name="even_prompt",
    )(x, g, w_in, ln_g, ln_b, w_s, b_s, w_out, cosf, sinf)


def _even_sample_kernel(x_ref, s0_ref, g_ref, win_ref, lng_ref, lnb_ref, ws_ref, bs_ref, wout_ref,
                        cos_ref, sin_ref, xo_ref, sout_ref, vrow_ref,
                        q_s, kz_s, v_s, o_s):
    tt = x_ref.shape[0]
    ls = tt // s0_ref.shape[0]
    x = x_ref[...]
    u, vn, qs, ks, vr, gate = _even_front(x, g_ref, win_ref, lng_ref, lnb_ref, cos_ref, sin_ref)
    vrow_ref[...] = vn
    vn_b = vn.astype(BF16)
    vr_b = vr.astype(BF16)

    row = lax.broadcasted_iota(jnp.int32, (tt, tt), 0)
    col = lax.broadcasted_iota(jnp.int32, (tt, tt), 1)
    same = (row // ls) == (col // ls)
    diff = (row - col).astype(F32)
    rloc = (lax.broadcasted_iota(jnp.int32, (tt, 1), 0) % ls).astype(F32)

    cat = []
    for g in range(SGU_GROUPS):
        cols = slice(g * SGU_GROUP_DIM, (g + 1) * SGU_GROUP_DIM)
        mixed = _dot(ws_ref[g], vn_b[:, cols]) + bs_ref[g]
        cat.append((u[:, cols] * mixed).astype(BF16))

    xi, cdec = [], []
    for hd in range(RET_HEADS):
        lg = _log_gamma(hd)
        kcols = slice(hd * RET_DK, (hd + 1) * RET_DK)
        vcols = slice(hd * RET_DV, (hd + 1) * RET_DV)
        decay = jnp.where(same & (diff >= 0), jnp.exp(jnp.maximum(diff, 0.0) * lg), 0.0)
        xi.append(jnp.exp((rloc[:ls] + 1.0) * lg))
        cdec.append(math.exp(ls * lg))
        zeta = jnp.exp((ls - 1.0 - rloc) * lg)
        sc = _dot_nt(qs[hd].astype(BF16), ks[hd].astype(BF16)) * decay
        o_s[:, vcols] = _dot(sc.astype(BF16), vr_b[:, vcols])
        q_s[:, kcols] = qs[hd]
        kz_s[:, kcols] = ks[hd] * zeta
    v_s[...] = vr

    def per_batch(bi, carry):
        rows = pl.ds(pl.multiple_of(bi * ls, ls), ls)
        for hd in range(RET_HEADS):
            kcols = slice(hd * RET_DK, (hd + 1) * RET_DK)
            vcols = slice(hd * RET_DV, (hd + 1) * RET_DV)
            s_prev = s0_ref[bi, hd]
            qb = q_s[rows, kcols].astype(BF16)
            o_s[rows, vcols] += _dot(qb, s_prev.astype(BF16)) * xi[hd]
            sout_ref[bi, hd] = s_prev * cdec[hd] + _dot_tn(kz_s[rows, kcols].astype(BF16),
                                                          v_s[rows, vcols].astype(BF16))
        return carry

    lax.fori_loop(0, s0_ref.shape[0], per_batch, 0)
    for hd in range(RET_HEADS):
        vcols = slice(hd * RET_DV, (hd + 1) * RET_DV)
        cat.append(_head_norm_gate(o_s[:, vcols], gate[:, vcols]).astype(BF16))
    xo_ref[...] = x + _dot(jnp.concatenate(cat, axis=-1), wout_ref[...])


def _even_sample(x, s0, g, w_in, ln_g, ln_b, w_s_bd, b_s_bd, w_out, cosf, sinf):
    n, d = x.shape
    nb = s0.shape[0]
    ls = n // nb
    bt = BT_SAMPLE
    tt = bt * ls
    state_spec = pl.BlockSpec((bt, RET_HEADS, RET_DK, RET_DV), lambda i: (i, 0, 0, 0))
    return pl.pallas_call(
        _even_sample_kernel,
        grid=(nb // bt,),
        in_specs=[pl.BlockSpec((tt, d), lambda i: (i, 0)), state_spec,
                  _const((1, d)), _const((d, EVEN_IN)), _const((1, d)), _const((1, d)),
                  _const((SGU_GROUPS, tt, tt)), _const((SGU_GROUPS, tt, 1)),
                  _const((EVEN_OUT, d)), _const((tt, RET_DK)), _const((tt, RET_DK))],
        out_specs=[pl.BlockSpec((tt, d), lambda i: (i, 0)), state_spec,
                   pl.BlockSpec((tt, d), lambda i: (i, 0))],
        out_shape=[jax.ShapeDtypeStruct((n, d), F32),
                   jax.ShapeDtypeStruct(s0.shape, F32),
                   jax.ShapeDtypeStruct((n, d), F32)],
        scratch_shapes=[pltpu.VMEM((tt, RET_HEADS * RET_DK), F32),
                        pltpu.VMEM((tt, RET_HEADS * RET_DK), F32),
                        pltpu.VMEM((tt, d), F32), pltpu.VMEM((tt, d), F32)],
        compiler_params=_params(1),
        name="even_sample",
    )(x, s0, g, w_in, ln_g, ln_b, w_s_bd, b_s_bd, w_out, cosf, sinf)


def _pad_even_odd(blk, kv_in_block):
    lane = lax.broadcasted_iota(jnp.int32, blk.shape, 1)
    rolled = pltpu.roll(blk, C_HEAD_DIM, 1)
    if kv_in_block == 0:
        even = jnp.where(lane < C_HEAD_DIM, blk, 0.0)
        odd = jnp.where(lane >= C_HEAD_DIM, rolled, 0.0)
    else:
        odd = jnp.where(lane >= C_HEAD_DIM, blk, 0.0)
        even = jnp.where(lane < C_HEAD_DIM, rolled, 0.0)
    return even.astype(BF16), odd.astype(BF16)


def _window_attend(q2, ke, ko, ve, vo, valid, sink_e, sink_o):
    outs, dens = [], []
    for kk, vv, sink in ((ke, ve, sink_e), (ko, vo, sink_o)):
        s = jnp.where(valid, _dot_nt(q2, kk), NEG)
        m = jnp.maximum(jnp.max(s, axis=-1, keepdims=True), sink)
        p = jnp.exp(s - m)
        dens.append(jnp.sum(p, axis=-1, keepdims=True) + jnp.exp(sink - m))
        outs.append(_dot(p.astype(BF16), vv))
    lane = lax.broadcasted_iota(jnp.int32, outs[0].shape, 1)
    inv = jnp.where(lane < C_HEAD_DIM, 1.0 / dens[0], 1.0 / dens[1])
    return (outs[0] + outs[1]) * inv


def _odd_front(x, g_ref, wqkv_ref, bqkv_ref, cos_ref, lo_ref, hi_ref):
    h = _rms(x, g_ref[...]).astype(BF16)
    qkv = _dot(h, wqkv_ref[...]) + bqkv_ref[...]
    cos, lo, hi = cos_ref[...], lo_ref[...], hi_ref[...]
    nq = C_HEADS * C_HEAD_DIM
    nk = C_KV_HEADS * C_HEAD_DIM
    qb = [_win_rotate(qkv[:, j * LANES:(j + 1) * LANES], cos, lo, hi) * (C_HEAD_DIM ** -0.5)
          for j in range(nq // LANES)]
    kb = [_win_rotate(qkv[:, nq + j * LANES:nq + (j + 1) * LANES], cos, lo, hi)
          for j in range(nk // LANES)]
    v = qkv[:, nq + nk:]
    return qb, kb, v


def _odd_prompt_kernel(sink_ref, x_ref, g_ref, wqkv_ref, bqkv_ref, wout_ref, bout_ref,
                       cos_ref, lo_ref, hi_ref, xo_ref, wk_ref, wv_ref,
                       q_s, kpad, vpad, att_s):
    t = pl.program_id(1)
    tm = x_ref.shape[1]
    w = CHUNK
    x = x_ref[0]
    qb, kb, v = _odd_front(x, g_ref, wqkv_ref, bqkv_ref, cos_ref, lo_ref, hi_ref)
    for j, blk in enumerate(qb):
        q_s[:, j * LANES:(j + 1) * LANES] = blk.astype(BF16)

    @pl.when(t == 0)
    def _():
        kpad[:, 0:w, :] = jnp.zeros((2 * C_KV_HEADS, w, LANES), BF16)
        vpad[:, 0:w, :] = jnp.zeros((2 * C_KV_HEADS, w, LANES), BF16)

    for kh in range(C_KV_HEADS):
        j, r = divmod(kh, LANES // C_HEAD_DIM)
        ke, ko = _pad_even_odd(kb[j], r)
        ve, vo = _pad_even_odd(v[:, j * LANES:(j + 1) * LANES], r)
        kpad[2 * kh, w:, :] = ke
        kpad[2 * kh + 1, w:, :] = ko
        vpad[2 * kh, w:, :] = ve
        vpad[2 * kh + 1, w:, :] = vo

    @pl.when(t == pl.num_programs(1) - 1)
    def _():
        for j, blk in enumerate(kb):
            wk_ref[0, :, j * LANES:(j + 1) * LANES] = blk[tm - w:, :]
        wv_ref[0] = v[tm - w:, :]

    qi = lax.broadcasted_iota(jnp.int32, (2 * w, 2 * w), 0) % w
    kj = lax.broadcasted_iota(jnp.int32, (2 * w, 2 * w), 1)
    dist = kj - qi
    band = (dist >= 1) & (dist <= w)
    first_half = lax.broadcasted_iota(jnp.int32, (2 * w, 1), 0) < w

    def block(c, carry):
        rows = pl.ds(pl.multiple_of(c * w, w), w)
        keys = pl.ds(pl.multiple_of(c * w, w), 2 * w)
        kmin = jnp.where((t == 0) & (c == 0), w, 0)
        valid = band & (kj >= kmin)
        for kh in range(C_KV_HEADS):
            ke, ko = kpad[2 * kh, keys, :], kpad[2 * kh + 1, keys, :]
            ve, vo = vpad[2 * kh, keys, :], vpad[2 * kh + 1, keys, :]
            c0 = (2 * kh) * LANES
            q2 = jnp.concatenate([q_s[rows, c0:c0 + LANES], q_s[rows, c0 + LANES:c0 + 2 * LANES]], 0)
            h0 = kh * (C_HEADS // C_KV_HEADS)
            sink_e = jnp.where(first_half, sink_ref[h0], sink_ref[h0 + 2])
            sink_o = jnp.where(first_half, sink_ref[h0 + 1], sink_ref[h0 + 3])
            o = _window_attend(q2, ke, ko, ve, vo, valid, sink_e, sink_o)
            att_s[rows, c0:c0 + LANES] = o[:w].astype(BF16)
            att_s[rows, c0 + LANES:c0 + 2 * LANES] = o[w:].astype(BF16)
        return carry

    lax.fori_loop(0, tm // w, block, 0)
    xo_ref[0] = x + _dot(att_s[...], wout_ref[...]) + bout_ref[...]
    kpad[:, 0:w, :] = kpad[:, tm:tm + w, :]
    vpad[:, 0:w, :] = vpad[:, tm:tm + w, :]


def _odd_prompt(x, sinks, g, w_qkv, b_qkv, w_out, b_out, cos, lo, hi):
    b, l, d = x.shape
    tm = TM_PROMPT
    kvw = C_KV_HEADS * C_HEAD_DIM
    win_spec = pl.BlockSpec((1, CHUNK, kvw), lambda i, t: (i, 0, 0))
    tab = pl.BlockSpec((tm, LANES), lambda i, t: (t, 0))
    return pl.pallas_call(
        _odd_prompt_kernel,
        grid=(b, l // tm),
        in_specs=[pl.BlockSpec(memory_space=pltpu.SMEM),
                  pl.BlockSpec((1, tm, d), lambda i, t: (i, t, 0)),
                  _const((1, d)), _const((d, ODD_IN)), _const((1, ODD_IN)),
                  _const((d, d)), _const((1, d)), tab, tab, tab],
        out_specs=[pl.BlockSpec((1, tm, d), lambda i, t: (i, t, 0)), win_spec, win_spec],
        out_shape=[jax.ShapeDtypeStruct((b, l, d), F32),
                   jax.ShapeDtypeStruct((b, CHUNK, kvw), F32),
                   jax.ShapeDtypeStruct((b, CHUNK, kvw), F32)],
        scratch_shapes=[pltpu.VMEM((tm, d), BF16),
                        pltpu.VMEM((2 * C_KV_HEADS, tm + CHUNK, LANES), BF16),
                        pltpu.VMEM((2 * C_KV_HEADS, tm + CHUNK, LANES), BF16),
                        pltpu.VMEM((tm, d), BF16)],
        compiler_params=_params(2),
        name="odd_prompt",
    )(sinks, x, g, w_qkv, b_qkv, w_out, b_out, cos, lo, hi)


def _odd_sample_kernel(sink_ref, x_ref, ck_ref, cv_ref, g_ref, wqkv_ref, bqkv_ref, wout_ref, bout_ref,
                       cos_ref, lo_ref, hi_ref, xo_ref, wk_ref, wv_ref):
    tt = x_ref.shape[0]
    bt, w, kvw = ck_ref.shape
    ls = tt // bt
    x = x_ref[...]
    qb, kb, v = _odd_front(x, g_ref, wqkv_ref, bqkv_ref, cos_ref, lo_ref, hi_ref)
    knew = jnp.concatenate(kb, axis=-1)
    ck = ck_ref[...]
    cv = cv_ref[...]
    for bi in range(bt):
        wk_ref[bi, 0:w - ls, :] = ck[bi, ls:, :]
        wk_ref[bi, w - ls:, :] = knew[bi * ls:(bi + 1) * ls, :]
        wv_ref[bi, 0:w - ls, :] = cv[bi, ls:, :]
        wv_ref[bi, w - ls:, :] = v[bi * ls:(bi + 1) * ls, :]
    kall = jnp.concatenate([ck.reshape(bt * w, kvw), knew], axis=0)
    vall = jnp.concatenate([cv.reshape(bt * w, kvw), v], axis=0)
    ns = bt * w + tt

    tok = lax.broadcasted_iota(jnp.int32, (2 * tt, ns), 0) % tt
    key = lax.broadcasted_iota(jnp.int32, (2 * tt, ns), 1)
    tb, tl = tok // ls, tok % ls
    cached = key < bt * w
    nkey = key - bt * w
    valid = ((cached & ((key // w) == tb) & ((key % w) > tl))
             | ((key >= bt * w) & ((nkey // ls) == tb) & ((nkey % ls) <= tl)))
    first_half = lax.broadcasted_iota(jnp.int32, (2 * tt, 1), 0) < tt

    att = []
    for kh in range(C_KV_HEADS):
        j, r = divmod(kh, LANES // C_HEAD_DIM)
        ke, ko = _pad_even_odd(kall[:, j * LANES:(j + 1) * LANES], r)
        ve, vo = _pad_even_odd(vall[:, j * LANES:(j + 1) * LANES], r)
        q2 = jnp.concatenate([qb[2 * kh], qb[2 * kh + 1]], axis=0).astype(BF16)
        h0 = kh * (C_HEADS // C_KV_HEADS)
        sink_e = jnp.where(first_half, sink_ref[h0], sink_ref[h0 + 2])
        sink_o = jnp.where(first_half, sink_ref[h0 + 1], sink_ref[h0 + 3])
        o = _window_attend(q2, ke, ko, ve, vo, valid, sink_e, sink_o)
        att += [o[:tt].astype(BF16), o[tt:].astype(BF16)]
    xo_ref[...] = x + _dot(jnp.concatenate(att, axis=-1), wout_ref[...]) + bout_ref[...]


def _odd_sample(x, ck, cv, sinks, g, w_qkv, b_qkv, w_out, b_out, cos, lo, hi):
    n, d = x.shape
    nb, w, kvw = ck.shape
    ls = n // nb
    bt = BT_SAMPLE
    tt = bt * ls
    cache_spec = pl.BlockSpec((bt, w, kvw), lambda i: (i, 0, 0))
    return pl.pallas_call(
        _odd_sample_kernel,
        grid=(nb // bt,),
        in_specs=[pl.BlockSpec(memory_space=pltpu.SMEM),
                  pl.BlockSpec((tt, d), lambda i: (i, 0)), cache_spec, cache_spec,
                  _const((1, d)), _const((d, ODD_IN)), _const((1, ODD_IN)),
                  _const((d, d)), _const((1, d)),
                  _const((tt, LANES)), _const((tt, LANES)), _const((tt, LANES))],
        out_specs=[pl.BlockSpec((tt, d), lambda i: (i, 0)), cache_spec, cache_spec],
        out_shape=[jax.ShapeDtypeStruct((n, d), F32),
                   jax.ShapeDtypeStruct(ck.shape, F32), jax.ShapeDtypeStruct(cv.shape, F32)],
        compiler_params=_params(1),
        name="odd_sample",
    )(sinks, x, ck, cv, g, w_qkv, b_qkv, w_out, b_out, cos, lo, hi)


def _cross_heads(q, mk, mv):
    outs = []
    for hd in range(MEM_HEADS):
        cols = slice(hd * MEM_HEAD_DIM, (hd + 1) * MEM_HEAD_DIM)
        s = _dot_nt(q[:, cols], mk[:, cols])
        p = jnp.exp(s - jnp.max(s, axis=-1, keepdims=True))
        den = jnp.sum(p, axis=-1, keepdims=True)
        outs.append(_dot(p.astype(BF16), mv[:, cols]) * (1.0 / den))
    return outs


def _mlp(x1, gf_ref, wup_ref, wdown_ref):
    h2 = _rms(x1, gf_ref[...]).astype(BF16)
    acc = x1
    for j in range(D_FF // D_MODEL):
        sl = slice(j * D_MODEL, (j + 1) * D_MODEL)
        a = jnp.square(jnp.maximum(_dot(h2, wup_ref[:, sl]), 0.0)).astype(BF16)
        acc = acc + _dot(a, wdown_ref[sl, :])
    return acc


def _cross_mlp_prompt_kernel(x_ref, mk_ref, mv_ref, gc_ref, wmq_ref, wmo_ref, gf_ref, wup_ref,
                             wdown_ref, gfin_ref, xo_ref, o_s, *, final):
    x = x_ref[0]
    h = _rms(x, gc_ref[...]).astype(BF16)
    q = (_dot(h, wmq_ref[...]) * (MEM_HEAD_DIM ** -0.5)).astype(BF16)
    outs = _cross_heads(q, mk_ref[0, 0].astype(BF16), mv_ref[0, 0].astype(BF16))
    for hd, o in enumerate(outs):
        o_s[:, hd * MEM_HEAD_DIM:(hd + 1) * MEM_HEAD_DIM] = o.astype(BF16)
    x1 = x + _dot(o_s[...], wmo_ref[...])
    y = _mlp(x1, gf_ref, wup_ref, wdown_ref)
    if final:
        y = _rms(y, gfin_ref[...])
    xo_ref[0] = y


def _cross_mlp_prompt(x, mk, mv, layer, g_cross, w_mq, w_mo, g_ffn, w_up, w_down, g_final, final):
    b, l, d = x.shape
    tm = TM_PROMPT
    mem_spec = pl.BlockSpec((1, 1, N_MEM, d), lambda i, t: (layer, i, 0, 0))
    return pl.pallas_call(
        functools.partial(_cross_mlp_prompt_kernel, final=final),
        grid=(b, l // tm),
        in_specs=[pl.BlockSpec((1, tm, d), lambda i, t: (i, t, 0)), mem_spec, mem_spec,
                  _const((1, d)), _const((d, d)), _const((d, d)), _const((1, d)),
                  _const((d, D_FF)), _const((D_FF, d)), _const((1, d))],
        out_specs=pl.BlockSpec((1, tm, d), lambda i, t: (i, t, 0)),
        out_shape=jax.ShapeDtypeStruct((b, l, d), F32),
        scratch_shapes=[pltpu.VMEM((tm, d), BF16)],
        compiler_params=_params(2),
        name=f"cross_mlp_prompt_{layer}",
    )(x, mk, mv, g_cross, w_mq, w_mo, g_ffn, w_up, w_down, g_final)


def _cross_sample_kernel(x_ref, mk_ref, mv_ref, gc_ref, wmq_ref, wmo_ref, xo_ref, q_s, o_s):
    tt = x_ref.shape[0]
    bt = mk_ref.shape[1]
    ls = tt // bt
    x = x_ref[...]
    h = _rms(x, gc_ref[...]).astype(BF16)
    q_s[...] = _dot(h, wmq_ref[...]) * (MEM_HEAD_DIM ** -0.5)

    def per_batch(bi, carry):
        rows = pl.ds(pl.multiple_of(bi * ls, ls), ls)
        outs = _cross_heads(q_s[rows, :].astype(BF16), mk_ref[0, bi].astype(BF16),
                            mv_ref[0, bi].astype(BF16))
        o_s[rows, :] = jnp.concatenate(outs, axis=-1)
        return carry

    lax.fori_loop(0, bt, per_batch, 0)
    xo_ref[...] = x + _dot(o_s[...].astype(BF16), wmo_ref[...])


def _cross_sample(x, mk, mv, layer, g_cross, w_mq, w_mo):
    n, d = x.shape
    nb = mk.shape[1]
    ls = n // nb
    bt = BT_SAMPLE
    tt = bt * ls
    mem_spec = pl.BlockSpec((1, bt, N_MEM, d), lambda i: (layer, i, 0, 0))
    return pl.pallas_call(
        _cross_sample_kernel,
        grid=(nb // bt,),
        in_specs=[pl.BlockSpec((tt, d), lambda i: (i, 0)), mem_spec, mem_spec,
                  _const((1, d)), _const((d, d)), _const((d, d))],
        out_specs=pl.BlockSpec((tt, d), lambda i: (i, 0)),
        out_shape=jax.ShapeDtypeStruct((n, d), F32),
        scratch_shapes=[pltpu.VMEM((tt, d), F32), pltpu.VMEM((tt, d), F32)],
        compiler_params=_params(1),
        name=f"cross_sample_{layer}",
    )(x, mk, mv, g_cross, w_mq, w_mo)


def _mlp_kernel(x_ref, gf_ref, wup_ref, wdown_ref, gfin_ref, xo_ref, *, final):
    y = _mlp(x_ref[...], gf_ref, wup_ref, wdown_ref)
    if final:
        y = _rms(y, gfin_ref[...])
    xo_ref[...] = y


def _mlp_sample(x, layer, g_ffn, w_up, w_down, g_final, final):
    n, d = x.shape
    tm = TM_SAMPLE_MLP
    return pl.pallas_call(
        functools.partial(_mlp_kernel, final=final),
        grid=(n // tm,),
        in_specs=[pl.BlockSpec((tm, d), lambda i: (i, 0)), _const((1, d)),
                  _const((d, D_FF)), _const((D_FF, d)), _const((1, d))],
        out_specs=pl.BlockSpec((tm, d), lambda i: (i, 0)),
        out_shape=jax.ShapeDtypeStruct((n, d), F32),
        compiler_params=_params(1),
        name=f"mlp_sample_{layer}",
    )(x, g_ffn, w_up, w_down, g_final)


def kernel(x_prompt, x_sample, state_ret, cache_win_k, cache_win_v, cache_mem_k, cache_mem_v, mem_prompt, g_mix, w_in_e, sgu_ln_g, sgu_ln_b, w_spatial, b_spatial, w_out_e, w_qkv_o, b_qkv_o, sinks, w_out_o, b_out_o, g_cross, g_mem, w_mq, w_mk, w_mv, w_mo, g_ffn, w_up, w_down, g_final):
    bp, lp, d = x_prompt.shape
    bs, ls, _ = x_sample.shape
    assert d == D_MODEL and lp % TM_PROMPT == 0 and bs % BT_SAMPLE == 0
    assert ls < CHUNK and CHUNK % ls == 0 and (bs * ls) % TM_SAMPLE_MLP == 0
    ns = bs * ls
    kvw = C_KV_HEADS * C_HEAD_DIM
    tt = BT_SAMPLE * ls

    def row(v):
        return v.reshape(1, -1).astype(F32)

    def bf(w):
        return w.astype(BF16)

    pos_p = np.arange(lp)
    pos_s_tile = PAST_LEN + (np.arange(tt) % ls)
    ret_tab_p = _ret_rope_tables(pos_p)
    ret_tab_s = _ret_rope_tables(pos_s_tile)
    win_tab_p = _win_rope_tables(pos_p)
    win_tab_s = _win_rope_tables(pos_s_tile)

    mk_p, mv_p = _memkv(mem_prompt, g_mem, bf(jnp.concatenate([w_mk, w_mv], axis=-1)))
    mk_s = cache_mem_k.reshape(DEPTH, bs, N_MEM, d)
    mv_s = cache_mem_v.reshape(DEPTH, bs, N_MEM, d)

    xp = x_prompt
    xs = x_sample.reshape(ns, d)
    final_g = row(g_final)
    outs = {}
    for layer in range(DEPTH):
        j = layer // 2
        g = row(g_mix[layer])
        if layer % 2 == 0:
            w_in, w_out = bf(w_in_e[j]), bf(w_out_e[j])
            ln_g, ln_b = row(sgu_ln_g[j]), row(sgu_ln_b[j])
            tril = jnp.tril(jnp.ones((CHUNK, CHUNK), bool))
            w_s = bf(jnp.where(tril[None], w_spatial[j], 0.0))
            b_s = b_spatial[j].reshape(SGU_GROUPS, CHUNK, 1)
            xp, ret_p = _even_prompt(xp, g, w_in, ln_g, ln_b, w_s, b_s, w_out, *ret_tab_p)
            w_small = w_s[:, :ls, :ls]
            eye = jnp.eye(BT_SAMPLE, dtype=BF16)
            w_bd = jnp.einsum("ab,gts->gatbs", eye, w_small).reshape(SGU_GROUPS, tt, tt)
            b_bd = jnp.tile(b_spatial[j][:, :ls], (1, BT_SAMPLE)).reshape(SGU_GROUPS, tt, 1)
            xs, ret_s, v_rows = _even_sample(xs, state_ret[j], g, w_in, ln_g, ln_b, w_bd, b_bd, w_out,
                                             *ret_tab_s)
            outs["ret_p"], outs["ret_s"], outs["v_rows"] = ret_p, ret_s, v_rows
        else:
            w_qkv, w_out = bf(w_qkv_o[j]), bf(w_out_o[j])
            b_qkv, b_out = row(b_qkv_o[j]), row(b_out_o[j])
            xp, wk_p, wv_p = _odd_prompt(xp, sinks[j], g, w_qkv, b_qkv, w_out, b_out, *win_tab_p)
            xs, wk_s, wv_s = _odd_sample(xs, cache_win_k[j].reshape(bs, CHUNK, kvw),
                                         cache_win_v[j].reshape(bs, CHUNK, kvw), sinks[j],
                                         g, w_qkv, b_qkv, w_out, b_out, *win_tab_s)
            outs["wk_p"], outs["wv_p"], outs["wk_s"], outs["wv_s"] = wk_p, wv_p, wk_s, wv_s
        final = layer == DEPTH - 1
        gc, gf = row(g_cross[layer]), row(g_ffn[layer])
        wmq, wmo, wup, wdown = bf(w_mq[layer]), bf(w_mo[layer]), bf(w_up[layer]), bf(w_down[layer])
        xp = _cross_mlp_prompt(xp, mk_p, mv_p, layer, gc, wmq, wmo, gf, wup, wdown, final_g, final)
        xs = _cross_sample(xs, mk_s, mv_s, layer, gc, wmq, wmo)
        xs = _mlp_sample(xs, layer, gf, wup, wdown, final_g, final)

    n_even, n_odd = (DEPTH + 1) // 2, DEPTH // 2
    return (xp,
            xs.reshape(bs, ls, d),
            mk_p.reshape(DEPTH, bp, N_MEM, MEM_HEADS, MEM_HEAD_DIM),
            mv_p.reshape(DEPTH, bp, N_MEM, MEM_HEADS, MEM_HEAD_DIM),
            outs["ret_p"].reshape(n_even, bp, RET_HEADS, RET_DK, RET_DV),
            outs["ret_s"].reshape(n_even, bs, RET_HEADS, RET_DK, RET_DV),
            outs["v_rows"].reshape(n_even, bs, ls, SGU_GROUPS, SGU_GROUP_DIM),
            outs["wk_p"].reshape(n_odd, bp, CHUNK, C_KV_HEADS, C_HEAD_DIM),
            outs["wv_p"].reshape(n_odd, bp, CHUNK, C_KV_HEADS, C_HEAD_DIM),
            outs["wk_s"].reshape(n_odd, bs, CHUNK, C_KV_HEADS, C_HEAD_DIM),
            outs["wv_s"].reshape(n_odd, bs, CHUNK, C_KV_HEADS, C_HEAD_DIM))
```

```python
import functools
import math

import numpy as np
import jax
import jax.numpy as jnp
from jax import lax
from jax.experimental import pallas as pl
from jax.experimental.pallas import tpu as pltpu

F32 = jnp.float32
BF16 = jnp.bfloat16

D_MODEL = 1024
DEPTH = 2
PAST_LEN = 16384
EPS = 1e-6

SGU_GROUPS = 4
SGU_GROUP_DIM = 256
RET_HEADS = 4
RET_DK = 128
RET_DV = 256
RET_ROPE_BASE = 10000.0
CHUNK = 128
C_HEADS = 16
C_KV_HEADS = 4
C_HEAD_DIM = 64
C_ROPE_BASE = 150000.0
N_MEM = 256
MEM_HEADS = 4
MEM_HEAD_DIM = 256
D_FF = 4096
EVEN_IN = 5120
EVEN_OUT = 2048
ODD_IN = 1536

LANES = 128
VMEM_LIMIT = 56 * 2 ** 20
NEG = -1e30

TM_PROMPT = 512
BT_SAMPLE = 8
TM_SAMPLE_MLP = 256


def _params(n_axes):
    return pltpu.CompilerParams(dimension_semantics=("arbitrary",) * n_axes,
                                vmem_limit_bytes=VMEM_LIMIT)


def _const(shape):
    nd = len(shape)
    return pl.BlockSpec(shape, lambda *_: (0,) * nd, pipeline_mode=pl.Buffered(1))


def _rms(x, g):
    return x * lax.rsqrt(jnp.mean(x * x, axis=-1, keepdims=True) + EPS) * g


def _dot(a, b):
    return jnp.dot(a, b, preferred_element_type=F32)


def _dot_nt(a, b):
    return lax.dot_general(a, b, (((1,), (1,)), ((), ())), preferred_element_type=F32)


def _dot_tn(a, b):
    return lax.dot_general(a, b, (((0,), (0,)), ((), ())), preferred_element_type=F32)


def _ret_rope_tables(pos):
    half = RET_DK // 2
    inv = RET_ROPE_BASE ** (-np.arange(half, dtype=np.float64) / half)
    ang = pos.astype(np.float64)[:, None] * inv[None, :]
    cos, sin = np.cos(ang), np.sin(ang)
    return (jnp.asarray(np.concatenate([cos, cos], -1), F32),
            jnp.asarray(np.concatenate([-sin, sin], -1), F32))


def _win_rope_tables(pos):
    half = C_HEAD_DIM // 2
    inv = C_ROPE_BASE ** (-np.arange(half, dtype=np.float64) / half)
    ang = pos.astype(np.float64)[:, None] * inv[None, :]
    lane = np.arange(LANES)
    cos = np.cos(ang)[:, lane % half]
    sin = np.sin(ang)[:, lane % half]
    low = (lane % C_HEAD_DIM) < half
    return (jnp.asarray(cos, F32),
            jnp.asarray(np.where(low[None], -sin, 0.0), F32),
            jnp.asarray(np.where(low[None], 0.0, sin), F32))


def _ret_rotate(x, cosf, sinf):
    return x * cosf + pltpu.roll(x, RET_DK // 2, 1) * sinf


def _win_rotate(x, cos, sin_lo, sin_hi):
    return (x * cos + pltpu.roll(x, LANES - C_HEAD_DIM // 2, 1) * sin_lo
            + pltpu.roll(x, C_HEAD_DIM // 2, 1) * sin_hi)


def _log_gamma(h):
    return math.log1p(-2.0 ** (-5.0 - h))


def _memkv_kernel(mem_ref, g_ref, w_ref, k_ref, v_ref):
    m = _rms(mem_ref[0], g_ref[0]).astype(BF16)
    kv = _dot(m, w_ref[0])
    k_ref[0, 0] = kv[:, :D_MODEL]
    v_ref[0, 0] = kv[:, D_MODEL:]


def _memkv(mem, g_mem, w_mkv):
    b = mem.shape[0]
    out = jax.ShapeDtypeStruct((DEPTH, b, N_MEM, D_MODEL), F32)
    return pl.pallas_call(
        _memkv_kernel,
        grid=(DEPTH, b),
        in_specs=[pl.BlockSpec((1, N_MEM, D_MODEL), lambda l, i: (i, 0, 0)),
                  pl.BlockSpec((1, 1, D_MODEL), lambda l, i: (l, 0, 0)),
                  pl.BlockSpec((1, D_MODEL, 2 * D_MODEL), lambda l, i: (l, 0, 0))],
        out_specs=[pl.BlockSpec((1, 1, N_MEM, D_MODEL), lambda l, i: (l, i, 0, 0)),
                   pl.BlockSpec((1, 1, N_MEM, D_MODEL), lambda l, i: (l, i, 0, 0))],
        out_shape=[out, out],
        compiler_params=_params(2),
        name="memkv",
    )(mem, g_mem.reshape(DEPTH, 1, D_MODEL), w_mkv)


def _even_front(x, g_ref, win_ref, lng_ref, lnb_ref, cos_ref, sin_ref):
    h = _rms(x, g_ref[...]).astype(BF16)

    def proj(lo, hi):
        return _dot(h, win_ref[:, lo:hi])

    u = jax.nn.gelu(proj(0, 1024))
    v = jax.nn.gelu(proj(1024, 2048))
    vc = v - jnp.mean(v, axis=-1, keepdims=True)
    vn = vc * lax.rsqrt(jnp.mean(vc * vc, axis=-1, keepdims=True) + EPS) * lng_ref[...] + lnb_ref[...]
    q = proj(2048, 2560)
    k = proj(2560, 3072)
    vr = proj(3072, 4096)
    gate = proj(4096, 5120)
    cosf, sinf = cos_ref[...], sin_ref[...]
    qs, ks = [], []
    for hd in range(RET_HEADS):
        cols = slice(hd * RET_DK, (hd + 1) * RET_DK)
        qs.append(_ret_rotate(q[:, cols], cosf, sinf))
        ks.append(_ret_rotate(k[:, cols], cosf, sinf) * (RET_DK ** -0.5))
    return u, vn, qs, ks, vr, gate


def _head_norm_gate(o, gate):
    on = o * lax.rsqrt(jnp.mean(o * o, axis=-1, keepdims=True) + EPS)
    return on * jax.nn.silu(gate)


def _even_prompt_kernel(x_ref, g_ref, win_ref, lng_ref, lnb_ref, ws_ref, bs_ref, wout_ref,
                        cos_ref, sin_ref, xo_ref, sout_ref,
                        state, u_s, vn_s, q_s, k_s, vr_s, gate_s, cat_s):
    t = pl.program_id(1)
    tm = x_ref.shape[1]

    @pl.when(t == 0)
    def _():
        state[...] = jnp.zeros_like(state)

    x = x_ref[0]
    u, vn, qs, ks, vr, gate = _even_front(x, g_ref, win_ref, lng_ref, lnb_ref, cos_ref, sin_ref)
    u_s[...] = u
    vn_s[...] = vn.astype(BF16)
    for hd in range(RET_HEADS):
        cols = slice(hd * RET_DK, (hd + 1) * RET_DK)
        q_s[:, cols] = qs[hd].astype(BF16)
        k_s[:, cols] = ks[hd]
    vr_s[...] = vr.astype(BF16)
    gate_s[...] = gate

    row = lax.broadcasted_iota(jnp.int32, (CHUNK, CHUNK), 0)
    col = lax.broadcasted_iota(jnp.int32, (CHUNK, CHUNK), 1)
    diff = (row - col).astype(F32)
    ridx = lax.broadcasted_iota(jnp.int32, (CHUNK, 1), 0).astype(F32)
    decay, xi, zeta, cdec = [], [], [], []
    for hd in range(RET_HEADS):
        lg = _log_gamma(hd)
        decay.append(jnp.where(diff >= 0, jnp.exp(jnp.maximum(diff, 0.0) * lg), 0.0))
        xi.append(jnp.exp((ridx + 1.0) * lg))
        zeta.append(jnp.exp((CHUNK - 1.0 - ridx) * lg))
        cdec.append(math.exp(CHUNK * lg))

    def chunk(c):
        rows = pl.ds(c * CHUNK, CHUNK)
        for g in range(SGU_GROUPS):
            cols = slice(g * SGU_GROUP_DIM, (g + 1) * SGU_GROUP_DIM)
            mixed = _dot(ws_ref[g], vn_s[rows, cols]) + bs_ref[g]
            cat_s[rows, cols] = (u_s[rows, cols] * mixed).astype(BF16)
        for hd in range(RET_HEADS):
            kcols = slice(hd * RET_DK, (hd + 1) * RET_DK)
            vcols = slice(hd * RET_DV, (hd + 1) * RET_DV)
            qc = q_s[rows, kcols]
            kf = k_s[rows, kcols]
            vc = vr_s[rows, vcols]
            s_prev = state[hd]
            sc = _dot_nt(qc, kf.astype(BF16)) * decay[hd]
            o = _dot(sc.astype(BF16), vc) + _dot(qc, s_prev.astype(BF16)) * xi[hd]
            state[hd] = s_prev * cdec[hd] + _dot_tn((kf * zeta[hd]).astype(BF16), vc)
            ocols = slice(D_MODEL + hd * RET_DV, D_MODEL + (hd + 1) * RET_DV)
            cat_s[rows, ocols] = _head_norm_gate(o, gate_s[rows, vcols]).astype(BF16)

    for c in range(tm // CHUNK):
        chunk(c)
    xo_ref[0] = x + _dot(cat_s[...], wout_ref[...])

    @pl.when(t == pl.num_programs(1) - 1)
    def _():
        sout_ref[0] = state[...]


def _even_prompt(x, g, w_in, ln_g, ln_b, w_s, b_s, w_out, cosf, sinf):
    b, l, d = x.shape
    tm = TM_PROMPT
    return pl.pallas_call(
        _even_prompt_kernel,
        grid=(b, l // tm),
        in_specs=[pl.BlockSpec((1, tm, d), lambda i, t: (i, t, 0)),
                  _const((1, d)), _const((d, EVEN_IN)), _const((1, d)), _const((1, d)),
                  _const((SGU_GROUPS, CHUNK, CHUNK)), _const((SGU_GROUPS, CHUNK, 1)),
                  _const((EVEN_OUT, d)),
                  pl.BlockSpec((tm, RET_DK), lambda i, t: (t, 0)),
                  pl.BlockSpec((tm, RET_DK), lambda i, t: (t, 0))],
        out_specs=[pl.BlockSpec((1, tm, d), lambda i, t: (i, t, 0)),
                   pl.BlockSpec((1, RET_HEADS, RET_DK, RET_DV), lambda i, t: (i, 0, 0, 0))],
        out_shape=[jax.ShapeDtypeStruct((b, l, d), F32),
                   jax.ShapeDtypeStruct((b, RET_HEADS, RET_DK, RET_DV), F32)],
        scratch_shapes=[pltpu.VMEM((RET_HEADS, RET_DK, RET_DV), F32),
                        pltpu.VMEM((tm, d), F32), pltpu.VMEM((tm, d), BF16),
                        pltpu.VMEM((tm, RET_HEADS * RET_DK), BF16),
                        pltpu.VMEM((tm, RET_HEADS * RET_DK), F32),
                        pltpu.VMEM((tm, d), BF16), pltpu.VMEM((tm, d), F32),
                        pltpu.VMEM((tm, EVEN_OUT), BF16)],
        compiler_params=_params(2),
        name="even_prompt",
    )(x, g, w_in, ln_g, ln_b, w_s, b_s, w_out, cosf, sinf)


def _even_sample_kernel(x_ref, s0_ref, g_ref, win_ref, lng_ref, lnb_ref, ws_ref, bs_ref, wout_ref,
                        cos_ref, sin_ref, xo_ref, sout_ref, vrow_ref,
                        q_s, kz_s, v_s, o_s):
    tt = x_ref.shape[0]
    ls = tt // s0_ref.shape[0]
    x = x_ref[...]
    u, vn, qs, ks, vr, gate = _even_front(x, g_ref, win_ref, lng_ref, lnb_ref, cos_ref, sin_ref)
    vrow_ref[...] = vn
    vn_b = vn.astype(BF16)
    vr_b = vr.astype(BF16)

    row = lax.broadcasted_iota(jnp.int32, (tt, tt), 0)
    col = lax.broadcasted_iota(jnp.int32, (tt, tt), 1)
    same = (row // ls) == (col // ls)
    diff = (row - col).astype(F32)
    rloc = (lax.broadcasted_iota(jnp.int32, (tt, 1), 0) % ls).astype(F32)

    cat = []
    for g in range(SGU_GROUPS):
        cols = slice(g * SGU_GROUP_DIM, (g + 1) * SGU_GROUP_DIM)
        mixed = _dot(ws_ref[g], vn_b[:, cols]) + bs_ref[g]
        cat.append((u[:, cols] * mixed).astype(BF16))

    xi, cdec = [], []
    for hd in range(RET_HEADS):
        lg = _log_gamma(hd)
        kcols = slice(hd * RET_DK, (hd + 1) * RET_DK)
        vcols = slice(hd * RET_DV, (hd + 1) * RET_DV)
        decay = jnp.where(same & (diff >= 0), jnp.exp(jnp.maximum(diff, 0.0) * lg), 0.0)
        xi.append(jnp.exp((rloc[:ls] + 1.0) * lg))
        cdec.append(math.exp(ls * lg))
        zeta = jnp.exp((ls - 1.0 - rloc) * lg)
        sc = _dot_nt(qs[hd].astype(BF16), ks[hd].astype(BF16)) * decay
        o_s[:, vcols] = _dot(sc.astype(BF16), vr_b[:, vcols])
        q_s[:, kcols] = qs[hd]
        kz_s[:, kcols] = ks[hd] * zeta
    v_s[...] = vr

    def per_batch(bi, carry):
        rows = pl.ds(pl.multiple_of(bi * ls, ls), ls)
        for hd in range(RET_HEADS):
            kcols = slice(hd * RET_DK, (hd + 1) * RET_DK)
            vcols = slice(hd * RET_DV, (hd + 1) * RET_DV)
            s_prev = s0_ref[bi, hd]
            qb = q_s[rows, kcols].astype(BF16)
            o_s[rows, vcols] += _dot(qb, s_prev.astype(BF16)) * xi[hd]
            sout_ref[bi, hd] = s_prev * cdec[hd] + _dot_tn(kz_s[rows, kcols].astype(BF16),
                                                          v_s[rows, vcols].astype(BF16))
        return carry

    lax.fori_loop(0, s0_ref.shape[0], per_batch, 0)
    for hd in range(RET_HEADS):
        vcols = slice(hd * RET_DV, (hd + 1) * RET_DV)
        cat.append(_head_norm_gate(o_s[:, vcols], gate[:, vcols]).astype(BF16))
    xo_ref[...] = x + _dot(jnp.concatenate(cat, axis=-1), wout_ref[...])


def _even_sample(x, s0, g, w_in, ln_g, ln_b, w_s_bd, b_s_bd, w_out, cosf, sinf):
    n, d = x.shape
    nb = s0.shape[0]
    ls = n // nb
    bt = BT_SAMPLE
    tt = bt * ls
    state_spec = pl.BlockSpec((bt, RET_HEADS, RET_DK, RET_DV), lambda i: (i, 0, 0, 0))
    return pl.pallas_call(
        _even_sample_kernel,
        grid=(nb // bt,),
        in_specs=[pl.BlockSpec((tt, d), lambda i: (i, 0)), state_spec,
                  _const((1, d)), _const((d, EVEN_IN)), _const((1, d)), _const((1, d)),
                  _const((SGU_GROUPS, tt, tt)), _const((SGU_GROUPS, tt, 1)),
                  _const((EVEN_OUT, d)), _const((tt, RET_DK)), _const((tt, RET_DK))],
        out_specs=[pl.BlockSpec((tt, d), lambda i: (i, 0)), state_spec,
                   pl.BlockSpec((tt, d), lambda i: (i, 0))],
        out_shape=[jax.ShapeDtypeStruct((n, d), F32),
                   jax.ShapeDtypeStruct(s0.shape, F32),
                   jax.ShapeDtypeStruct((n, d), F32)],
        scratch_shapes=[pltpu.VMEM((tt, RET_HEADS * RET_DK), F32),
                        pltpu.VMEM((tt, RET_HEADS * RET_DK), F32),
                        pltpu.VMEM((tt, d), F32), pltpu.VMEM((tt, d), F32)],
        compiler_params=_params(1),
        name="even_sample",
    )(x, s0, g, w_in, ln_g, ln_b, w_s_bd, b_s_bd, w_out, cosf, sinf)


def _pad_even_odd(blk, kv_in_block):
    lane = lax.broadcasted_iota(jnp.int32, blk.shape, 1)
    rolled = pltpu.roll(blk, C_HEAD_DIM, 1)
    if kv_in_block == 0:
        even = jnp.where(lane < C_HEAD_DIM, blk, 0.0)
        odd = jnp.where(lane >= C_HEAD_DIM, rolled, 0.0)
    else:
        odd = jnp.where(lane >= C_HEAD_DIM, blk, 0.0)
        even = jnp.where(lane < C_HEAD_DIM, rolled, 0.0)
    return even.astype(BF16), odd.astype(BF16)


def _window_attend(q2, ke, ko, ve, vo, valid, sink_e, sink_o):
    outs, dens = [], []
    for kk, vv, sink in ((ke, ve, sink_e), (ko, vo, sink_o)):
        s = jnp.where(valid, _dot_nt(q2, kk), NEG)
        m = jnp.maximum(jnp.max(s, axis=-1, keepdims=True), sink)
        p = jnp.exp(s - m)
        dens.append(jnp.sum(p, axis=-1, keepdims=True) + jnp.exp(sink - m))
        outs.append(_dot(p.astype(BF16), vv))
    lane = lax.broadcasted_iota(jnp.int32, outs[0].shape, 1)
    inv = jnp.where(lane < C_HEAD_DIM, 1.0 / dens[0], 1.0 / dens[1])
    return (outs[0] + outs[1]) * inv


def _odd_front(x, g_ref, wqkv_ref, bqkv_ref, cos_ref, lo_ref, hi_ref):
    h = _rms(x, g_ref[...]).astype(BF16)
    qkv = _dot(h, wqkv_ref[...]) + bqkv_ref[...]
    cos, lo, hi = cos_ref[...], lo_ref[...], hi_ref[...]
    nq = C_HEADS * C_HEAD_DIM
    nk = C_KV_HEADS * C_HEAD_DIM
    qb = [_win_rotate(qkv[:, j * LANES:(j + 1) * LANES], cos, lo, hi) * (C_HEAD_DIM ** -0.5)
          for j in range(nq // LANES)]
    kb = [_win_rotate(qkv[:, nq + j * LANES:nq + (j + 1) * LANES], cos, lo, hi)
          for j in range(nk // LANES)]
    v = qkv[:, nq + nk:]
    return qb, kb, v


def _odd_prompt_kernel(sink_ref, x_ref, g_ref, wqkv_ref, bqkv_ref, wout_ref, bout_ref,
                       cos_ref, lo_ref, hi_ref, xo_ref, wk_ref, wv_ref,
                       q_s, kpad, vpad, att_s):
    t = pl.program_id(1)
    tm = x_ref.shape[1]
    w = CHUNK
    x = x_ref[0]
    qb, kb, v = _odd_front(x, g_ref, wqkv_ref, bqkv_ref, cos_ref, lo_ref, hi_ref)
    for j, blk in enumerate(qb):
        q_s[:, j * LANES:(j + 1) * LANES] = blk.astype(BF16)

    @pl.when(t == 0)
    def _():
        kpad[:, 0:w, :] = jnp.zeros((2 * C_KV_HEADS, w, LANES), BF16)
        vpad[:, 0:w, :] = jnp.zeros((2 * C_KV_HEADS, w, LANES), BF16)

    for kh in range(C_KV_HEADS):
        j, r = divmod(kh, LANES // C_HEAD_DIM)
        ke, ko = _pad_even_odd(kb[j], r)
        ve, vo = _pad_even_odd(v[:, j * LANES:(j + 1) * LANES], r)
        kpad[2 * kh, w:, :] = ke
        kpad[2 * kh + 1, w:, :] = ko
        vpad[2 * kh, w:, :] = ve
        vpad[2 * kh + 1, w:, :] = vo

    @pl.when(t == pl.num_programs(1) - 1)
    def _():
        for j, blk in enumerate(kb):
            wk_ref[0, :, j * LANES:(j + 1) * LANES] = blk[tm - w:, :]
        wv_ref[0] = v[tm - w:, :]

    qi = lax.broadcasted_iota(jnp.int32, (2 * w, 2 * w), 0) % w
    kj = lax.broadcasted_iota(jnp.int32, (2 * w, 2 * w), 1)
    dist = kj - qi
    band = (dist >= 1) & (dist <= w)
    first_half = lax.broadcasted_iota(jnp.int32, (2 * w, 1), 0) < w

    def block(c):
        rows = pl.ds(c * w, w)
        keys = pl.ds(c * w, 2 * w)
        kmin = jnp.where(t == 0, w, 0) if c == 0 else 0
        valid = band & (kj >= kmin)
        for kh in range(C_KV_HEADS):
            ke, ko = kpad[2 * kh, keys, :], kpad[2 * kh + 1, keys, :]
            ve, vo = vpad[2 * kh, keys, :], vpad[2 * kh + 1, keys, :]
            c0 = (2 * kh) * LANES
            q2 = jnp.concatenate([q_s[rows, c0:c0 + LANES], q_s[rows, c0 + LANES:c0 + 2 * LANES]], 0)
            h0 = kh * (C_HEADS // C_KV_HEADS)
            sink_e = jnp.where(first_half, sink_ref[h0], sink_ref[h0 + 2])
            sink_o = jnp.where(first_half, sink_ref[h0 + 1], sink_ref[h0 + 3])
            o = _window_attend(q2, ke, ko, ve, vo, valid, sink_e, sink_o)
            att_s[rows, c0:c0 + LANES] = o[:w].astype(BF16)
            att_s[rows, c0 + LANES:c0 + 2 * LANES] = o[w:].astype(BF16)

    for c in range(tm // w):
        block(c)
    xo_ref[0] = x + _dot(att_s[...], wout_ref[...]) + bout_ref[...]
    kpad[:, 0:w, :] = kpad[:, tm:tm + w, :]
    vpad[:, 0:w, :] = vpad[:, tm:tm + w, :]


def _odd_prompt(x, sinks, g, w_qkv, b_qkv, w_out, b_out, cos, lo, hi):
    b, l, d = x.shape
    tm = TM_PROMPT
    kvw = C_KV_HEADS * C_HEAD_DIM
    win_spec = pl.BlockSpec((1, CHUNK, kvw), lambda i, t: (i, 0, 0))
    tab = pl.BlockSpec((tm, LANES), lambda i, t: (t, 0))
    return pl.pallas_call(
        _odd_prompt_kernel,
        grid=(b, l // tm),
        in_specs=[pl.BlockSpec(memory_space=pltpu.SMEM),
                  pl.BlockSpec((1, tm, d), lambda i, t: (i, t, 0)),
                  _const((1, d)), _const((d, ODD_IN)), _const((1, ODD_IN)),
                  _const((d, d)), _const((1, d)), tab, tab, tab],
        out_specs=[pl.BlockSpec((1, tm, d), lambda i, t: (i, t, 0)), win_spec, win_spec],
        out_shape=[jax.ShapeDtypeStruct((b, l, d), F32),
                   jax.ShapeDtypeStruct((b, CHUNK, kvw), F32),
                   jax.ShapeDtypeStruct((b, CHUNK, kvw), F32)],
        scratch_shapes=[pltpu.VMEM((tm, d), BF16),
                        pltpu.VMEM((2 * C_KV_HEADS, tm + CHUNK, LANES), BF16),
                        pltpu.VMEM((2 * C_KV_HEADS, tm + CHUNK, LANES), BF16),
                        pltpu.VMEM((tm, d), BF16)],
        compiler_params=_params(2),
        name="odd_prompt",
    )(sinks, x, g, w_qkv, b_qkv, w_out, b_out, cos, lo, hi)


def _odd_sample_kernel(sink_ref, x_ref, ck_ref, cv_ref, g_ref, wqkv_ref, bqkv_ref, wout_ref, bout_ref,
                       cos_ref, lo_ref, hi_ref, xo_ref, wk_ref, wv_ref):
    tt = x_ref.shape[0]
    bt, w, kvw = ck_ref.shape
    ls = tt // bt
    x = x_ref[...]
    qb, kb, v = _odd_front(x, g_ref, wqkv_ref, bqkv_ref, cos_ref, lo_ref, hi_ref)
    knew = jnp.concatenate(kb, axis=-1)
    ck = ck_ref[...]
    cv = cv_ref[...]
    for bi in range(bt):
        wk_ref[bi, 0:w - ls, :] = ck[bi, ls:, :]
        wk_ref[bi, w - ls:, :] = knew[bi * ls:(bi + 1) * ls, :]
        wv_ref[bi, 0:w - ls, :] = cv[bi, ls:, :]
        wv_ref[bi, w - ls:, :] = v[bi * ls:(bi + 1) * ls, :]
    kall = jnp.concatenate([ck.reshape(bt * w, kvw), knew], axis=0)
    vall = jnp.concatenate([cv.reshape(bt * w, kvw), v], axis=0)
    ns = bt * w + tt

    tok = lax.broadcasted_iota(jnp.int32, (2 * tt, ns), 0) % tt
    key = lax.broadcasted_iota(jnp.int32, (2 * tt, ns), 1)
    tb, tl = tok // ls, tok % ls
    cached = key < bt * w
    nkey = key - bt * w
    valid = ((cached & ((key // w) == tb) & ((key % w) > tl))
             | ((key >= bt * w) & ((nkey // ls) == tb) & ((nkey % ls) <= tl)))
    first_half = lax.broadcasted_iota(jnp.int32, (2 * tt, 1), 0) < tt

    att = []
    for kh in range(C_KV_HEADS):
        j, r = divmod(kh, LANES // C_HEAD_DIM)
        ke, ko = _pad_even_odd(kall[:, j * LANES:(j + 1) * LANES], r)
        ve, vo = _pad_even_odd(vall[:, j * LANES:(j + 1) * LANES], r)
        q2 = jnp.concatenate([qb[2 * kh], qb[2 * kh + 1]], axis=0).astype(BF16)
        h0 = kh * (C_HEADS // C_KV_HEADS)
        sink_e = jnp.where(first_half, sink_ref[h0], sink_ref[h0 + 2])
        sink_o = jnp.where(first_half, sink_ref[h0 + 1], sink_ref[h0 + 3])
        o = _window_attend(q2, ke, ko, ve, vo, valid, sink_e, sink_o)
        att += [o[:tt].astype(BF16), o[tt:].astype(BF16)]
    xo_ref[...] = x + _dot(jnp.concatenate(att, axis=-1), wout_ref[...]) + bout_ref[...]


def _odd_sample(x, ck, cv, sinks, g, w_qkv, b_qkv, w_out, b_out, cos, lo, hi):
    n, d = x.shape
    nb, w, kvw = ck.shape
    ls = n // nb
    bt = BT_SAMPLE
    tt = bt * ls
    cache_spec = pl.BlockSpec((bt, w, kvw), lambda i: (i, 0, 0))
    return pl.pallas_call(
        _odd_sample_kernel,
        grid=(nb // bt,),
        in_specs=[pl.BlockSpec(memory_space=pltpu.SMEM),
                  pl.BlockSpec((tt, d), lambda i: (i, 0)), cache_spec, cache_spec,
                  _const((1, d)), _const((d, ODD_IN)), _const((1, ODD_IN)),
                  _const((d, d)), _const((1, d)),
                  _const((tt, LANES)), _const((tt, LANES)), _const((tt, LANES))],
        out_specs=[pl.BlockSpec((tt, d), lambda i: (i, 0)), cache_spec, cache_spec],
        out_shape=[jax.ShapeDtypeStruct((n, d), F32),
                   jax.ShapeDtypeStruct(ck.shape, F32), jax.ShapeDtypeStruct(cv.shape, F32)],
        compiler_params=_params(1),
        name="odd_sample",
    )(sinks, x, ck, cv, g, w_qkv, b_qkv, w_out, b_out, cos, lo, hi)


def _cross_heads(q, mk, mv):
    outs = []
    for hd in range(MEM_HEADS):
        cols = slice(hd * MEM_HEAD_DIM, (hd + 1) * MEM_HEAD_DIM)
        s = _dot_nt(q[:, cols], mk[:, cols])
        p = jnp.exp(s - jnp.max(s, axis=-1, keepdims=True))
        den = jnp.sum(p, axis=-1, keepdims=True)
        outs.append(_dot(p.astype(BF16), mv[:, cols]) * (1.0 / den))
    return outs


def _mlp(x1, gf_ref, wup_ref, wdown_ref):
    h2 = _rms(x1, gf_ref[...]).astype(BF16)
    acc = x1
    for j in range(D_FF // D_MODEL):
        sl = slice(j * D_MODEL, (j + 1) * D_MODEL)
        a = jnp.square(jnp.maximum(_dot(h2, wup_ref[:, sl]), 0.0)).astype(BF16)
        acc = acc + _dot(a, wdown_ref[sl, :])
    return acc


def _cross_mlp_prompt_kernel(x_ref, mk_ref, mv_ref, gc_ref, wmq_ref, wmo_ref, gf_ref, wup_ref,
                             wdown_ref, gfin_ref, xo_ref, o_s, *, final):
    x = x_ref[0]
    h = _rms(x, gc_ref[...]).astype(BF16)
    q = (_dot(h, wmq_ref[...]) * (MEM_HEAD_DIM ** -0.5)).astype(BF16)
    outs = _cross_heads(q, mk_ref[0, 0].astype(BF16), mv_ref[0, 0].astype(BF16))
    for hd, o in enumerate(outs):
        o_s[:, hd * MEM_HEAD_DIM:(hd + 1) * MEM_HEAD_DIM] = o.astype(BF16)
    x1 = x + _dot(o_s[...], wmo_ref[...])
    y = _mlp(x1, gf_ref, wup_ref, wdown_ref)
    if final:
        y = _rms(y, gfin_ref[...])
    xo_ref[0] = y


def _cross_mlp_prompt(x, mk, mv, layer, g_cross, w_mq, w_mo, g_ffn, w_up, w_down, g_final, final):
    b, l, d = x.shape
    tm = TM_PROMPT
    mem_spec = pl.BlockSpec((1, 1, N_MEM, d), lambda i, t: (layer, i, 0, 0))
    return pl.pallas_call(
        functools.partial(_cross_mlp_prompt_kernel, final=final),
        grid=(b, l // tm),
        in_specs=[pl.BlockSpec((1, tm, d), lambda i, t: (i, t, 0)), mem_spec, mem_spec,
                  _const((1, d)), _const((d, d)), _const((d, d)), _const((1, d)),
                  _const((d, D_FF)), _const((D_FF, d)), _const((1, d))],
        out_specs=pl.BlockSpec((1, tm, d), lambda i, t: (i, t, 0)),
        out_shape=jax.ShapeDtypeStruct((b, l, d), F32),
        scratch_shapes=[pltpu.VMEM((tm, d), BF16)],
        compiler_params=_params(2),
        name=f"cross_mlp_prompt_{layer}",
    )(x, mk, mv, g_cross, w_mq, w_mo, g_ffn, w_up, w_down, g_final)


def _cross_sample_kernel(x_ref, mk_ref, mv_ref, gc_ref, wmq_ref, wmo_ref, xo_ref, q_s, o_s):
    tt = x_ref.shape[0]
    bt = mk_ref.shape[1]
    ls = tt // bt
    nrow = N_MEM * MEM_HEADS
    x = x_ref[...]
    h = _rms(x, gc_ref[...]).astype(BF16)
    q_s[...] = _dot(h, wmq_ref[...]) * (MEM_HEAD_DIM ** -0.5)
    qhead = lax.broadcasted_iota(jnp.int32, (MEM_HEADS * ls, nrow), 0) // ls
    khead = lax.broadcasted_iota(jnp.int32, (MEM_HEADS * ls, nrow), 1) % MEM_HEADS
    own = qhead == khead

    def per_batch(bi, carry):
        rows = pl.ds(pl.multiple_of(bi * ls, ls), ls)
        qb = q_s[rows, :]
        q4 = jnp.concatenate([qb[:, hd * MEM_HEAD_DIM:(hd + 1) * MEM_HEAD_DIM]
                              for hd in range(MEM_HEADS)], axis=0).astype(BF16)
        k2 = mk_ref[0, bi].reshape(nrow, MEM_HEAD_DIM).astype(BF16)
        v2 = mv_ref[0, bi].reshape(nrow, MEM_HEAD_DIM).astype(BF16)
        s = jnp.where(own, _dot_nt(q4, k2), NEG)
        p = jnp.exp(s - jnp.max(s, axis=-1, keepdims=True))
        den = jnp.sum(p, axis=-1, keepdims=True)
        o4 = _dot(p.astype(BF16), v2) * (1.0 / den)
        for hd in range(MEM_HEADS):
            o_s[rows, hd * MEM_HEAD_DIM:(hd + 1) * MEM_HEAD_DIM] = o4[hd * ls:(hd + 1) * ls]
        return carry

    lax.fori_loop(0, bt, per_batch, 0)
    xo_ref[...] = x + _dot(o_s[...].astype(BF16), wmo_ref[...])


def _cross_sample(x, mk, mv, layer, g_cross, w_mq, w_mo):
    n, d = x.shape
    nb = mk.shape[1]
    ls = n // nb
    bt = BT_SAMPLE
    tt = bt * ls
    mem_spec = pl.BlockSpec((1, bt, N_MEM, MEM_HEADS, MEM_HEAD_DIM), lambda i: (layer, i, 0, 0, 0))
    return pl.pallas_call(
        _cross_sample_kernel,
        grid=(nb // bt,),
        in_specs=[pl.BlockSpec((tt, d), lambda i: (i, 0)), mem_spec, mem_spec,
                  _const((1, d)), _const((d, d)), _const((d, d))],
        out_specs=pl.BlockSpec((tt, d), lambda i: (i, 0)),
        out_shape=jax.ShapeDtypeStruct((n, d), F32),
        scratch_shapes=[pltpu.VMEM((tt, d), F32), pltpu.VMEM((tt, d), F32)],
        compiler_params=_params(1),
        name=f"cross_sample_{layer}",
    )(x, mk, mv, g_cross, w_mq, w_mo)


def _mlp_kernel(x_ref, gf_ref, wup_ref, wdown_ref, gfin_ref, xo_ref, *, final):
    y = _mlp(x_ref[...], gf_ref, wup_ref, wdown_ref)
    if final:
        y = _rms(y, gfin_ref[...])
    xo_ref[...] = y


def _mlp_sample(x, layer, g_ffn, w_up, w_down, g_final, final):
    n, d = x.shape
    tm = TM_SAMPLE_MLP
    return pl.pallas_call(
        functools.partial(_mlp_kernel, final=final),
        grid=(n // tm,),
        in_specs=[pl.BlockSpec((tm, d), lambda i: (i, 0)), _const((1, d)),
                  _const((d, D_FF)), _const((D_FF, d)), _const((1, d))],
        out_specs=pl.BlockSpec((tm, d), lambda i: (i, 0)),
        out_shape=jax.ShapeDtypeStruct((n, d), F32),
        compiler_params=_params(1),
        name=f"mlp_sample_{layer}",
    )(x, g_ffn, w_up, w_down, g_final)


def kernel(x_prompt, x_sample, state_ret, cache_win_k, cache_win_v, cache_mem_k, cache_mem_v, mem_prompt, g_mix, w_in_e, sgu_ln_g, sgu_ln_b, w_spatial, b_spatial, w_out_e, w_qkv_o, b_qkv_o, sinks, w_out_o, b_out_o, g_cross, g_mem, w_mq, w_mk, w_mv, w_mo, g_ffn, w_up, w_down, g_final):
    bp, lp, d = x_prompt.shape
    bs, ls, _ = x_sample.shape
    assert d == D_MODEL and lp % TM_PROMPT == 0 and bs % BT_SAMPLE == 0
    assert ls < CHUNK and CHUNK % ls == 0 and (bs * ls) % TM_SAMPLE_MLP == 0
    ns = bs * ls
    kvw = C_KV_HEADS * C_HEAD_DIM
    tt = BT_SAMPLE * ls

    def row(v):
        return v.reshape(1, -1).astype(F32)

    def bf(w):
        return w.astype(BF16)

    pos_p = np.arange(lp)
    pos_s_tile = PAST_LEN + (np.arange(tt) % ls)
    ret_tab_p = _ret_rope_tables(pos_p)
    ret_tab_s = _ret_rope_tables(pos_s_tile)
    win_tab_p = _win_rope_tables(pos_p)
    win_tab_s = _win_rope_tables(pos_s_tile)

    mk_p, mv_p = _memkv(mem_prompt, g_mem, bf(jnp.concatenate([w_mk, w_mv], axis=-1)))
    mk_s, mv_s = cache_mem_k, cache_mem_v

    xp = x_prompt
    xs = x_sample.reshape(ns, d)
    final_g = row(g_final)
    outs = {}
    for layer in range(DEPTH):
        j = layer // 2
        g = row(g_mix[layer])
        if layer % 2 == 0:
            w_in, w_out = bf(w_in_e[j]), bf(w_out_e[j])
            ln_g, ln_b = row(sgu_ln_g[j]), row(sgu_ln_b[j])
            tril = jnp.tril(jnp.ones((CHUNK, CHUNK), bool))
            w_s = bf(jnp.where(tril[None], w_spatial[j], 0.0))
            b_s = b_spatial[j].reshape(SGU_GROUPS, CHUNK, 1)
            xp, ret_p = _even_prompt(xp, g, w_in, ln_g, ln_b, w_s, b_s, w_out, *ret_tab_p)
            w_small = w_s[:, :ls, :ls]
            eye = jnp.eye(BT_SAMPLE, dtype=BF16)
            w_bd = jnp.einsum("ab,gts->gatbs", eye, w_small).reshape(SGU_GROUPS, tt, tt)
            b_bd = jnp.tile(b_spatial[j][:, :ls], (1, BT_SAMPLE)).reshape(SGU_GROUPS, tt, 1)
            xs, ret_s, v_rows = _even_sample(xs, state_ret[j], g, w_in, ln_g, ln_b, w_bd, b_bd, w_out,
                                             *ret_tab_s)
            outs["ret_p"], outs["ret_s"], outs["v_rows"] = ret_p, ret_s, v_rows
        else:
            w_qkv, w_out = bf(w_qkv_o[j]), bf(w_out_o[j])
            b_qkv, b_out = row(b_qkv_o[j]), row(b_out_o[j])
            xp, wk_p, wv_p = _odd_prompt(xp, sinks[j], g, w_qkv, b_qkv, w_out, b_out, *win_tab_p)
            xs, wk_s, wv_s = _odd_sample(xs, cache_win_k[j].reshape(bs, CHUNK, kvw),
                                         cache_win_v[j].reshape(bs, CHUNK, kvw), sinks[j],
                                         g, w_qkv, b_qkv, w_out, b_out, *win_tab_s)
            outs["wk_p"], outs["wv_p"], outs["wk_s"], outs["wv_s"] = wk_p, wv_p, wk_s, wv_s
        final = layer == DEPTH - 1
        gc, gf = row(g_cross[layer]), row(g_ffn[layer])
        wmq, wmo, wup, wdown = bf(w_mq[layer]), bf(w_mo[layer]), bf(w_up[layer]), bf(w_down[layer])
        xp = _cross_mlp_prompt(xp, mk_p, mv_p, layer, gc, wmq, wmo, gf, wup, wdown, final_g, final)
        xs = _cross_sample(xs, mk_s, mv_s, layer, gc, wmq, wmo)
        xs = _mlp_sample(xs, layer, gf, wup, wdown, final_g, final)

    n_even, n_odd = (DEPTH + 1) // 2, DEPTH // 2
    return (xp,
            xs.reshape(bs, ls, d),
            mk_p.reshape(DEPTH, bp, N_MEM, MEM_HEADS, MEM_HEAD_DIM),
            mv_p.reshape(DEPTH, bp, N_MEM, MEM_HEADS, MEM_HEAD_DIM),
            outs["ret_p"].reshape(n_even, bp, RET_HEADS, RET_DK, RET_DV),
            outs["ret_s"].reshape(n_even, bs, RET_HEADS, RET_DK, RET_DV),
            outs["v_rows"].reshape(n_even, bs, ls, SGU_GROUPS, SGU_GROUP_DIM),
            outs["wk_p"].reshape(n_odd, bp, CHUNK, C_KV_HEADS, C_HEAD_DIM),
            outs["wv_p"].reshape(n_odd, bp, CHUNK, C_KV_HEADS, C_HEAD_DIM),
            outs["wk_s"].reshape(n_odd, bs, CHUNK, C_KV_HEADS, C_HEAD_DIM),
            outs["wv_s"].reshape(n_odd, bs, CHUNK, C_KV_HEADS, C_HEAD_DIM))
```

```python
import functools
import math

import numpy as np
import jax
import jax.numpy as jnp
from jax import lax
from jax.experimental import pallas as pl
from jax.experimental.pallas import tpu as pltpu

F32 = jnp.float32
BF16 = jnp.bfloat16

D_MODEL = 1024
DEPTH = 2
PAST_LEN = 16384
EPS = 1e-6

SGU_GROUPS = 4
SGU_GROUP_DIM = 256
RET_HEADS = 4
RET_DK = 128
RET_DV = 256
RET_ROPE_BASE = 10000.0
CHUNK = 128
C_HEADS = 16
C_KV_HEADS = 4
C_HEAD_DIM = 64
C_ROPE_BASE = 150000.0
N_MEM = 256
MEM_HEADS = 4
MEM_HEAD_DIM = 256
D_FF = 4096
EVEN_IN = 5120
EVEN_OUT = 2048
ODD_IN = 1536

LANES = 128
VMEM_LIMIT = 56 * 2 ** 20
NEG = -1e30
LOG2E = math.log2(math.e)

TM_PROMPT = 512
BT_SAMPLE = 8
TM_SAMPLE_MLP = 256


def _params(n_axes):
    return pltpu.CompilerParams(dimension_semantics=("arbitrary",) * n_axes,
                                vmem_limit_bytes=VMEM_LIMIT)


def _const(shape):
    nd = len(shape)
    return pl.BlockSpec(shape, lambda *_: (0,) * nd, pipeline_mode=pl.Buffered(1))


def _layer(layer, shape):
    nd = len(shape)
    return pl.BlockSpec((None,) + tuple(shape), lambda *_: (layer,) + (0,) * nd,
                        pipeline_mode=pl.Buffered(1))


def _rms(x, g):
    return x * lax.rsqrt(jnp.mean(x * x, axis=-1, keepdims=True) + EPS) * g


def _dot(a, b):
    return jnp.dot(a, b, preferred_element_type=F32)


def _dot_nt(a, b):
    return lax.dot_general(a, b, (((1,), (1,)), ((), ())), preferred_element_type=F32)


def _dot_tn(a, b):
    return lax.dot_general(a, b, (((0,), (0,)), ((), ())), preferred_element_type=F32)


def _ret_rope_tables(pos):
    half = RET_DK // 2
    inv = RET_ROPE_BASE ** (-np.arange(half, dtype=np.float64) / half)
    ang = pos.astype(np.float64)[:, None] * inv[None, :]
    cos, sin = np.cos(ang), np.sin(ang)
    return (jnp.asarray(np.concatenate([cos, cos], -1), F32),
            jnp.asarray(np.concatenate([-sin, sin], -1), F32))


def _win_rope_tables(pos):
    half = C_HEAD_DIM // 2
    inv = C_ROPE_BASE ** (-np.arange(half, dtype=np.float64) / half)
    ang = pos.astype(np.float64)[:, None] * inv[None, :]
    lane = np.arange(LANES)
    cos = np.cos(ang)[:, lane % half]
    sin = np.sin(ang)[:, lane % half]
    low = (lane % C_HEAD_DIM) < half
    return (jnp.asarray(cos, F32),
            jnp.asarray(np.where(low[None], -sin, 0.0), F32),
            jnp.asarray(np.where(low[None], 0.0, sin), F32))


def _ret_rotate(x, cosf, sinf):
    return x * cosf + pltpu.roll(x, RET_DK // 2, 1) * sinf


def _win_rotate(x, cos, sin_lo, sin_hi):
    return (x * cos + pltpu.roll(x, LANES - C_HEAD_DIM // 2, 1) * sin_lo
            + pltpu.roll(x, C_HEAD_DIM // 2, 1) * sin_hi)


def _log_gamma(h):
    return math.log1p(-2.0 ** (-5.0 - h))


def _memkv_kernel(mem_ref, g_ref, wk_ref, wv_ref, k_ref, v_ref):
    m = _rms(mem_ref[0], g_ref[0]).astype(BF16)
    k_ref[0, 0] = _dot(m, wk_ref[0])
    v_ref[0, 0] = _dot(m, wv_ref[0])


def _memkv(mem, g_mem, w_mk, w_mv):
    b = mem.shape[0]
    out = jax.ShapeDtypeStruct((DEPTH, b, N_MEM, D_MODEL), F32)
    return pl.pallas_call(
        _memkv_kernel,
        grid=(DEPTH, b),
        in_specs=[pl.BlockSpec((1, N_MEM, D_MODEL), lambda l, i: (i, 0, 0)),
                  pl.BlockSpec((1, 1, D_MODEL), lambda l, i: (l, 0, 0)),
                  pl.BlockSpec((1, D_MODEL, D_MODEL), lambda l, i: (l, 0, 0)),
                  pl.BlockSpec((1, D_MODEL, D_MODEL), lambda l, i: (l, 0, 0))],
        out_specs=[pl.BlockSpec((1, 1, N_MEM, D_MODEL), lambda l, i: (l, i, 0, 0)),
                   pl.BlockSpec((1, 1, N_MEM, D_MODEL), lambda l, i: (l, i, 0, 0))],
        out_shape=[out, out],
        compiler_params=_params(2),
        name="memkv",
    )(mem, g_mem.reshape(DEPTH, 1, D_MODEL), w_mk, w_mv)


def _even_front(x, g_ref, win_ref, lng_ref, lnb_ref, cos_ref, sin_ref):
    h = _rms(x, g_ref[...]).astype(BF16)

    def proj(lo, hi):
        return _dot(h, win_ref[:, lo:hi])

    u = jax.nn.gelu(proj(0, 1024))
    v = jax.nn.gelu(proj(1024, 2048))
    vc = v - jnp.mean(v, axis=-1, keepdims=True)
    vn = vc * lax.rsqrt(jnp.mean(vc * vc, axis=-1, keepdims=True) + EPS) * lng_ref[...] + lnb_ref[...]
    q = proj(2048, 2560)
    k = proj(2560, 3072)
    vr = proj(3072, 4096)
    gate = proj(4096, 5120)
    cosf, sinf = cos_ref[...], sin_ref[...]
    qs, ks = [], []
    for hd in range(RET_HEADS):
        cols = slice(hd * RET_DK, (hd + 1) * RET_DK)
        qs.append(_ret_rotate(q[:, cols], cosf, sinf))
        ks.append(_ret_rotate(k[:, cols], cosf, sinf) * (RET_DK ** -0.5))
    return u, vn, qs, ks, vr, gate


def _head_norm_gate(o, gate):
    on = o * lax.rsqrt(jnp.mean(o * o, axis=-1, keepdims=True) + EPS)
    return on * jax.nn.silu(gate)


def _even_prompt_kernel(x_ref, g_ref, win_ref, lng_ref, lnb_ref, ws_ref, bs_ref, wout_ref,
                        cos_ref, sin_ref, xo_ref, sout_ref,
                        state, u_s, vn_s, q_s, k_s, vr_s, gate_s, cat_s):
    t = pl.program_id(1)
    tm = x_ref.shape[1]

    @pl.when(t == 0)
    def _():
        state[...] = jnp.zeros_like(state)

    x = x_ref[0]
    u, vn, qs, ks, vr, gate = _even_front(x, g_ref, win_ref, lng_ref, lnb_ref, cos_ref, sin_ref)
    u_s[...] = u
    vn_s[...] = vn.astype(BF16)
    for hd in range(RET_HEADS):
        cols = slice(hd * RET_DK, (hd + 1) * RET_DK)
        q_s[:, cols] = qs[hd].astype(BF16)
        k_s[:, cols] = ks[hd]
    vr_s[...] = vr.astype(BF16)
    gate_s[...] = gate

    row = lax.broadcasted_iota(jnp.int32, (CHUNK, CHUNK), 0)
    col = lax.broadcasted_iota(jnp.int32, (CHUNK, CHUNK), 1)
    diff = (row - col).astype(F32)
    ridx = lax.broadcasted_iota(jnp.int32, (CHUNK, 1), 0).astype(F32)
    decay, xi, zeta, cdec = [], [], [], []
    for hd in range(RET_HEADS):
        lg = _log_gamma(hd)
        decay.append(jnp.where(diff >= 0, jnp.exp(jnp.maximum(diff, 0.0) * lg), 0.0))
        xi.append(jnp.exp((ridx + 1.0) * lg))
        zeta.append(jnp.exp((CHUNK - 1.0 - ridx) * lg))
        cdec.append(math.exp(CHUNK * lg))

    def chunk(c):
        rows = pl.ds(c * CHUNK, CHUNK)
        for g in range(SGU_GROUPS):
            cols = slice(g * SGU_GROUP_DIM, (g + 1) * SGU_GROUP_DIM)
            mixed = _dot(ws_ref[g], vn_s[rows, cols]) + bs_ref[g]
            cat_s[rows, cols] = (u_s[rows, cols] * mixed).astype(BF16)
        for hd in range(RET_HEADS):
            kcols = slice(hd * RET_DK, (hd + 1) * RET_DK)
            vcols = slice(hd * RET_DV, (hd + 1) * RET_DV)
            qc = q_s[rows, kcols]
            kf = k_s[rows, kcols]
            vc = vr_s[rows, vcols]
            s_prev = state[hd]
            sc = _dot_nt(qc, kf.astype(BF16)) * decay[hd]
            o = _dot(sc.astype(BF16), vc) + _dot(qc, s_prev.astype(BF16)) * xi[hd]
            state[hd] = s_prev * cdec[hd] + _dot_tn((kf * zeta[hd]).astype(BF16), vc)
            ocols = slice(D_MODEL + hd * RET_DV, D_MODEL + (hd + 1) * RET_DV)
            cat_s[rows, ocols] = _head_norm_gate(o, gate_s[rows, vcols]).astype(BF16)

    for c in range(tm // CHUNK):
        chunk(c)
    xo_ref[0] = x + _dot(cat_s[...], wout_ref[...])

    @pl.when(t == pl.num_programs(1) - 1)
    def _():
        sout_ref[0] = state[...]


def _even_prompt(x, j, g, w_in, ln_g, ln_b, w_s, b_s, w_out, cosf, sinf):
    b, l, d = x.shape
    tm = TM_PROMPT
    return pl.pallas_call(
        _even_prompt_kernel,
        grid=(b, l // tm),
        in_specs=[pl.BlockSpec((1, tm, d), lambda i, t: (i, t, 0)),
                  _const((1, d)), _layer(j, (d, EVEN_IN)), _const((1, d)), _const((1, d)),
                  _const((SGU_GROUPS, CHUNK, CHUNK)), _const((SGU_GROUPS, CHUNK, 1)),
                  _layer(j, (EVEN_OUT, d)),
                  pl.BlockSpec((tm, RET_DK), lambda i, t: (t, 0)),
                  pl.BlockSpec((tm, RET_DK), lambda i, t: (t, 0))],
        out_specs=[pl.BlockSpec((1, tm, d), lambda i, t: (i, t, 0)),
                   pl.BlockSpec((1, RET_HEADS, RET_DK, RET_DV), lambda i, t: (i, 0, 0, 0))],
        out_shape=[jax.ShapeDtypeStruct((b, l, d), F32),
                   jax.ShapeDtypeStruct((b, RET_HEADS, RET_DK, RET_DV), F32)],
        scratch_shapes=[pltpu.VMEM((RET_HEADS, RET_DK, RET_DV), F32),
                        pltpu.VMEM((tm, d), F32), pltpu.VMEM((tm, d), BF16),
                        pltpu.VMEM((tm, RET_HEADS * RET_DK), BF16),
                        pltpu.VMEM((tm, RET_HEADS * RET_DK), F32),
                        pltpu.VMEM((tm, d), BF16), pltpu.VMEM((tm, d), F32),
                        pltpu.VMEM((tm, EVEN_OUT), BF16)],
        compiler_params=_params(2),
        name="even_prompt",
    )(x, g, w_in, ln_g, ln_b, w_s, b_s, w_out, cosf, sinf)


def _even_sample_kernel(x_ref, s0_ref, g_ref, win_ref, lng_ref, lnb_ref, ws_ref, bs_ref, wout_ref,
                        cos_ref, sin_ref, xo_ref, sout_ref, vrow_ref,
                        q_s, kz_s, v_s, o_s):
    tt = x_ref.shape[0]
    ls = tt // s0_ref.shape[0]
    x = x_ref[...]
    u, vn, qs, ks, vr, gate = _even_front(x, g_ref, win_ref, lng_ref, lnb_ref, cos_ref, sin_ref)
    vrow_ref[...] = vn
    vn_b = vn.astype(BF16)
    vr_b = vr.astype(BF16)

    row = lax.broadcasted_iota(jnp.int32, (tt, tt), 0)
    col = lax.broadcasted_iota(jnp.int32, (tt, tt), 1)
    same = (row // ls) == (col // ls)
    diff = (row - col).astype(F32)
    rloc = (lax.broadcasted_iota(jnp.int32, (tt, 1), 0) % ls).astype(F32)

    cat = []
    for g in range(SGU_GROUPS):
        cols = slice(g * SGU_GROUP_DIM, (g + 1) * SGU_GROUP_DIM)
        mixed = _dot(ws_ref[g], vn_b[:, cols]) + bs_ref[g]
        cat.append((u[:, cols] * mixed).astype(BF16))

    xi, cdec = [], []
    for hd in range(RET_HEADS):
        lg = _log_gamma(hd)
        kcols = slice(hd * RET_DK, (hd + 1) * RET_DK)
        vcols = slice(hd * RET_DV, (hd + 1) * RET_DV)
        decay = jnp.where(same & (diff >= 0), jnp.exp(jnp.maximum(diff, 0.0) * lg), 0.0)
        xi.append(jnp.exp((rloc[:ls] + 1.0) * lg))
        cdec.append(math.exp(ls * lg))
        zeta = jnp.exp((ls - 1.0 - rloc) * lg)
        sc = _dot_nt(qs[hd].astype(BF16), ks[hd].astype(BF16)) * decay
        o_s[:, vcols] = _dot(sc.astype(BF16), vr_b[:, vcols])
        q_s[:, kcols] = qs[hd]
        kz_s[:, kcols] = ks[hd] * zeta
    v_s[...] = vr

    def per_batch(bi, carry):
        rows = pl.ds(pl.multiple_of(bi * ls, ls), ls)
        for hd in range(RET_HEADS):
            kcols = slice(hd * RET_DK, (hd + 1) * RET_DK)
            vcols = slice(hd * RET_DV, (hd + 1) * RET_DV)
            s_prev = s0_ref[bi, hd]
            qb = q_s[rows, kcols].astype(BF16)
            o_s[rows, vcols] += _dot(qb, s_prev.astype(BF16)) * xi[hd]
            sout_ref[bi, hd] = s_prev * cdec[hd] + _dot_tn(kz_s[rows, kcols].astype(BF16),
                                                          v_s[rows, vcols].astype(BF16))
        return carry

    lax.fori_loop(0, s0_ref.shape[0], per_batch, 0, unroll=2)
    for hd in range(RET_HEADS):
        vcols = slice(hd * RET_DV, (hd + 1) * RET_DV)
        cat.append(_head_norm_gate(o_s[:, vcols], gate[:, vcols]).astype(BF16))
    xo_ref[...] = x + _dot(jnp.concatenate(cat, axis=-1), wout_ref[...])


def _even_sample(x, s0, j, g, w_in, ln_g, ln_b, w_s_bd, b_s_bd, w_out, cosf, sinf):
    n, d = x.shape
    nb = s0.shape[0]
    ls = n // nb
    bt = BT_SAMPLE
    tt = bt * ls
    state_spec = pl.BlockSpec((bt, RET_HEADS, RET_DK, RET_DV), lambda i: (i, 0, 0, 0))
    return pl.pallas_call(
        _even_sample_kernel,
        grid=(nb // bt,),
        in_specs=[pl.BlockSpec((tt, d), lambda i: (i, 0)), state_spec,
                  _const((1, d)), _layer(j, (d, EVEN_IN)), _const((1, d)), _const((1, d)),
                  _const((SGU_GROUPS, tt, tt)), _const((SGU_GROUPS, tt, 1)),
                  _layer(j, (EVEN_OUT, d)), _const((tt, RET_DK)), _const((tt, RET_DK))],
        out_specs=[pl.BlockSpec((tt, d), lambda i: (i, 0)), state_spec,
                   pl.BlockSpec((tt, d), lambda i: (i, 0))],
        out_shape=[jax.ShapeDtypeStruct((n, d), F32),
                   jax.ShapeDtypeStruct(s0.shape, F32),
                   jax.ShapeDtypeStruct((n, d), F32)],
        scratch_shapes=[pltpu.VMEM((tt, RET_HEADS * RET_DK), F32),
                        pltpu.VMEM((tt, RET_HEADS * RET_DK), F32),
                        pltpu.VMEM((tt, d), F32), pltpu.VMEM((tt, d), F32)],
        compiler_params=_params(1),
        name="even_sample",
    )(x, s0, g, w_in, ln_g, ln_b, w_s_bd, b_s_bd, w_out, cosf, sinf)


def _pad_even_odd(blk, kv_in_block):
    lane = lax.broadcasted_iota(jnp.int32, blk.shape, 1)
    rolled = pltpu.roll(blk, C_HEAD_DIM, 1)
    if kv_in_block == 0:
        even = jnp.where(lane < C_HEAD_DIM, blk, 0.0)
        odd = jnp.where(lane >= C_HEAD_DIM, rolled, 0.0)
    else:
        odd = jnp.where(lane >= C_HEAD_DIM, blk, 0.0)
        even = jnp.where(lane < C_HEAD_DIM, rolled, 0.0)
    return even.astype(BF16), odd.astype(BF16)


def _pair_normalise(outs, dens):
    lane = lax.broadcasted_iota(jnp.int32, outs[0].shape, 1)
    inv = jnp.where(lane < C_HEAD_DIM, 1.0 / dens[0], 1.0 / dens[1])
    return (outs[0] + outs[1]) * inv


def _window_attend(q2, ke, ko, ve, vo, valid, sink_e, sink_o):
    outs, dens = [], []
    for kk, vv, sink in ((ke, ve, sink_e), (ko, vo, sink_o)):
        s = jnp.where(valid, _dot_nt(q2, kk), NEG)
        m = jnp.maximum(jnp.max(s, axis=-1, keepdims=True), sink)
        p = jnp.exp2(s - m)
        dens.append(jnp.sum(p, axis=-1, keepdims=True) + jnp.exp2(sink - m))
        outs.append(_dot(p.astype(BF16), vv))
    return _pair_normalise(outs, dens)


def _banded_attend(q2, ke, ko, ve, vo, upper, prev_bias, sink_e, sink_o):
    w = upper.shape[1]
    zero = jnp.zeros((), F32)
    outs, dens = [], []
    for kk, vv, sink in ((ke, ve, sink_e), (ko, vo, sink_o)):
        s2 = _dot_nt(q2, kk)
        s_prev = s2[:, :w] if prev_bias is None else s2[:, :w] + prev_bias
        s = jnp.where(upper, s_prev, s2[:, w:])
        m = jnp.maximum(jnp.max(s, axis=-1, keepdims=True), sink)
        p = jnp.exp2(s - m)
        dens.append(jnp.sum(p, axis=-1, keepdims=True) + jnp.exp2(sink - m))
        pp = jnp.concatenate([jnp.where(upper, p, zero), jnp.where(upper, zero, p)], axis=1)
        outs.append(_dot(pp.astype(BF16), vv))
    return _pair_normalise(outs, dens)


def _odd_front(x, g_ref, wqkv_ref, bqkv_ref, cos_ref, lo_ref, hi_ref):
    h = _rms(x, g_ref[...]).astype(BF16)
    qkv = _dot(h, wqkv_ref[...]) + bqkv_ref[...]
    cos, lo, hi = cos_ref[...], lo_ref[...], hi_ref[...]
    nq = C_HEADS * C_HEAD_DIM
    nk = C_KV_HEADS * C_HEAD_DIM
    qb = [_win_rotate(qkv[:, j * LANES:(j + 1) * LANES], cos, lo, hi) * (C_HEAD_DIM ** -0.5 * LOG2E)
          for j in range(nq // LANES)]
    kb = [_win_rotate(qkv[:, nq + j * LANES:nq + (j + 1) * LANES], cos, lo, hi)
          for j in range(nk // LANES)]
    v = qkv[:, nq + nk:]
    return qb, kb, v


def _odd_prompt_kernel(sink_ref, x_ref, g_ref, wqkv_ref, bqkv_ref, wout_ref, bout_ref,
                       cos_ref, lo_ref, hi_ref, xo_ref, wk_ref, wv_ref,
                       q_s, kpad, vpad, att_s):
    t = pl.program_id(1)
    tm = x_ref.shape[1]
    w = CHUNK
    x = x_ref[0]
    qb, kb, v = _odd_front(x, g_ref, wqkv_ref, bqkv_ref, cos_ref, lo_ref, hi_ref)
    for j, blk in enumerate(qb):
        q_s[:, j * LANES:(j + 1) * LANES] = blk.astype(BF16)

    @pl.when(t == 0)
    def _():
        kpad[:, 0:w, :] = jnp.zeros((2 * C_KV_HEADS, w, LANES), BF16)
        vpad[:, 0:w, :] = jnp.zeros((2 * C_KV_HEADS, w, LANES), BF16)

    for kh in range(C_KV_HEADS):
        j, r = divmod(kh, LANES // C_HEAD_DIM)
        ke, ko = _pad_even_odd(kb[j], r)
        ve, vo = _pad_even_odd(v[:, j * LANES:(j + 1) * LANES], r)
        kpad[2 * kh, w:, :] = ke
        kpad[2 * kh + 1, w:, :] = ko
        vpad[2 * kh, w:, :] = ve
        vpad[2 * kh + 1, w:, :] = vo

    @pl.when(t == pl.num_programs(1) - 1)
    def _():
        for j, blk in enumerate(kb):
            wk_ref[0, :, j * LANES:(j + 1) * LANES] = blk[tm - w:, :]
        wv_ref[0] = v[tm - w:, :]

    qi = lax.broadcasted_iota(jnp.int32, (2 * w, w), 0) % w
    kj = lax.broadcasted_iota(jnp.int32, (2 * w, w), 1)
    upper = kj > qi
    first_half = lax.broadcasted_iota(jnp.int32, (2 * w, 1), 0) < w

    def block(c):
        rows = pl.ds(c * w, w)
        keys = pl.ds(c * w, 2 * w)
        prev_bias = jnp.where(t == 0, NEG, 0.0) if c == 0 else None
        for kh in range(C_KV_HEADS):
            ke, ko = kpad[2 * kh, keys, :], kpad[2 * kh + 1, keys, :]
            ve, vo = vpad[2 * kh, keys, :], vpad[2 * kh + 1, keys, :]
            c0 = (2 * kh) * LANES
            q2 = jnp.concatenate([q_s[rows, c0:c0 + LANES], q_s[rows, c0 + LANES:c0 + 2 * LANES]], 0)
            h0 = kh * (C_HEADS // C_KV_HEADS)
            sink_e = jnp.where(first_half, sink_ref[h0], sink_ref[h0 + 2]) * LOG2E
            sink_o = jnp.where(first_half, sink_ref[h0 + 1], sink_ref[h0 + 3]) * LOG2E
            o = _banded_attend(q2, ke, ko, ve, vo, upper, prev_bias, sink_e, sink_o)
            att_s[rows, c0:c0 + LANES] = o[:w].astype(BF16)
            att_s[rows, c0 + LANES:c0 + 2 * LANES] = o[w:].astype(BF16)

    for c in range(tm // w):
        block(c)
    xo_ref[0] = x + _dot(att_s[...], wout_ref[...]) + bout_ref[...]
    kpad[:, 0:w, :] = kpad[:, tm:tm + w, :]
    vpad[:, 0:w, :] = vpad[:, tm:tm + w, :]


def _odd_prompt(x, j, sinks, g, w_qkv, b_qkv, w_out, b_out, cos, lo, hi):
    b, l, d = x.shape
    tm = TM_PROMPT
    kvw = C_KV_HEADS * C_HEAD_DIM
    win_spec = pl.BlockSpec((1, CHUNK, kvw), lambda i, t: (i, 0, 0))
    tab = pl.BlockSpec((tm, LANES), lambda i, t: (t, 0))
    return pl.pallas_call(
        _odd_prompt_kernel,
        grid=(b, l // tm),
        in_specs=[pl.BlockSpec(memory_space=pltpu.SMEM),
                  pl.BlockSpec((1, tm, d), lambda i, t: (i, t, 0)),
                  _const((1, d)), _layer(j, (d, ODD_IN)), _const((1, ODD_IN)),
                  _layer(j, (d, d)), _const((1, d)), tab, tab, tab],
        out_specs=[pl.BlockSpec((1, tm, d), lambda i, t: (i, t, 0)), win_spec, win_spec],
        out_shape=[jax.ShapeDtypeStruct((b, l, d), F32),
                   jax.ShapeDtypeStruct((b, CHUNK, kvw), F32),
                   jax.ShapeDtypeStruct((b, CHUNK, kvw), F32)],
        scratch_shapes=[pltpu.VMEM((tm, d), BF16),
                        pltpu.VMEM((2 * C_KV_HEADS, tm + CHUNK, LANES), BF16),
                        pltpu.VMEM((2 * C_KV_HEADS, tm + CHUNK, LANES), BF16),
                        pltpu.VMEM((tm, d), BF16)],
        compiler_params=_params(2),
        name="odd_prompt",
    )(sinks, x, g, w_qkv, b_qkv, w_out, b_out, cos, lo, hi)


def _odd_sample_kernel(sink_ref, x_ref, ck_ref, cv_ref, g_ref, wqkv_ref, bqkv_ref, wout_ref, bout_ref,
                       cos_ref, lo_ref, hi_ref, xo_ref, wk_ref, wv_ref):
    tt = x_ref.shape[0]
    bt, w, kvw = ck_ref.shape
    ls = tt // bt
    x = x_ref[...]
    qb, kb, v = _odd_front(x, g_ref, wqkv_ref, bqkv_ref, cos_ref, lo_ref, hi_ref)
    knew = jnp.concatenate(kb, axis=-1)
    ck = ck_ref[...]
    cv = cv_ref[...]
    for bi in range(bt):
        wk_ref[bi, 0:w - ls, :] = ck[bi, ls:, :]
        wk_ref[bi, w - ls:, :] = knew[bi * ls:(bi + 1) * ls, :]
        wv_ref[bi, 0:w - ls, :] = cv[bi, ls:, :]
        wv_ref[bi, w - ls:, :] = v[bi * ls:(bi + 1) * ls, :]
    kall = jnp.concatenate([ck.reshape(bt * w, kvw), knew], axis=0)
    vall = jnp.concatenate([cv.reshape(bt * w, kvw), v], axis=0)
    ns = bt * w + tt

    tok = lax.broadcasted_iota(jnp.int32, (2 * tt, ns), 0) % tt
    key = lax.broadcasted_iota(jnp.int32, (2 * tt, ns), 1)
    tb, tl = tok // ls, tok % ls
    cached = key < bt * w
    nkey = key - bt * w
    valid = ((cached & ((key // w) == tb) & ((key % w) > tl))
             | ((key >= bt * w) & ((nkey // ls) == tb) & ((nkey % ls) <= tl)))
    first_half = lax.broadcasted_iota(jnp.int32, (2 * tt, 1), 0) < tt

    att = []
    for kh in range(C_KV_HEADS):
        j, r = divmod(kh, LANES // C_HEAD_DIM)
        ke, ko = _pad_even_odd(kall[:, j * LANES:(j + 1) * LANES], r)
        ve, vo = _pad_even_odd(vall[:, j * LANES:(j + 1) * LANES], r)
        q2 = jnp.concatenate([qb[2 * kh], qb[2 * kh + 1]], axis=0).astype(BF16)
        h0 = kh * (C_HEADS // C_KV_HEADS)
        sink_e = jnp.where(first_half, sink_ref[h0], sink_ref[h0 + 2]) * LOG2E
        sink_o = jnp.where(first_half, sink_ref[h0 + 1], sink_ref[h0 + 3]) * LOG2E
        o = _window_attend(q2, ke, ko, ve, vo, valid, sink_e, sink_o)
        att += [o[:tt].astype(BF16), o[tt:].astype(BF16)]
    xo_ref[...] = x + _dot(jnp.concatenate(att, axis=-1), wout_ref[...]) + bout_ref[...]


def _odd_sample(x, ck, cv, j, sinks, g, w_qkv, b_qkv, w_out, b_out, cos, lo, hi):
    n, d = x.shape
    nb, w, kvw = ck.shape
    ls = n // nb
    bt = BT_SAMPLE
    tt = bt * ls
    cache_spec = pl.BlockSpec((bt, w, kvw), lambda i: (i, 0, 0))
    return pl.pallas_call(
        _odd_sample_kernel,
        grid=(nb // bt,),
        in_specs=[pl.BlockSpec(memory_space=pltpu.SMEM),
                  pl.BlockSpec((tt, d), lambda i: (i, 0)), cache_spec, cache_spec,
                  _const((1, d)), _layer(j, (d, ODD_IN)), _const((1, ODD_IN)),
                  _layer(j, (d, d)), _const((1, d)),
                  _const((tt, LANES)), _const((tt, LANES)), _const((tt, LANES))],
        out_specs=[pl.BlockSpec((tt, d), lambda i: (i, 0)), cache_spec, cache_spec],
        out_shape=[jax.ShapeDtypeStruct((n, d), F32),
                   jax.ShapeDtypeStruct(ck.shape, F32), jax.ShapeDtypeStruct(cv.shape, F32)],
        compiler_params=_params(1),
        name="odd_sample",
    )(sinks, x, ck, cv, g, w_qkv, b_qkv, w_out, b_out, cos, lo, hi)


def _cross_heads(q, mk, mv):
    outs = []
    for hd in range(MEM_HEADS):
        cols = slice(hd * MEM_HEAD_DIM, (hd + 1) * MEM_HEAD_DIM)
        s = _dot_nt(q[:, cols], mk[:, cols])
        p = jnp.exp(s - jnp.max(s, axis=-1, keepdims=True))
        den = jnp.sum(p, axis=-1, keepdims=True)
        outs.append(_dot(p.astype(BF16), mv[:, cols]) * (1.0 / den))
    return outs


def _mlp(x1, gf_ref, wup_ref, wdown_ref):
    h2 = _rms(x1, gf_ref[...]).astype(BF16)
    acc = x1
    for j in range(D_FF // D_MODEL):
        sl = slice(j * D_MODEL, (j + 1) * D_MODEL)
        a = jnp.square(jnp.maximum(_dot(h2, wup_ref[:, sl]), 0.0)).astype(BF16)
        acc = acc + _dot(a, wdown_ref[sl, :])
    return acc


def _cross_mlp_prompt_kernel(x_ref, mk_ref, mv_ref, gc_ref, wmq_ref, wmo_ref, gf_ref, wup_ref,
                             wdown_ref, gfin_ref, xo_ref, o_s, *, final):
    x = x_ref[0]
    h = _rms(x, gc_ref[...]).astype(BF16)
    q = (_dot(h, wmq_ref[...]) * (MEM_HEAD_DIM ** -0.5)).astype(BF16)
    outs = _cross_heads(q, mk_ref[0, 0].astype(BF16), mv_ref[0, 0].astype(BF16))
    for hd, o in enumerate(outs):
        o_s[:, hd * MEM_HEAD_DIM:(hd + 1) * MEM_HEAD_DIM] = o.astype(BF16)
    x1 = x + _dot(o_s[...], wmo_ref[...])
    y = _mlp(x1, gf_ref, wup_ref, wdown_ref)
    if final:
        y = _rms(y, gfin_ref[...])
    xo_ref[0] = y


def _cross_mlp_prompt(x, mk, mv, layer, g_cross, w_mq, w_mo, g_ffn, w_up, w_down, g_final, final):
    b, l, d = x.shape
    tm = TM_PROMPT
    mem_spec = pl.BlockSpec((1, 1, N_MEM, d), lambda i, t: (layer, i, 0, 0))
    return pl.pallas_call(
        functools.partial(_cross_mlp_prompt_kernel, final=final),
        grid=(b, l // tm),
        in_specs=[pl.BlockSpec((1, tm, d), lambda i, t: (i, t, 0)), mem_spec, mem_spec,
                  _const((1, d)), _layer(layer, (d, d)), _layer(layer, (d, d)), _const((1, d)),
                  _layer(layer, (d, D_FF)), _layer(layer, (D_FF, d)), _const((1, d))],
        out_specs=pl.BlockSpec((1, tm, d), lambda i, t: (i, t, 0)),
        out_shape=jax.ShapeDtypeStruct((b, l, d), F32),
        scratch_shapes=[pltpu.VMEM((tm, d), BF16)],
        compiler_params=_params(2),
        name=f"cross_mlp_prompt_{layer}",
    )(x, mk, mv, g_cross, w_mq, w_mo, g_ffn, w_up, w_down, g_final)


def _cross_sample_kernel(x_ref, mk_ref, mv_ref, gc_ref, wmq_ref, wmo_ref, xo_ref, q_s, o_s):
    tt = x_ref.shape[0]
    bt = mk_ref.shape[1]
    ls = tt // bt
    nrow = N_MEM * MEM_HEADS
    x = x_ref[...]
    h = _rms(x, gc_ref[...]).astype(BF16)
    q_s[...] = _dot(h, wmq_ref[...]) * (MEM_HEAD_DIM ** -0.5)
    qhead = lax.broadcasted_iota(jnp.int32, (MEM_HEADS * ls, nrow), 0) // ls
    khead = lax.broadcasted_iota(jnp.int32, (MEM_HEADS * ls, nrow), 1) % MEM_HEADS
    own = qhead == khead

    def per_batch(bi, carry):
        rows = pl.ds(pl.multiple_of(bi * ls, ls), ls)
        qb = q_s[rows, :]
        q4 = jnp.concatenate([qb[:, hd * MEM_HEAD_DIM:(hd + 1) * MEM_HEAD_DIM]
                              for hd in range(MEM_HEADS)], axis=0).astype(BF16)
        k2 = mk_ref[0, bi].reshape(nrow, MEM_HEAD_DIM).astype(BF16)
        v2 = mv_ref[0, bi].reshape(nrow, MEM_HEAD_DIM).astype(BF16)
        s = jnp.where(own, _dot_nt(q4, k2), NEG)
        p = jnp.exp(s - jnp.max(s, axis=-1, keepdims=True))
        den = jnp.sum(p, axis=-1, keepdims=True)
        o4 = _dot(p.astype(BF16), v2) * (1.0 / den)
        for hd in range(MEM_HEADS):
            o_s[rows, hd * MEM_HEAD_DIM:(hd + 1) * MEM_HEAD_DIM] = o4[hd * ls:(hd + 1) * ls]
        return carry

    lax.fori_loop(0, bt, per_batch, 0, unroll=2)
    xo_ref[...] = x + _dot(o_s[...].astype(BF16), wmo_ref[...])


def _cross_sample(x, mk, mv, layer, g_cross, w_mq, w_mo):
    n, d = x.shape
    nb = mk.shape[1]
    ls = n // nb
    bt = BT_SAMPLE
    tt = bt * ls
    mem_spec = pl.BlockSpec((1, bt, N_MEM, MEM_HEADS, MEM_HEAD_DIM), lambda i: (layer, i, 0, 0, 0))
    return pl.pallas_call(
        _cross_sample_kernel,
        grid=(nb // bt,),
        in_specs=[pl.BlockSpec((tt, d), lambda i: (i, 0)), mem_spec, mem_spec,
                  _const((1, d)), _layer(layer, (d, d)), _layer(layer, (d, d))],
        out_specs=pl.BlockSpec((tt, d), lambda i: (i, 0)),
        out_shape=jax.ShapeDtypeStruct((n, d), F32),
        scratch_shapes=[pltpu.VMEM((tt, d), F32), pltpu.VMEM((tt, d), F32)],
        compiler_params=_params(1),
        name=f"cross_sample_{layer}",
    )(x, mk, mv, g_cross, w_mq, w_mo)


def _mlp_kernel(x_ref, gf_ref, wup_ref, wdown_ref, gfin_ref, xo_ref, *, final):
    y = _mlp(x_ref[...], gf_ref, wup_ref, wdown_ref)
    if final:
        y = _rms(y, gfin_ref[...])
    xo_ref[...] = y


def _mlp_sample(x, layer, g_ffn, w_up, w_down, g_final, final):
    n, d = x.shape
    tm = TM_SAMPLE_MLP
    return pl.pallas_call(
        functools.partial(_mlp_kernel, final=final),
        grid=(n // tm,),
        in_specs=[pl.BlockSpec((tm, d), lambda i: (i, 0)), _const((1, d)),
                  _layer(layer, (d, D_FF)), _layer(layer, (D_FF, d)), _const((1, d))],
        out_specs=pl.BlockSpec((tm, d), lambda i: (i, 0)),
        out_shape=jax.ShapeDtypeStruct((n, d), F32),
        compiler_params=_params(1),
        name=f"mlp_sample_{layer}",
    )(x, g_ffn, w_up, w_down, g_final)


def kernel(x_prompt, x_sample, state_ret, cache_win_k, cache_win_v, cache_mem_k, cache_mem_v, mem_prompt, g_mix, w_in_e, sgu_ln_g, sgu_ln_b, w_spatial, b_spatial, w_out_e, w_qkv_o, b_qkv_o, sinks, w_out_o, b_out_o, g_cross, g_mem, w_mq, w_mk, w_mv, w_mo, g_ffn, w_up, w_down, g_final):
    bp, lp, d = x_prompt.shape
    bs, ls, _ = x_sample.shape
    assert d == D_MODEL and lp % TM_PROMPT == 0 and bs % BT_SAMPLE == 0
    assert ls < CHUNK and CHUNK % ls == 0 and (bs * ls) % TM_SAMPLE_MLP == 0
    ns = bs * ls
    kvw = C_KV_HEADS * C_HEAD_DIM
    tt = BT_SAMPLE * ls

    def row(v):
        return v.reshape(1, -1).astype(F32)

    def bf(w):
        return w.astype(BF16)

    pos_p = np.arange(lp)
    pos_s_tile = PAST_LEN + (np.arange(tt) % ls)
    ret_tab_p = _ret_rope_tables(pos_p)
    ret_tab_s = _ret_rope_tables(pos_s_tile)
    win_tab_p = _win_rope_tables(pos_p)
    win_tab_s = _win_rope_tables(pos_s_tile)

    w_in_b, w_out_e_b, w_qkv_b, w_out_o_b = bf(w_in_e), bf(w_out_e), bf(w_qkv_o), bf(w_out_o)
    w_mq_b, w_mk_b, w_mv_b, w_mo_b = bf(w_mq), bf(w_mk), bf(w_mv), bf(w_mo)
    w_up_b, w_down_b = bf(w_up), bf(w_down)

    mk_p, mv_p = _memkv(mem_prompt, g_mem, w_mk_b, w_mv_b)
    mk_s, mv_s = cache_mem_k, cache_mem_v

    xp = x_prompt
    xs = x_sample.reshape(ns, d)
    final_g = row(g_final)
    outs = {}
    for layer in range(DEPTH):
        j = layer // 2
        g = row(g_mix[layer])
        if layer % 2 == 0:
            ln_g, ln_b = row(sgu_ln_g[j]), row(sgu_ln_b[j])
            tril = jnp.tril(jnp.ones((CHUNK, CHUNK), bool))
            w_s = bf(jnp.where(tril[None], w_spatial[j], 0.0))
            b_s = b_spatial[j].reshape(SGU_GROUPS, CHUNK, 1)
            xp, ret_p = _even_prompt(xp, j, g, w_in_b, ln_g, ln_b, w_s, b_s, w_out_e_b, *ret_tab_p)
            w_small = w_s[:, :ls, :ls]
            eye = jnp.eye(BT_SAMPLE, dtype=BF16)
            w_bd = jnp.einsum("ab,gts->gatbs", eye, w_small).reshape(SGU_GROUPS, tt, tt)
            b_bd = jnp.tile(b_spatial[j][:, :ls], (1, BT_SAMPLE)).reshape(SGU_GROUPS, tt, 1)
            xs, ret_s, v_rows = _even_sample(xs, state_ret[j], j, g, w_in_b, ln_g, ln_b, w_bd, b_bd,
                                             w_out_e_b, *ret_tab_s)
            outs["ret_p"], outs["ret_s"], outs["v_rows"] = ret_p, ret_s, v_rows
        else:
            b_qkv, b_out = row(b_qkv_o[j]), row(b_out_o[j])
            xp, wk_p, wv_p = _odd_prompt(xp, j, sinks[j], g, w_qkv_b, b_qkv, w_out_o_b, b_out,
                                         *win_tab_p)
            xs, wk_s, wv_s = _odd_sample(xs, cache_win_k[j].reshape(bs, CHUNK, kvw),
                                         cache_win_v[j].reshape(bs, CHUNK, kvw), j, sinks[j],
                                         g, w_qkv_b, b_qkv, w_out_o_b, b_out, *win_tab_s)
            outs["wk_p"], outs["wv_p"], outs["wk_s"], outs["wv_s"] = wk_p, wv_p, wk_s, wv_s
        final = layer == DEPTH - 1
        gc, gf = row(g_cross[layer]), row(g_ffn[layer])
        xp = _cross_mlp_prompt(xp, mk_p, mv_p, layer, gc, w_mq_b, w_mo_b, gf, w_up_b, w_down_b,
                               final_g, final)
        xs = _cross_sample(xs, mk_s, mv_s, layer, gc, w_mq_b, w_mo_b)
        xs = _mlp_sample(xs, layer, gf, w_up_b, w_down_b, final_g, final)

    n_even, n_odd = (DEPTH + 1) // 2, DEPTH // 2
    return (xp,
            xs.reshape(bs, ls, d),
            mk_p.reshape(DEPTH, bp, N_MEM, MEM_HEADS, MEM_HEAD_DIM),
            mv_p.reshape(DEPTH, bp, N_MEM, MEM_HEADS, MEM_HEAD_DIM),
            outs["ret_p"].reshape(n_even, bp, RET_HEADS, RET_DK, RET_DV),
            outs["ret_s"].reshape(n_even, bs, RET_HEADS, RET_DK, RET_DV),
            outs["v_rows"].reshape(n_even, bs, ls, SGU_GROUPS, SGU_GROUP_DIM),
            outs["wk_p"].reshape(n_odd, bp, CHUNK, C_KV_HEADS, C_HEAD_DIM),
            outs["wv_p"].reshape(n_odd, bp, CHUNK, C_KV_HEADS, C_HEAD_DIM),
            outs["wk_s"].reshape(n_odd, bs, CHUNK, C_KV_HEADS, C_HEAD_DIM),
            outs["wv_s"].reshape(n_odd, bs, CHUNK, C_KV_HEADS, C_HEAD_DIM))
```

```python
import functools
import math

import numpy as np
import jax
import jax.numpy as jnp
from jax import lax
from jax.experimental import pallas as pl
from jax.experimental.pallas import tpu as pltpu

F32 = jnp.float32
BF16 = jnp.bfloat16

D_MODEL = 1024
DEPTH = 2
PAST_LEN = 16384
EPS = 1e-6

SGU_GROUPS = 4
SGU_GROUP_DIM = 256
RET_HEADS = 4
RET_DK = 128
RET_DV = 256
RET_ROPE_BASE = 10000.0
CHUNK = 128
C_HEADS = 16
C_KV_HEADS = 4
C_HEAD_DIM = 64
C_ROPE_BASE = 150000.0
N_MEM = 256
MEM_HEADS = 4
MEM_HEAD_DIM = 256
D_FF = 4096
EVEN_IN = 5120
EVEN_OUT = 2048
ODD_IN = 1536

LANES = 128
VMEM_LIMIT = 56 * 2 ** 20
NEG = -1e30
LOG2E = math.log2(math.e)

TM_PROMPT = 512
BT_SAMPLE = 8
BT_EVEN_SAMPLE = 16
FF_CHUNK = 512
MEMKV_BT = 2


def _params(n_axes):
    return pltpu.CompilerParams(dimension_semantics=("arbitrary",) * n_axes,
                                vmem_limit_bytes=VMEM_LIMIT)


def _const(shape):
    nd = len(shape)
    return pl.BlockSpec(shape, lambda *_: (0,) * nd, pipeline_mode=pl.Buffered(1))


def _layer(layer, shape):
    nd = len(shape)
    return pl.BlockSpec((None,) + tuple(shape), lambda *_: (layer,) + (0,) * nd,
                        pipeline_mode=pl.Buffered(1))


def _rms(x, g):
    return x * lax.rsqrt(jnp.mean(x * x, axis=-1, keepdims=True) + EPS) * g


def _dot(a, b):
    return jnp.dot(a, b, preferred_element_type=F32)


def _dot_nt(a, b):
    return lax.dot_general(a, b, (((1,), (1,)), ((), ())), preferred_element_type=F32)


def _dot_tn(a, b):
    return lax.dot_general(a, b, (((0,), (0,)), ((), ())), preferred_element_type=F32)


def _ret_rope_tables(pos):
    half = RET_DK // 2
    inv = RET_ROPE_BASE ** (-np.arange(half, dtype=np.float64) / half)
    ang = pos.astype(np.float64)[:, None] * inv[None, :]
    cos, sin = np.cos(ang), np.sin(ang)
    return (jnp.asarray(np.concatenate([cos, cos], -1), F32),
            jnp.asarray(np.concatenate([-sin, sin], -1), F32))


def _win_rope_tables(pos):
    half = C_HEAD_DIM // 2
    inv = C_ROPE_BASE ** (-np.arange(half, dtype=np.float64) / half)
    ang = pos.astype(np.float64)[:, None] * inv[None, :]
    lane = np.arange(LANES)
    cos = np.cos(ang)[:, lane % half]
    sin = np.sin(ang)[:, lane % half]
    low = (lane % C_HEAD_DIM) < half
    return (jnp.asarray(cos, F32),
            jnp.asarray(np.where(low[None], -sin, 0.0), F32),
            jnp.asarray(np.where(low[None], 0.0, sin), F32))


def _ret_rotate(x, cosf, sinf):
    return x * cosf + pltpu.roll(x, RET_DK // 2, 1) * sinf


def _win_rotate(x, cos, sin_lo, sin_hi):
    return (x * cos + pltpu.roll(x, LANES - C_HEAD_DIM // 2, 1) * sin_lo
            + pltpu.roll(x, C_HEAD_DIM // 2, 1) * sin_hi)


def _log_gamma(h):
    return math.log1p(-2.0 ** (-5.0 - h))


def _memkv_kernel(mem_ref, g_ref, wk_ref, wv_ref, k_ref, v_ref, wk_s, wv_s):
    @pl.when(pl.program_id(1) == 0)
    def _():
        wk_s[...] = wk_ref[0].astype(BF16)
        wv_s[...] = wv_ref[0].astype(BF16)

    nb = mem_ref.shape[0]
    m = _rms(mem_ref[...].reshape(nb * N_MEM, D_MODEL), g_ref[0]).astype(BF16)
    k_ref[0] = _dot(m, wk_s[...]).reshape(nb, N_MEM, D_MODEL)
    v_ref[0] = _dot(m, wv_s[...]).reshape(nb, N_MEM, D_MODEL)


def _memkv(mem, g_mem, w_mk, w_mv):
    b = mem.shape[0]
    nb = MEMKV_BT
    out = jax.ShapeDtypeStruct((DEPTH, b, N_MEM, D_MODEL), F32)
    return pl.pallas_call(
        _memkv_kernel,
        grid=(DEPTH, b // nb),
        in_specs=[pl.BlockSpec((nb, N_MEM, D_MODEL), lambda l, i: (i, 0, 0)),
                  pl.BlockSpec((1, 1, D_MODEL), lambda l, i: (l, 0, 0)),
                  pl.BlockSpec((1, D_MODEL, D_MODEL), lambda l, i: (l, 0, 0)),
                  pl.BlockSpec((1, D_MODEL, D_MODEL), lambda l, i: (l, 0, 0))],
        out_specs=[pl.BlockSpec((1, nb, N_MEM, D_MODEL), lambda l, i: (l, i, 0, 0)),
                   pl.BlockSpec((1, nb, N_MEM, D_MODEL), lambda l, i: (l, i, 0, 0))],
        out_shape=[out, out],
        scratch_shapes=[pltpu.VMEM((D_MODEL, D_MODEL), BF16), pltpu.VMEM((D_MODEL, D_MODEL), BF16)],
        compiler_params=_params(2),
        name="memkv",
    )(mem, g_mem.reshape(DEPTH, 1, D_MODEL), w_mk, w_mv)


def _even_front(x, g_ref, win_ref, lng_ref, lnb_ref, cos_ref, sin_ref):
    h = _rms(x, g_ref[...]).astype(BF16)

    def proj(lo, hi):
        return _dot(h, win_ref[:, lo:hi])

    u = jax.nn.gelu(proj(0, 1024))
    v = jax.nn.gelu(proj(1024, 2048))
    vc = v - jnp.mean(v, axis=-1, keepdims=True)
    vn = vc * lax.rsqrt(jnp.mean(vc * vc, axis=-1, keepdims=True) + EPS) * lng_ref[...] + lnb_ref[...]
    q = proj(2048, 2560)
    k = proj(2560, 3072)
    vr = proj(3072, 4096)
    gate = proj(4096, 5120)
    cosf, sinf = cos_ref[...], sin_ref[...]
    qs, ks = [], []
    for hd in range(RET_HEADS):
        cols = slice(hd * RET_DK, (hd + 1) * RET_DK)
        qs.append(_ret_rotate(q[:, cols], cosf, sinf))
        ks.append(_ret_rotate(k[:, cols], cosf, sinf) * (RET_DK ** -0.5))
    return u, vn, qs, ks, vr, gate


def _head_norm_gate(o, gate):
    on = o * lax.rsqrt(jnp.mean(o * o, axis=-1, keepdims=True) + EPS)
    return on * jax.nn.silu(gate)


def _even_prompt_kernel(x_ref, g_ref, win_ref, lng_ref, lnb_ref, ws_ref, bs_ref, wout_ref,
                        cos_ref, sin_ref, xo_ref, sout_ref,
                        state, u_s, vn_s, q_s, k_s, vr_s, gate_s, cat_s):
    t = pl.program_id(1)
    tm = x_ref.shape[1]

    @pl.when(t == 0)
    def _():
        state[...] = jnp.zeros_like(state)

    x = x_ref[0]
    u, vn, qs, ks, vr, gate = _even_front(x, g_ref, win_ref, lng_ref, lnb_ref, cos_ref, sin_ref)
    u_s[...] = u
    vn_s[...] = vn.astype(BF16)
    for hd in range(RET_HEADS):
        cols = slice(hd * RET_DK, (hd + 1) * RET_DK)
        q_s[:, cols] = qs[hd].astype(BF16)
        k_s[cols, :] = ks[hd].T
    vr_s[...] = vr.astype(BF16)
    gate_s[...] = gate

    row = lax.broadcasted_iota(jnp.int32, (CHUNK, CHUNK), 0)
    col = lax.broadcasted_iota(jnp.int32, (CHUNK, CHUNK), 1)
    diff = (row - col).astype(F32)
    ridx = lax.broadcasted_iota(jnp.int32, (CHUNK, 1), 0).astype(F32)
    cidx = lax.broadcasted_iota(jnp.int32, (1, CHUNK), 1).astype(F32)
    decay, xi, zeta, cdec = [], [], [], []
    for hd in range(RET_HEADS):
        lg = _log_gamma(hd)
        decay.append(jnp.where(diff >= 0, jnp.exp(jnp.maximum(diff, 0.0) * lg), 0.0))
        xi.append(jnp.exp((ridx + 1.0) * lg))
        zeta.append(jnp.exp((CHUNK - 1.0 - cidx) * lg))
        cdec.append(math.exp(CHUNK * lg))

    def chunk(c):
        rows = pl.ds(c * CHUNK, CHUNK)
        for g in range(SGU_GROUPS):
            cols = slice(g * SGU_GROUP_DIM, (g + 1) * SGU_GROUP_DIM)
            mixed = _dot(ws_ref[g], vn_s[rows, cols]) + bs_ref[g]
            cat_s[rows, cols] = (u_s[rows, cols] * mixed).astype(BF16)
        for hd in range(RET_HEADS):
            kcols = slice(hd * RET_DK, (hd + 1) * RET_DK)
            vcols = slice(hd * RET_DV, (hd + 1) * RET_DV)
            qc = q_s[rows, kcols]
            kt = k_s[kcols, rows]
            vc = vr_s[rows, vcols]
            s_prev = state[hd]
            sc = _dot(qc, kt.astype(BF16)) * decay[hd]
            o = _dot(sc.astype(BF16), vc) + _dot(qc, s_prev.astype(BF16)) * xi[hd]
            state[hd] = s_prev * cdec[hd] + _dot((kt * zeta[hd]).astype(BF16), vc)
            ocols = slice(D_MODEL + hd * RET_DV, D_MODEL + (hd + 1) * RET_DV)
            cat_s[rows, ocols] = _head_norm_gate(o, gate_s[rows, vcols]).astype(BF16)

    for c in range(tm // CHUNK):
        chunk(c)
    xo_ref[0] = x + _dot(cat_s[...], wout_ref[...])

    @pl.when(t == pl.num_programs(1) - 1)
    def _():
        sout_ref[0] = state[...]


def _even_prompt(x, j, g, w_in, ln_g, ln_b, w_s, b_s, w_out, cosf, sinf):
    b, l, d = x.shape
    tm = TM_PROMPT
    return pl.pallas_call(
        _even_prompt_kernel,
        grid=(b, l // tm),
        in_specs=[pl.BlockSpec((1, tm, d), lambda i, t: (i, t, 0)),
                  _const((1, d)), _layer(j, (d, EVEN_IN)), _const((1, d)), _const((1, d)),
                  _const((SGU_GROUPS, CHUNK, CHUNK)), _const((SGU_GROUPS, CHUNK, 1)),
                  _layer(j, (EVEN_OUT, d)),
                  pl.BlockSpec((tm, RET_DK), lambda i, t: (t, 0)),
                  pl.BlockSpec((tm, RET_DK), lambda i, t: (t, 0))],
        out_specs=[pl.BlockSpec((1, tm, d), lambda i, t: (i, t, 0)),
                   pl.BlockSpec((1, RET_HEADS, RET_DK, RET_DV), lambda i, t: (i, 0, 0, 0))],
        out_shape=[jax.ShapeDtypeStruct((b, l, d), F32),
                   jax.ShapeDtypeStruct((b, RET_HEADS, RET_DK, RET_DV), F32)],
        scratch_shapes=[pltpu.VMEM((RET_HEADS, RET_DK, RET_DV), F32),
                        pltpu.VMEM((tm, d), F32), pltpu.VMEM((tm, d), BF16),
                        pltpu.VMEM((tm, RET_HEADS * RET_DK), BF16),
                        pltpu.VMEM((RET_HEADS * RET_DK, tm), F32),
                        pltpu.VMEM((tm, d), BF16), pltpu.VMEM((tm, d), F32),
                        pltpu.VMEM((tm, EVEN_OUT), BF16)],
        compiler_params=_params(2),
        name="even_prompt",
    )(x, g, w_in, ln_g, ln_b, w_s, b_s, w_out, cosf, sinf)


def _even_sample_kernel(x_ref, s0_ref, g_ref, win_ref, lng_ref, lnb_ref, ws_ref, bs_ref, wout_ref,
                        cos_ref, sin_ref, xo_ref, sout_ref, vrow_ref,
                        q_s, kz_s, v_s, o_s):
    tt = x_ref.shape[0]
    ls = tt // s0_ref.shape[0]
    x = x_ref[...]
    u, vn, qs, ks, vr, gate = _even_front(x, g_ref, win_ref, lng_ref, lnb_ref, cos_ref, sin_ref)
    vrow_ref[...] = vn
    vn_b = vn.astype(BF16)
    vr_b = vr.astype(BF16)

    row = lax.broadcasted_iota(jnp.int32, (tt, tt), 0)
    col = lax.broadcasted_iota(jnp.int32, (tt, tt), 1)
    same = (row // ls) == (col // ls)
    diff = (row - col).astype(F32)
    rloc = (lax.broadcasted_iota(jnp.int32, (tt, 1), 0) % ls).astype(F32)

    cat = []
    for g in range(SGU_GROUPS):
        cols = slice(g * SGU_GROUP_DIM, (g + 1) * SGU_GROUP_DIM)
        mixed = _dot(ws_ref[g], vn_b[:, cols]) + bs_ref[g]
        cat.append((u[:, cols] * mixed).astype(BF16))

    xi, cdec = [], []
    for hd in range(RET_HEADS):
        lg = _log_gamma(hd)
        kcols = slice(hd * RET_DK, (hd + 1) * RET_DK)
        vcols = slice(hd * RET_DV, (hd + 1) * RET_DV)
        decay = jnp.where(same & (diff >= 0), jnp.exp(jnp.maximum(diff, 0.0) * lg), 0.0)
        xi.append(jnp.exp((rloc[:ls] + 1.0) * lg))
        cdec.append(math.exp(ls * lg))
        zeta = jnp.exp((ls - 1.0 - rloc) * lg)
        sc = _dot_nt(qs[hd].astype(BF16), ks[hd].astype(BF16)) * decay
        o_s[:, vcols] = _dot(sc.astype(BF16), vr_b[:, vcols])
        q_s[:, kcols] = qs[hd]
        kz_s[:, kcols] = ks[hd] * zeta
    v_s[...] = vr

    def per_batch(bi, carry):
        rows = pl.ds(pl.multiple_of(bi * ls, ls), ls)
        for hd in range(RET_HEADS):
            kcols = slice(hd * RET_DK, (hd + 1) * RET_DK)
            vcols = slice(hd * RET_DV, (hd + 1) * RET_DV)
            s_prev = s0_ref[bi, hd]
            qb = q_s[rows, kcols].astype(BF16)
            o_s[rows, vcols] += _dot(qb, s_prev.astype(BF16)) * xi[hd]
            sout_ref[bi, hd] = s_prev * cdec[hd] + _dot_tn(kz_s[rows, kcols].astype(BF16),
                                                          v_s[rows, vcols].astype(BF16))
        return carry

    lax.fori_loop(0, s0_ref.shape[0], per_batch, 0, unroll=2)
    for hd in range(RET_HEADS):
        vcols = slice(hd * RET_DV, (hd + 1) * RET_DV)
        cat.append(_head_norm_gate(o_s[:, vcols], gate[:, vcols]).astype(BF16))
    xo_ref[...] = x + _dot(jnp.concatenate(cat, axis=-1), wout_ref[...])


def _even_sample(x, s0, j, g, w_in, ln_g, ln_b, w_s_bd, b_s_bd, w_out, cosf, sinf):
    n, d = x.shape
    nb = s0.shape[0]
    ls = n // nb
    bt = BT_EVEN_SAMPLE
    tt = bt * ls
    state_spec = pl.BlockSpec((bt, RET_HEADS, RET_DK, RET_DV), lambda i: (i, 0, 0, 0))
    return pl.pallas_call(
        _even_sample_kernel,
        grid=(nb // bt,),
        in_specs=[pl.BlockSpec((tt, d), lambda i: (i, 0)), state_spec,
                  _const((1, d)), _layer(j, (d, EVEN_IN)), _const((1, d)), _const((1, d)),
                  _const((SGU_GROUPS, tt, tt)), _const((SGU_GROUPS, tt, 1)),
                  _layer(j, (EVEN_OUT, d)), _const((tt, RET_DK)), _const((tt, RET_DK))],
        out_specs=[pl.BlockSpec((tt, d), lambda i: (i, 0)), state_spec,
                   pl.BlockSpec((tt, d), lambda i: (i, 0))],
        out_shape=[jax.ShapeDtypeStruct((n, d), F32),
                   jax.ShapeDtypeStruct(s0.shape, F32),
                   jax.ShapeDtypeStruct((n, d), F32)],
        scratch_shapes=[pltpu.VMEM((tt, RET_HEADS * RET_DK), F32),
                        pltpu.VMEM((tt, RET_HEADS * RET_DK), F32),
                        pltpu.VMEM((tt, d), F32), pltpu.VMEM((tt, d), F32)],
        compiler_params=_params(1),
        name="even_sample",
    )(x, s0, g, w_in, ln_g, ln_b, w_s_bd, b_s_bd, w_out, cosf, sinf)


def _pad_even_odd(blk, kv_in_block):
    lane = lax.broadcasted_iota(jnp.int32, blk.shape, 1)
    rolled = pltpu.roll(blk, C_HEAD_DIM, 1)
    if kv_in_block == 0:
        even = jnp.where(lane < C_HEAD_DIM, blk, 0.0)
        odd = jnp.where(lane >= C_HEAD_DIM, rolled, 0.0)
    else:
        odd = jnp.where(lane >= C_HEAD_DIM, blk, 0.0)
        even = jnp.where(lane < C_HEAD_DIM, rolled, 0.0)
    return even.astype(BF16), odd.astype(BF16)


def _pair_normalise(outs, dens):
    lane = lax.broadcasted_iota(jnp.int32, outs[0].shape, 1)
    inv = jnp.where(lane < C_HEAD_DIM, 1.0 / dens[0], 1.0 / dens[1])
    return (outs[0] + outs[1]) * inv


def _window_attend(q2, ke, ko, ve, vo, valid, sink_e, sink_o):
    outs, dens = [], []
    for kk, vv, sink in ((ke, ve, sink_e), (ko, vo, sink_o)):
        s = jnp.where(valid, _dot_nt(q2, kk), NEG)
        m = jnp.maximum(jnp.max(s, axis=-1, keepdims=True), sink)
        p = jnp.exp2(s - m)
        dens.append(jnp.sum(p, axis=-1, keepdims=True) + jnp.exp2(sink - m))
        outs.append(_dot(p.astype(BF16), vv))
    return _pair_normalise(outs, dens)


def _banded_attend(q2, ke, ko, ve, vo, upper, prev_bias, sink_e, sink_o):
    w = upper.shape[1]
    zero = jnp.zeros((), F32)
    outs, dens = [], []
    for kk, vv, sink in ((ke, ve, sink_e), (ko, vo, sink_o)):
        s2 = _dot(q2, kk)
        s_prev = s2[:, :w] if prev_bias is None else s2[:, :w] + prev_bias
        s = jnp.where(upper, s_prev, s2[:, w:])
        m = jnp.maximum(jnp.max(s, axis=-1, keepdims=True), sink)
        p = jnp.exp2(s - m)
        dens.append(jnp.sum(p, axis=-1, keepdims=True) + jnp.exp2(sink - m))
        pp = jnp.concatenate([jnp.where(upper, p, zero), jnp.where(upper, zero, p)], axis=1)
        outs.append(_dot(pp.astype(BF16), vv))
    return _pair_normalise(outs, dens)


def _odd_front(x, g_ref, wqkv_ref, bqkv_ref, cos_ref, lo_ref, hi_ref):
    h = _rms(x, g_ref[...]).astype(BF16)
    qkv = _dot(h, wqkv_ref[...]) + bqkv_ref[...]
    cos, lo, hi = cos_ref[...], lo_ref[...], hi_ref[...]
    nq = C_HEADS * C_HEAD_DIM
    nk = C_KV_HEADS * C_HEAD_DIM
    qb = [_win_rotate(qkv[:, j * LANES:(j + 1) * LANES], cos, lo, hi) * (C_HEAD_DIM ** -0.5 * LOG2E)
          for j in range(nq // LANES)]
    kb = [_win_rotate(qkv[:, nq + j * LANES:nq + (j + 1) * LANES], cos, lo, hi)
          for j in range(nk // LANES)]
    v = qkv[:, nq + nk:]
    return qb, kb, v


def _odd_prompt_kernel(sink_ref, x_ref, g_ref, wqkv_ref, bqkv_ref, wout_ref, bout_ref,
                       cos_ref, lo_ref, hi_ref, xo_ref, wk_ref, wv_ref,
                       q_s, kpad, kprev, vpad, att_s):
    t = pl.program_id(1)
    tm = x_ref.shape[1]
    w = CHUNK
    x = x_ref[0]
    qb, kb, v = _odd_front(x, g_ref, wqkv_ref, bqkv_ref, cos_ref, lo_ref, hi_ref)
    for j, blk in enumerate(qb):
        q_s[:, j * LANES:(j + 1) * LANES] = blk.astype(BF16)

    @pl.when(t == 0)
    def _():
        kprev[...] = jnp.zeros((2 * C_KV_HEADS, LANES, w), BF16)
        vpad[:, 0:w, :] = jnp.zeros((2 * C_KV_HEADS, w, LANES), BF16)

    kt = [blk.T for blk in kb]
    k_tails = []
    for kh in range(C_KV_HEADS):
        j, r = divmod(kh, LANES // C_HEAD_DIM)
        head = kt[j][r * C_HEAD_DIM:(r + 1) * C_HEAD_DIM, :]
        zeros = jnp.zeros_like(head)
        ke_t = jnp.concatenate([head, zeros], axis=0).astype(BF16)
        ko_t = jnp.concatenate([zeros, head], axis=0).astype(BF16)
        kpad[2 * kh] = ke_t
        kpad[2 * kh + 1] = ko_t
        k_tails += [ke_t[:, tm - w:], ko_t[:, tm - w:]]
        ve, vo = _pad_even_odd(v[:, j * LANES:(j + 1) * LANES], r)
        vpad[2 * kh, w:, :] = ve
        vpad[2 * kh + 1, w:, :] = vo

    @pl.when(t == pl.num_programs(1) - 1)
    def _():
        for j, blk in enumerate(kb):
            wk_ref[0, :, j * LANES:(j + 1) * LANES] = blk[tm - w:, :]
        wv_ref[0] = v[tm - w:, :]

    qi = lax.broadcasted_iota(jnp.int32, (2 * w, w), 0) % w
    kj = lax.broadcasted_iota(jnp.int32, (2 * w, w), 1)
    upper = kj > qi
    first_half = lax.broadcasted_iota(jnp.int32, (2 * w, 1), 0) < w

    def block(c):
        rows = pl.ds(c * w, w)
        keys = pl.ds(c * w, 2 * w)
        kkeys = pl.ds((c - 1) * w, 2 * w)
        prev_bias = jnp.where(t == 0, NEG, 0.0) if c == 0 else None
        for kh in range(C_KV_HEADS):
            if c == 0:
                ke = jnp.concatenate([kprev[2 * kh], kpad[2 * kh, :, 0:w]], axis=1)
                ko = jnp.concatenate([kprev[2 * kh + 1], kpad[2 * kh + 1, :, 0:w]], axis=1)
            else:
                ke, ko = kpad[2 * kh, :, kkeys], kpad[2 * kh + 1, :, kkeys]
            ve, vo = vpad[2 * kh, keys, :], vpad[2 * kh + 1, keys, :]
            c0 = (2 * kh) * LANES
            q2 = jnp.concatenate([q_s[rows, c0:c0 + LANES], q_s[rows, c0 + LANES:c0 + 2 * LANES]], 0)
            h0 = kh * (C_HEADS // C_KV_HEADS)
            sink_e = jnp.where(first_half, sink_ref[h0], sink_ref[h0 + 2]) * LOG2E
            sink_o = jnp.where(first_half, sink_ref[h0 + 1], sink_ref[h0 + 3]) * LOG2E
            o = _banded_attend(q2, ke, ko, ve, vo, upper, prev_bias, sink_e, sink_o)
            att_s[rows, c0:c0 + LANES] = o[:w].astype(BF16)
            att_s[rows, c0 + LANES:c0 + 2 * LANES] = o[w:].astype(BF16)

    for c in range(tm // w):
        block(c)
    xo_ref[0] = x + _dot(att_s[...], wout_ref[...]) + bout_ref[...]
    for i, tail in enumerate(k_tails):
        kprev[i] = tail
    vpad[:, 0:w, :] = vpad[:, tm:tm + w, :]


def _odd_prompt(x, j, sinks, g, w_qkv, b_qkv, w_out, b_out, cos, lo, hi):
    b, l, d = x.shape
    tm = TM_PROMPT
    kvw = C_KV_HEADS * C_HEAD_DIM
    win_spec = pl.BlockSpec((1, CHUNK, kvw), lambda i, t: (i, 0, 0))
    tab = pl.BlockSpec((tm, LANES), lambda i, t: (t, 0))
    return pl.pallas_call(
        _odd_prompt_kernel,
        grid=(b, l // tm),
        in_specs=[pl.BlockSpec(memory_space=pltpu.SMEM),
                  pl.BlockSpec((1, tm, d), lambda i, t: (i, t, 0)),
                  _const((1, d)), _layer(j, (d, ODD_IN)), _const((1, ODD_IN)),
                  _layer(j, (d, d)), _const((1, d)), tab, tab, tab],
        out_specs=[pl.BlockSpec((1, tm, d), lambda i, t: (i, t, 0)), win_spec, win_spec],
        out_shape=[jax.ShapeDtypeStruct((b, l, d), F32),
                   jax.ShapeDtypeStruct((b, CHUNK, kvw), F32),
                   jax.ShapeDtypeStruct((b, CHUNK, kvw), F32)],
        scratch_shapes=[pltpu.VMEM((tm, d), BF16),
                        pltpu.VMEM((2 * C_KV_HEADS, LANES, tm), BF16),
                        pltpu.VMEM((2 * C_KV_HEADS, LANES, CHUNK), BF16),
                        pltpu.VMEM((2 * C_KV_HEADS, tm + CHUNK, LANES), BF16),
                        pltpu.VMEM((tm, d), BF16)],
        compiler_params=_params(2),
        name="odd_prompt",
    )(sinks, x, g, w_qkv, b_qkv, w_out, b_out, cos, lo, hi)


def _odd_sample_kernel(sink_ref, x_ref, ck_ref, cv_ref, g_ref, wqkv_ref, bqkv_ref, wout_ref, bout_ref,
                       cos_ref, lo_ref, hi_ref, xo_ref, wk_ref, wv_ref):
    tt = x_ref.shape[0]
    bt, w, kvw = ck_ref.shape
    ls = tt // bt
    x = x_ref[...]
    qb, kb, v = _odd_front(x, g_ref, wqkv_ref, bqkv_ref, cos_ref, lo_ref, hi_ref)
    knew = jnp.concatenate(kb, axis=-1)
    ck = ck_ref[...]
    cv = cv_ref[...]
    for bi in range(bt):
        wk_ref[bi, 0:w - ls, :] = ck[bi, ls:, :]
        wk_ref[bi, w - ls:, :] = knew[bi * ls:(bi + 1) * ls, :]
        wv_ref[bi, 0:w - ls, :] = cv[bi, ls:, :]
        wv_ref[bi, w - ls:, :] = v[bi * ls:(bi + 1) * ls, :]
    kall = jnp.concatenate([ck.reshape(bt * w, kvw), knew], axis=0)
    vall = jnp.concatenate([cv.reshape(bt * w, kvw), v], axis=0)
    ns = bt * w + tt

    tok = lax.broadcasted_iota(jnp.int32, (2 * tt, ns), 0) % tt
    key = lax.broadcasted_iota(jnp.int32, (2 * tt, ns), 1)
    tb, tl = tok // ls, tok % ls
    cached = key < bt * w
    nkey = key - bt * w
    valid = ((cached & ((key // w) == tb) & ((key % w) > tl))
             | ((key >= bt * w) & ((nkey // ls) == tb) & ((nkey % ls) <= tl)))
    first_half = lax.broadcasted_iota(jnp.int32, (2 * tt, 1), 0) < tt

    att = []
    for kh in range(C_KV_HEADS):
        j, r = divmod(kh, LANES // C_HEAD_DIM)
        ke, ko = _pad_even_odd(kall[:, j * LANES:(j + 1) * LANES], r)
        ve, vo = _pad_even_odd(vall[:, j * LANES:(j + 1) * LANES], r)
        q2 = jnp.concatenate([qb[2 * kh], qb[2 * kh + 1]], axis=0).astype(BF16)
        h0 = kh * (C_HEADS // C_KV_HEADS)
        sink_e = jnp.where(first_half, sink_ref[h0], sink_ref[h0 + 2]) * LOG2E
        sink_o = jnp.where(first_half, sink_ref[h0 + 1], sink_ref[h0 + 3]) * LOG2E
        o = _window_attend(q2, ke, ko, ve, vo, valid, sink_e, sink_o)
        att += [o[:tt].astype(BF16), o[tt:].astype(BF16)]
    xo_ref[...] = x + _dot(jnp.concatenate(att, axis=-1), wout_ref[...]) + bout_ref[...]


def _odd_sample(x, ck, cv, j, sinks, g, w_qkv, b_qkv, w_out, b_out, cos, lo, hi):
    n, d = x.shape
    nb, w, kvw = ck.shape
    ls = n // nb
    bt = BT_SAMPLE
    tt = bt * ls
    cache_spec = pl.BlockSpec((bt, w, kvw), lambda i: (i, 0, 0))
    return pl.pallas_call(
        _odd_sample_kernel,
        grid=(nb // bt,),
        in_specs=[pl.BlockSpec(memory_space=pltpu.SMEM),
                  pl.BlockSpec((tt, d), lambda i: (i, 0)), cache_spec, cache_spec,
                  _const((1, d)), _layer(j, (d, ODD_IN)), _const((1, ODD_IN)),
                  _layer(j, (d, d)), _const((1, d)),
                  _const((tt, LANES)), _const((tt, LANES)), _const((tt, LANES))],
        out_specs=[pl.BlockSpec((tt, d), lambda i: (i, 0)), cache_spec, cache_spec],
        out_shape=[jax.ShapeDtypeStruct((n, d), F32),
                   jax.ShapeDtypeStruct(ck.shape, F32), jax.ShapeDtypeStruct(cv.shape, F32)],
        compiler_params=_params(1),
        name="odd_sample",
    )(sinks, x, ck, cv, g, w_qkv, b_qkv, w_out, b_out, cos, lo, hi)


def _cross_heads(q, mk, mv):
    outs = []
    for hd in range(MEM_HEADS):
        cols = slice(hd * MEM_HEAD_DIM, (hd + 1) * MEM_HEAD_DIM)
        s = _dot_nt(q[:, cols], mk[:, cols])
        p = jnp.exp(s - jnp.max(s, axis=-1, keepdims=True))
        den = jnp.sum(p, axis=-1, keepdims=True)
        outs.append(_dot(p.astype(BF16), mv[:, cols]) * (1.0 / den))
    return outs


def _mlp(x1, gf_ref, wup_ref, wdown_ref):
    h2 = _rms(x1, gf_ref[...]).astype(BF16)
    acc = x1
    for j in range(D_FF // D_MODEL):
        sl = slice(j * D_MODEL, (j + 1) * D_MODEL)
        a = jnp.square(jnp.maximum(_dot(h2, wup_ref[:, sl]), 0.0)).astype(BF16)
        acc = acc + _dot(a, wdown_ref[sl, :])
    return acc


def _cross_mlp_prompt_kernel(x_ref, mk_ref, mv_ref, gc_ref, wmq_ref, wmo_ref, gf_ref, wup_ref,
                             wdown_ref, gfin_ref, xo_ref, o_s, *, final):
    x = x_ref[0]
    h = _rms(x, gc_ref[...]).astype(BF16)
    q = (_dot(h, wmq_ref[...]) * (MEM_HEAD_DIM ** -0.5)).astype(BF16)
    outs = _cross_heads(q, mk_ref[0, 0].astype(BF16), mv_ref[0, 0].astype(BF16))
    for hd, o in enumerate(outs):
        o_s[:, hd * MEM_HEAD_DIM:(hd + 1) * MEM_HEAD_DIM] = o.astype(BF16)
    x1 = x + _dot(o_s[...], wmo_ref[...])
    y = _mlp(x1, gf_ref, wup_ref, wdown_ref)
    if final:
        y = _rms(y, gfin_ref[...])
    xo_ref[0] = y


def _cross_mlp_prompt(x, mk, mv, layer, g_cross, w_mq, w_mo, g_ffn, w_up, w_down, g_final, final):
    b, l, d = x.shape
    tm = TM_PROMPT
    mem_spec = pl.BlockSpec((1, 1, N_MEM, d), lambda i, t: (layer, i, 0, 0))
    return pl.pallas_call(
        functools.partial(_cross_mlp_prompt_kernel, final=final),
        grid=(b, l // tm),
        in_specs=[pl.BlockSpec((1, tm, d), lambda i, t: (i, t, 0)), mem_spec, mem_spec,
                  _const((1, d)), _layer(layer, (d, d)), _layer(layer, (d, d)), _const((1, d)),
                  _const((d, D_FF)), _const((D_FF, d)), _const((1, d))],
        out_specs=pl.BlockSpec((1, tm, d), lambda i, t: (i, t, 0)),
        out_shape=jax.ShapeDtypeStruct((b, l, d), F32),
        scratch_shapes=[pltpu.VMEM((tm, d), BF16)],
        compiler_params=_params(2),
        name=f"cross_mlp_prompt_{layer}",
    )(x, mk, mv, g_cross, w_mq, w_mo, g_ffn, w_up, w_down, g_final)


def _cross_sample_kernel(x_ref, mk_ref, mv_ref, gc_ref, wmq_ref, wmo_ref, xo_ref, q_s, o_s):
    tt = x_ref.shape[0]
    bt = mk_ref.shape[1]
    ls = tt // bt
    nrow = N_MEM * MEM_HEADS
    x = x_ref[...]
    h = _rms(x, gc_ref[...]).astype(BF16)
    q_s[...] = _dot(h, wmq_ref[...]) * (MEM_HEAD_DIM ** -0.5)
    qhead = lax.broadcasted_iota(jnp.int32, (MEM_HEADS * ls, nrow), 0) // ls
    khead = lax.broadcasted_iota(jnp.int32, (MEM_HEADS * ls, nrow), 1) % MEM_HEADS
    own = qhead == khead

    def per_batch(bi, carry):
        rows = pl.ds(pl.multiple_of(bi * ls, ls), ls)
        qb = q_s[rows, :]
        q4 = jnp.concatenate([qb[:, hd * MEM_HEAD_DIM:(hd + 1) * MEM_HEAD_DIM]
                              for hd in range(MEM_HEADS)], axis=0).astype(BF16)
        k2 = mk_ref[0, bi].reshape(nrow, MEM_HEAD_DIM).astype(BF16)
        v2 = mv_ref[0, bi].reshape(nrow, MEM_HEAD_DIM).astype(BF16)
        s = jnp.where(own, _dot_nt(q4, k2), NEG)
        p = jnp.exp(s - jnp.max(s, axis=-1, keepdims=True))
        den = jnp.sum(p, axis=-1, keepdims=True)
        o4 = _dot(p.astype(BF16), v2) * (1.0 / den)
        for hd in range(MEM_HEADS):
            o_s[rows, hd * MEM_HEAD_DIM:(hd + 1) * MEM_HEAD_DIM] = o4[hd * ls:(hd + 1) * ls]
        return carry

    lax.fori_loop(0, bt, per_batch, 0, unroll=4)
    xo_ref[...] = x + _dot(o_s[...].astype(BF16), wmo_ref[...])


def _cross_sample(x, mk, mv, layer, g_cross, w_mq, w_mo):
    n, d = x.shape
    nb = mk.shape[1]
    ls = n // nb
    bt = BT_SAMPLE
    tt = bt * ls
    mem_spec = pl.BlockSpec((1, bt, N_MEM, MEM_HEADS, MEM_HEAD_DIM), lambda i: (layer, i, 0, 0, 0))
    return pl.pallas_call(
        _cross_sample_kernel,
        grid=(nb // bt,),
        in_specs=[pl.BlockSpec((tt, d), lambda i: (i, 0)), mem_spec, mem_spec,
                  _const((1, d)), _layer(layer, (d, d)), _layer(layer, (d, d))],
        out_specs=pl.BlockSpec((tt, d), lambda i: (i, 0)),
        out_shape=jax.ShapeDtypeStruct((n, d), F32),
        scratch_shapes=[pltpu.VMEM((tt, d), F32), pltpu.VMEM((tt, d), F32)],
        compiler_params=_params(1),
        name=f"cross_sample_{layer}",
    )(x, mk, mv, g_cross, w_mq, w_mo)


def _mlp_stream_kernel(x_ref, gf_ref, wup_ref, wdown_ref, gfin_ref, xo_ref, wup_b_ref, wdown_b_ref,
                       h_s, acc_s, *, final):
    j = pl.program_id(0)

    @pl.when(j == 0)
    def _():
        x = x_ref[...]
        h_s[...] = _rms(x, gf_ref[...]).astype(BF16)
        acc_s[...] = x

    wu = wup_ref[...].astype(BF16)
    wd = wdown_ref[...].astype(BF16)
    wup_b_ref[...] = wu
    wdown_b_ref[...] = wd
    a = jnp.square(jnp.maximum(_dot(h_s[...], wu), 0.0)).astype(BF16)
    acc_s[...] += _dot(a, wd)

    @pl.when(j == pl.num_programs(0) - 1)
    def _():
        y = acc_s[...]
        if final:
            y = _rms(y, gfin_ref[...])
        xo_ref[...] = y


def _mlp_sample(x, layer, g_ffn, w_up, w_down, g_final, final):
    n, d = x.shape
    fc = FF_CHUNK
    return pl.pallas_call(
        functools.partial(_mlp_stream_kernel, final=final),
        grid=(D_FF // fc,),
        in_specs=[_const((n, d)), _const((1, d)),
                  pl.BlockSpec((None, d, fc), lambda c: (layer, 0, c)),
                  pl.BlockSpec((None, fc, d), lambda c: (layer, c, 0)),
                  _const((1, d))],
        out_specs=[pl.BlockSpec((n, d), lambda c: (0, 0)),
                   pl.BlockSpec((d, fc), lambda c: (0, c)),
                   pl.BlockSpec((fc, d), lambda c: (c, 0))],
        out_shape=[jax.ShapeDtypeStruct((n, d), F32),
                   jax.ShapeDtypeStruct((d, D_FF), BF16),
                   jax.ShapeDtypeStruct((D_FF, d), BF16)],
        scratch_shapes=[pltpu.VMEM((n, d), BF16), pltpu.VMEM((n, d), F32)],
        compiler_params=_params(1),
        name=f"mlp_sample_{layer}",
    )(x, g_ffn, w_up, w_down, g_final)


def kernel(x_prompt, x_sample, state_ret, cache_win_k, cache_win_v, cache_mem_k, cache_mem_v, mem_prompt, g_mix, w_in_e, sgu_ln_g, sgu_ln_b, w_spatial, b_spatial, w_out_e, w_qkv_o, b_qkv_o, sinks, w_out_o, b_out_o, g_cross, g_mem, w_mq, w_mk, w_mv, w_mo, g_ffn, w_up, w_down, g_final):
    bp, lp, d = x_prompt.shape
    bs, ls, _ = x_sample.shape
    assert d == D_MODEL and lp % TM_PROMPT == 0 and bs % BT_SAMPLE == 0 and bs % BT_EVEN_SAMPLE == 0
    assert ls < CHUNK and CHUNK % ls == 0 and D_FF % FF_CHUNK == 0 and bp % MEMKV_BT == 0
    ns = bs * ls
    kvw = C_KV_HEADS * C_HEAD_DIM
    tt = BT_SAMPLE * ls

    def row(v):
        return v.reshape(1, -1).astype(F32)

    def bf(w):
        return w.astype(BF16)

    pos_p = np.arange(lp)
    pos_s_tile = PAST_LEN + (np.arange(tt) % ls)
    ret_tab_p = _ret_rope_tables(pos_p)
    tt_e = BT_EVEN_SAMPLE * ls
    ret_tab_s = _ret_rope_tables(PAST_LEN + (np.arange(tt_e) % ls))
    win_tab_p = _win_rope_tables(pos_p)
    win_tab_s = _win_rope_tables(pos_s_tile)

    w_in_b, w_out_e_b, w_qkv_b, w_out_o_b = bf(w_in_e), bf(w_out_e), bf(w_qkv_o), bf(w_out_o)
    w_mq_b, w_mo_b = bf(w_mq), bf(w_mo)

    mk_p, mv_p = _memkv(mem_prompt, g_mem, w_mk, w_mv)
    mk_s, mv_s = cache_mem_k, cache_mem_v

    xp = x_prompt
    xs = x_sample.reshape(ns, d)
    final_g = row(g_final)
    outs = {}
    for layer in range(DEPTH):
        j = layer // 2
        g = row(g_mix[layer])
        if layer % 2 == 0:
            ln_g, ln_b = row(sgu_ln_g[j]), row(sgu_ln_b[j])
            tril = jnp.tril(jnp.ones((CHUNK, CHUNK), bool))
            w_s = bf(jnp.where(tril[None], w_spatial[j], 0.0))
            b_s = b_spatial[j].reshape(SGU_GROUPS, CHUNK, 1)
            xp, ret_p = _even_prompt(xp, j, g, w_in_b, ln_g, ln_b, w_s, b_s, w_out_e_b, *ret_tab_p)
            w_small = w_s[:, :ls, :ls]
            eye = jnp.eye(BT_EVEN_SAMPLE, dtype=BF16)
            w_bd = jnp.einsum("ab,gts->gatbs", eye, w_small).reshape(SGU_GROUPS, tt_e, tt_e)
            b_bd = jnp.tile(b_spatial[j][:, :ls], (1, BT_EVEN_SAMPLE)).reshape(SGU_GROUPS, tt_e, 1)
            xs, ret_s, v_rows = _even_sample(xs, state_ret[j], j, g, w_in_b, ln_g, ln_b, w_bd, b_bd,
                                             w_out_e_b, *ret_tab_s)
            outs["ret_p"], outs["ret_s"], outs["v_rows"] = ret_p, ret_s, v_rows
        else:
            b_qkv, b_out = row(b_qkv_o[j]), row(b_out_o[j])
            xp, wk_p, wv_p = _odd_prompt(xp, j, sinks[j], g, w_qkv_b, b_qkv, w_out_o_b, b_out,
                                         *win_tab_p)
            xs, wk_s, wv_s = _odd_sample(xs, cache_win_k[j].reshape(bs, CHUNK, kvw),
                                         cache_win_v[j].reshape(bs, CHUNK, kvw), j, sinks[j],
                                         g, w_qkv_b, b_qkv, w_out_o_b, b_out, *win_tab_s)
            outs["wk_p"], outs["wv_p"], outs["wk_s"], outs["wv_s"] = wk_p, wv_p, wk_s, wv_s
        final = layer == DEPTH - 1
        gc, gf = row(g_cross[layer]), row(g_ffn[layer])
        xs = _cross_sample(xs, mk_s, mv_s, layer, gc, w_mq_b, w_mo_b)
        xs, w_up_l, w_down_l = _mlp_sample(xs, layer, gf, w_up, w_down, final_g, final)
        xp = _cross_mlp_prompt(xp, mk_p, mv_p, layer, gc, w_mq_b, w_mo_b, gf, w_up_l, w_down_l,
                               final_g, final)

    n_even, n_odd = (DEPTH + 1) // 2, DEPTH // 2
    return (xp,
            xs.reshape(bs, ls, d),
            mk_p.reshape(DEPTH, bp, N_MEM, MEM_HEADS, MEM_HEAD_DIM),
            mv_p.reshape(DEPTH, bp, N_MEM, MEM_HEADS, MEM_HEAD_DIM),
            outs["ret_p"].reshape(n_even, bp, RET_HEADS, RET_DK, RET_DV),
            outs["ret_s"].reshape(n_even, bs, RET_HEADS, RET_DK, RET_DV),
            outs["v_rows"].reshape(n_even, bs, ls, SGU_GROUPS, SGU_GROUP_DIM),
            outs["wk_p"].reshape(n_odd, bp, CHUNK, C_KV_HEADS, C_HEAD_DIM),
            outs["wv_p"].reshape(n_odd, bp, CHUNK, C_KV_HEADS, C_HEAD_DIM),
            outs["wk_s"].reshape(n_odd, bs, CHUNK, C_KV_HEADS, C_HEAD_DIM),
            outs["wv_s"].reshape(n_odd, bs, CHUNK, C_KV_HEADS, C_HEAD_DIM))
```

```python
import functools
import math

import numpy as np
import jax
import jax.numpy as jnp
from jax import lax
from jax.experimental import pallas as pl
from jax.experimental.pallas import tpu as pltpu

F32 = jnp.float32
BF16 = jnp.bfloat16

D_MODEL = 1024
DEPTH = 2
PAST_LEN = 16384
EPS = 1e-6

SGU_GROUPS = 4
SGU_GROUP_DIM = 256
RET_HEADS = 4
RET_DK = 128
RET_DV = 256
RET_ROPE_BASE = 10000.0
CHUNK = 128
C_HEADS = 16
C_KV_HEADS = 4
C_HEAD_DIM = 64
C_ROPE_BASE = 150000.0
N_MEM = 256
MEM_HEADS = 4
MEM_HEAD_DIM = 256
D_FF = 4096
EVEN_IN = 5120
EVEN_OUT = 2048
ODD_IN = 1536

LANES = 128
VMEM_LIMIT = 56 * 2 ** 20
NEG = -1e30
LOG2E = math.log2(math.e)

TM_PROMPT = 512
TM_PROMPT_WIDE = 1024
BT_SAMPLE = 8
BT_EVEN_SAMPLE = 16
FF_CHUNK = 512
MEMKV_BT = 2


def _params(n_axes):
    return pltpu.CompilerParams(dimension_semantics=("arbitrary",) * n_axes,
                                vmem_limit_bytes=VMEM_LIMIT)


def _const(shape):
    nd = len(shape)
    return pl.BlockSpec(shape, lambda *_: (0,) * nd, pipeline_mode=pl.Buffered(1))


def _layer(layer, shape):
    nd = len(shape)
    return pl.BlockSpec((None,) + tuple(shape), lambda *_: (layer,) + (0,) * nd,
                        pipeline_mode=pl.Buffered(1))


def _rms(x, g):
    return x * lax.rsqrt(jnp.mean(x * x, axis=-1, keepdims=True) + EPS) * g


def _dot(a, b):
    return jnp.dot(a, b, preferred_element_type=F32)


def _dot_nt(a, b):
    return lax.dot_general(a, b, (((1,), (1,)), ((), ())), preferred_element_type=F32)


def _dot_tn(a, b):
    return lax.dot_general(a, b, (((0,), (0,)), ((), ())), preferred_element_type=F32)


def _ret_rope_tables(pos):
    half = RET_DK // 2
    inv = RET_ROPE_BASE ** (-np.arange(half, dtype=np.float64) / half)
    ang = pos.astype(np.float64)[:, None] * inv[None, :]
    cos, sin = np.cos(ang), np.sin(ang)
    return (jnp.asarray(np.concatenate([cos, cos], -1), F32),
            jnp.asarray(np.concatenate([-sin, sin], -1), F32))


def _win_rope_tables(pos):
    half = C_HEAD_DIM // 2
    inv = C_ROPE_BASE ** (-np.arange(half, dtype=np.float64) / half)
    ang = pos.astype(np.float64)[:, None] * inv[None, :]
    lane = np.arange(LANES)
    cos = np.cos(ang)[:, lane % half]
    sin = np.sin(ang)[:, lane % half]
    low = (lane % C_HEAD_DIM) < half
    return (jnp.asarray(cos, F32),
            jnp.asarray(np.where(low[None], -sin, 0.0), F32),
            jnp.asarray(np.where(low[None], 0.0, sin), F32))


def _ret_rotate(x, cosf, sinf):
    return x * cosf + pltpu.roll(x, RET_DK // 2, 1) * sinf


def _win_rotate(x, cos, sin_lo, sin_hi):
    return (x * cos + pltpu.roll(x, LANES - C_HEAD_DIM // 2, 1) * sin_lo
            + pltpu.roll(x, C_HEAD_DIM // 2, 1) * sin_hi)


def _log_gamma(h):
    return math.log1p(-2.0 ** (-5.0 - h))


def _memkv_kernel(mem_ref, g_ref, wk_ref, wv_ref, k_ref, v_ref, kb_ref, vb_ref, wk_s, wv_s):
    @pl.when(pl.program_id(1) == 0)
    def _():
        wk_s[...] = wk_ref[0].astype(BF16)
        wv_s[...] = wv_ref[0].astype(BF16)

    nb = mem_ref.shape[0]
    m = _rms(mem_ref[...].reshape(nb * N_MEM, D_MODEL), g_ref[0]).astype(BF16)
    for w_s, out_ref, bf_ref in ((wk_s, k_ref, kb_ref), (wv_s, v_ref, vb_ref)):
        kv = _dot(m, w_s[...])
        bf_ref[0] = kv.astype(BF16).reshape(nb, N_MEM, D_MODEL)
        for bi in range(nb):
            for hd in range(MEM_HEADS):
                out_ref[0, bi, :, hd, :] = kv[bi * N_MEM:(bi + 1) * N_MEM,
                                              hd * MEM_HEAD_DIM:(hd + 1) * MEM_HEAD_DIM]


def _memkv(mem, g_mem, w_mk, w_mv):
    b = mem.shape[0]
    nb = MEMKV_BT
    out = jax.ShapeDtypeStruct((DEPTH, b, N_MEM, MEM_HEADS, MEM_HEAD_DIM), F32)
    out_b = jax.ShapeDtypeStruct((DEPTH, b, N_MEM, D_MODEL), BF16)
    spec5 = pl.BlockSpec((1, nb, N_MEM, MEM_HEADS, MEM_HEAD_DIM), lambda l, i: (l, i, 0, 0, 0))
    spec_b = pl.BlockSpec((1, nb, N_MEM, D_MODEL), lambda l, i: (l, i, 0, 0))
    return pl.pallas_call(
        _memkv_kernel,
        grid=(DEPTH, b // nb),
        in_specs=[pl.BlockSpec((nb, N_MEM, D_MODEL), lambda l, i: (i, 0, 0)),
                  pl.BlockSpec((1, 1, D_MODEL), lambda l, i: (l, 0, 0)),
                  pl.BlockSpec((1, D_MODEL, D_MODEL), lambda l, i: (l, 0, 0)),
                  pl.BlockSpec((1, D_MODEL, D_MODEL), lambda l, i: (l, 0, 0))],
        out_specs=[spec5, spec5, spec_b, spec_b],
        out_shape=[out, out, out_b, out_b],
        scratch_shapes=[pltpu.VMEM((D_MODEL, D_MODEL), BF16), pltpu.VMEM((D_MODEL, D_MODEL), BF16)],
        compiler_params=_params(2),
        name="memkv",
    )(mem, g_mem.reshape(DEPTH, 1, D_MODEL), w_mk, w_mv)


def _even_front(x, g_ref, win_ref, lng_ref, lnb_ref, cos_ref, sin_ref):
    h = _rms(x, g_ref[...]).astype(BF16)

    def proj(lo, hi):
        return _dot(h, win_ref[:, lo:hi])

    u = jax.nn.gelu(proj(0, 1024))
    v = jax.nn.gelu(proj(1024, 2048))
    vc = v - jnp.mean(v, axis=-1, keepdims=True)
    vn = vc * lax.rsqrt(jnp.mean(vc * vc, axis=-1, keepdims=True) + EPS) * lng_ref[...] + lnb_ref[...]
    q = proj(2048, 2560)
    k = proj(2560, 3072)
    vr = proj(3072, 4096)
    gate = proj(4096, 5120)
    cosf, sinf = cos_ref[...], sin_ref[...]
    qs, ks = [], []
    for hd in range(RET_HEADS):
        cols = slice(hd * RET_DK, (hd + 1) * RET_DK)
        qs.append(_ret_rotate(q[:, cols], cosf, sinf))
        ks.append(_ret_rotate(k[:, cols], cosf, sinf) * (RET_DK ** -0.5))
    return u, vn, qs, ks, vr, gate


def _head_norm_gate(o, gate):
    on = o * lax.rsqrt(jnp.mean(o * o, axis=-1, keepdims=True) + EPS)
    return on * jax.nn.silu(gate)


def _even_prompt_kernel(x_ref, g_ref, win_ref, lng_ref, lnb_ref, ws_ref, bs_ref, wout_ref,
                        cos_ref, sin_ref, xo_ref, sout_ref,
                        state, u_s, vn_s, q_s, k_s, vr_s, gate_s, cat_s):
    t = pl.program_id(1)
    tm = x_ref.shape[1]

    @pl.when(t == 0)
    def _():
        state[...] = jnp.zeros_like(state)

    x = x_ref[0]
    u, vn, qs, ks, vr, gate = _even_front(x, g_ref, win_ref, lng_ref, lnb_ref, cos_ref, sin_ref)
    u_s[...] = u
    vn_s[...] = vn.astype(BF16)
    for hd in range(RET_HEADS):
        cols = slice(hd * RET_DK, (hd + 1) * RET_DK)
        q_s[:, cols] = qs[hd].astype(BF16)
        k_s[cols, :] = ks[hd].T
    vr_s[...] = vr.astype(BF16)
    gate_s[...] = gate

    row = lax.broadcasted_iota(jnp.int32, (CHUNK, CHUNK), 0)
    col = lax.broadcasted_iota(jnp.int32, (CHUNK, CHUNK), 1)
    diff = (row - col).astype(F32)
    ridx = lax.broadcasted_iota(jnp.int32, (CHUNK, 1), 0).astype(F32)
    cidx = lax.broadcasted_iota(jnp.int32, (1, CHUNK), 1).astype(F32)
    decay, xi, zeta, cdec = [], [], [], []
    for hd in range(RET_HEADS):
        lg = _log_gamma(hd)
        decay.append(jnp.where(diff >= 0, jnp.exp(jnp.maximum(diff, 0.0) * lg), 0.0))
        xi.append(jnp.exp((ridx + 1.0) * lg))
        zeta.append(jnp.exp((CHUNK - 1.0 - cidx) * lg))
        cdec.append(math.exp(CHUNK * lg))

    def chunk(c):
        rows = pl.ds(c * CHUNK, CHUNK)
        for g in range(SGU_GROUPS):
            cols = slice(g * SGU_GROUP_DIM, (g + 1) * SGU_GROUP_DIM)
            mixed = _dot(ws_ref[g], vn_s[rows, cols]) + bs_ref[g]
            cat_s[rows, cols] = (u_s[rows, cols] * mixed).astype(BF16)
        for hd in range(RET_HEADS):
            kcols = slice(hd * RET_DK, (hd + 1) * RET_DK)
            vcols = slice(hd * RET_DV, (hd + 1) * RET_DV)
            qc = q_s[rows, kcols]
            kt = k_s[kcols, rows]
            vc = vr_s[rows, vcols]
            s_prev = state[hd]
            sc = _dot(qc, kt.astype(BF16)) * decay[hd]
            o = _dot(sc.astype(BF16), vc) + _dot(qc, s_prev.astype(BF16)) * xi[hd]
            state[hd] = s_prev * cdec[hd] + _dot((kt * zeta[hd]).astype(BF16), vc)
            ocols = slice(D_MODEL + hd * RET_DV, D_MODEL + (hd + 1) * RET_DV)
            cat_s[rows, ocols] = _head_norm_gate(o, gate_s[rows, vcols]).astype(BF16)

    for c in range(tm // CHUNK):
        chunk(c)
    xo_ref[0] = x + _dot(cat_s[...], wout_ref[...])

    @pl.when(t == pl.num_programs(1) - 1)
    def _():
        sout_ref[0] = state[...]


def _even_prompt(x, j, g, w_in, ln_g, ln_b, w_s, b_s, w_out, cosf, sinf):
    b, l, d = x.shape
    tm = TM_PROMPT
    return pl.pallas_call(
        _even_prompt_kernel,
        grid=(b, l // tm),
        in_specs=[pl.BlockSpec((1, tm, d), lambda i, t: (i, t, 0)),
                  _const((1, d)), _layer(j, (d, EVEN_IN)), _const((1, d)), _const((1, d)),
                  _const((SGU_GROUPS, CHUNK, CHUNK)), _const((SGU_GROUPS, CHUNK, 1)),
                  _layer(j, (EVEN_OUT, d)),
                  pl.BlockSpec((tm, RET_DK), lambda i, t: (t, 0)),
                  pl.BlockSpec((tm, RET_DK), lambda i, t: (t, 0))],
        out_specs=[pl.BlockSpec((1, tm, d), lambda i, t: (i, t, 0)),
                   pl.BlockSpec((1, RET_HEADS, RET_DK, RET_DV), lambda i, t: (i, 0, 0, 0))],
        out_shape=[jax.ShapeDtypeStruct((b, l, d), F32),
                   jax.ShapeDtypeStruct((b, RET_HEADS, RET_DK, RET_DV), F32)],
        scratch_shapes=[pltpu.VMEM((RET_HEADS, RET_DK, RET_DV), F32),
                        pltpu.VMEM((tm, d), F32), pltpu.VMEM((tm, d), BF16),
                        pltpu.VMEM((tm, RET_HEADS * RET_DK), BF16),
                        pltpu.VMEM((RET_HEADS * RET_DK, tm), F32),
                        pltpu.VMEM((tm, d), BF16), pltpu.VMEM((tm, d), F32),
                        pltpu.VMEM((tm, EVEN_OUT), BF16)],
        compiler_params=_params(2),
        name="even_prompt",
    )(x, g, w_in, ln_g, ln_b, w_s, b_s, w_out, cosf, sinf)


def _even_sample_kernel(x_ref, s0_ref, g_ref, win_ref, lng_ref, lnb_ref, ws_ref, bs_ref, wout_ref,
                        cos_ref, sin_ref, xo_ref, sout_ref, vrow_ref,
                        q_s, kz_s, v_s, o_s):
    tt = x_ref.shape[0]
    ls = tt // s0_ref.shape[0]
    x = x_ref[...]
    u, vn, qs, ks, vr, gate = _even_front(x, g_ref, win_ref, lng_ref, lnb_ref, cos_ref, sin_ref)
    vrow_ref[...] = vn
    vn_b = vn.astype(BF16)
    vr_b = vr.astype(BF16)

    row = lax.broadcasted_iota(jnp.int32, (tt, tt), 0)
    col = lax.broadcasted_iota(jnp.int32, (tt, tt), 1)
    same = (row // ls) == (col // ls)
    diff = (row - col).astype(F32)
    rloc = (lax.broadcasted_iota(jnp.int32, (tt, 1), 0) % ls).astype(F32)

    cat = []
    for g in range(SGU_GROUPS):
        cols = slice(g * SGU_GROUP_DIM, (g + 1) * SGU_GROUP_DIM)
        mixed = _dot(ws_ref[g], vn_b[:, cols]) + bs_ref[g]
        cat.append((u[:, cols] * mixed).astype(BF16))

    xi, cdec = [], []
    for hd in range(RET_HEADS):
        lg = _log_gamma(hd)
        kcols = slice(hd * RET_DK, (hd + 1) * RET_DK)
        vcols = slice(hd * RET_DV, (hd + 1) * RET_DV)
        decay = jnp.where(same & (diff >= 0), jnp.exp(jnp.maximum(diff, 0.0) * lg), 0.0)
        xi.append(jnp.exp((rloc[:ls] + 1.0) * lg))
        cdec.append(math.exp(ls * lg))
        zeta = jnp.exp((ls - 1.0 - rloc) * lg)
        sc = _dot_nt(qs[hd].astype(BF16), ks[hd].astype(BF16)) * decay
        o_s[:, vcols] = _dot(sc.astype(BF16), vr_b[:, vcols])
        q_s[:, kcols] = qs[hd]
        kz_s[:, kcols] = ks[hd] * zeta
    v_s[...] = vr

    def per_batch(bi, carry):
        rows = pl.ds(pl.multiple_of(bi * ls, ls), ls)
        for hd in range(RET_HEADS):
            kcols = slice(hd * RET_DK, (hd + 1) * RET_DK)
            vcols = slice(hd * RET_DV, (hd + 1) * RET_DV)
            s_prev = s0_ref[bi, hd]
            qb = q_s[rows, kcols].astype(BF16)
            o_s[rows, vcols] += _dot(qb, s_prev.astype(BF16)) * xi[hd]
            sout_ref[bi, hd] = s_prev * cdec[hd] + _dot_tn(kz_s[rows, kcols].astype(BF16),
                                                          v_s[rows, vcols].astype(BF16))
        return carry

    lax.fori_loop(0, s0_ref.shape[0], per_batch, 0, unroll=2)
    for hd in range(RET_HEADS):
        vcols = slice(hd * RET_DV, (hd + 1) * RET_DV)
        cat.append(_head_norm_gate(o_s[:, vcols], gate[:, vcols]).astype(BF16))
    xo_ref[...] = x + _dot(jnp.concatenate(cat, axis=-1), wout_ref[...])


def _even_sample(x, s0, j, g, w_in, ln_g, ln_b, w_s_bd, b_s_bd, w_out, cosf, sinf):
    n, d = x.shape
    nb = s0.shape[0]
    ls = n // nb
    bt = BT_EVEN_SAMPLE
    tt = bt * ls
    state_spec = pl.BlockSpec((bt, RET_HEADS, RET_DK, RET_DV), lambda i: (i, 0, 0, 0))
    return pl.pallas_call(
        _even_sample_kernel,
        grid=(nb // bt,),
        in_specs=[pl.BlockSpec((tt, d), lambda i: (i, 0)), state_spec,
                  _const((1, d)), _layer(j, (d, EVEN_IN)), _const((1, d)), _const((1, d)),
                  _const((SGU_GROUPS, tt, tt)), _const((SGU_GROUPS, tt, 1)),
                  _layer(j, (EVEN_OUT, d)), _const((tt, RET_DK)), _const((tt, RET_DK))],
        out_specs=[pl.BlockSpec((tt, d), lambda i: (i, 0)), state_spec,
                   pl.BlockSpec((tt, d), lambda i: (i, 0))],
        out_shape=[jax.ShapeDtypeStruct((n, d), F32),
                   jax.ShapeDtypeStruct(s0.shape, F32),
                   jax.ShapeDtypeStruct((n, d), F32)],
        scratch_shapes=[pltpu.VMEM((tt, RET_HEADS * RET_DK), F32),
                        pltpu.VMEM((tt, RET_HEADS * RET_DK), F32),
                        pltpu.VMEM((tt, d), F32), pltpu.VMEM((tt, d), F32)],
        compiler_params=_params(1),
        name="even_sample",
    )(x, s0, g, w_in, ln_g, ln_b, w_s_bd, b_s_bd, w_out, cosf, sinf)


def _pad_even_odd(blk, kv_in_block):
    lane = lax.broadcasted_iota(jnp.int32, blk.shape, 1)
    rolled = pltpu.roll(blk, C_HEAD_DIM, 1)
    if kv_in_block == 0:
        even = jnp.where(lane < C_HEAD_DIM, blk, 0.0)
        odd = jnp.where(lane >= C_HEAD_DIM, rolled, 0.0)
    else:
        odd = jnp.where(lane >= C_HEAD_DIM, blk, 0.0)
        even = jnp.where(lane < C_HEAD_DIM, rolled, 0.0)
    return even.astype(BF16), odd.astype(BF16)


def _pair_normalise(outs, dens):
    lane = lax.broadcasted_iota(jnp.int32, outs[0].shape, 1)
    inv = jnp.where(lane < C_HEAD_DIM, 1.0 / dens[0], 1.0 / dens[1])
    return (outs[0] + outs[1]) * inv


def _window_attend(q2, ke, ko, ve, vo, valid, sink_e, sink_o):
    outs, dens = [], []
    for kk, vv, sink in ((ke, ve, sink_e), (ko, vo, sink_o)):
        s = jnp.where(valid, _dot_nt(q2, kk), NEG)
        m = jnp.maximum(jnp.max(s, axis=-1, keepdims=True), sink)
        p = jnp.exp2(s - m)
        dens.append(jnp.sum(p, axis=-1, keepdims=True) + jnp.exp2(sink - m))
        outs.append(_dot(p.astype(BF16), vv))
    return _pair_normalise(outs, dens)


def _banded_attend(q2, ke, ko, ve, vo, upper, prev_bias, sink_e, sink_o):
    w = upper.shape[1]
    zero = jnp.zeros((), F32)
    outs, dens = [], []
    for kk, vv, sink in ((ke, ve, sink_e), (ko, vo, sink_o)):
        s2 = _dot(q2, kk)
        s_prev = s2[:, :w] if prev_bias is None else s2[:, :w] + prev_bias
        s = jnp.where(upper, s_prev, s2[:, w:])
        m = jnp.maximum(jnp.max(s, axis=-1, keepdims=True), sink)
        p = jnp.exp2(s - m)
        dens.append(jnp.sum(p, axis=-1, keepdims=True) + jnp.exp2(sink - m))
        pp = jnp.concatenate([jnp.where(upper, p, zero), jnp.where(upper, zero, p)], axis=1)
        outs.append(_dot(pp.astype(BF16), vv))
    return _pair_normalise(outs, dens)


def _odd_front(x, g_ref, wqkv_ref, bqkv_ref, cos_ref, lo_ref, hi_ref):
    h = _rms(x, g_ref[...]).astype(BF16)
    qkv = _dot(h, wqkv_ref[...]) + bqkv_ref[...]
    cos, lo, hi = cos_ref[...], lo_ref[...], hi_ref[...]
    nq = C_HEADS * C_HEAD_DIM
    nk = C_KV_HEADS * C_HEAD_DIM
    qb = [_win_rotate(qkv[:, j * LANES:(j + 1) * LANES], cos, lo, hi) * (C_HEAD_DIM ** -0.5 * LOG2E)
          for j in range(nq // LANES)]
    kb = [_win_rotate(qkv[:, nq + j * LANES:nq + (j + 1) * LANES], cos, lo, hi)
          for j in range(nk // LANES)]
    v = qkv[:, nq + nk:]
    return qb, kb, v


def _odd_prompt_kernel(sink_ref, x_ref, g_ref, wqkv_ref, bqkv_ref, wout_ref, bout_ref,
                       cos_ref, lo_ref, hi_ref, xo_ref, wk_ref, wv_ref,
                       q_s, kpad, kprev, vpad, att_s):
    t = pl.program_id(1)
    tm = x_ref.shape[1]
    w = CHUNK
    x = x_ref[0]
    qb, kb, v = _odd_front(x, g_ref, wqkv_ref, bqkv_ref, cos_ref, lo_ref, hi_ref)
    for j, blk in enumerate(qb):
        q_s[:, j * LANES:(j + 1) * LANES] = blk.astype(BF16)

    @pl.when(t == 0)
    def _():
        kprev[...] = jnp.zeros((2 * C_KV_HEADS, LANES, w), BF16)
        vpad[:, 0:w, :] = jnp.zeros((2 * C_KV_HEADS, w, LANES), BF16)

    kt = [blk.T for blk in kb]
    k_tails = []
    for kh in range(C_KV_HEADS):
        j, r = divmod(kh, LANES // C_HEAD_DIM)
        head = kt[j][r * C_HEAD_DIM:(r + 1) * C_HEAD_DIM, :]
        zeros = jnp.zeros_like(head)
        ke_t = jnp.concatenate([head, zeros], axis=0).astype(BF16)
        ko_t = jnp.concatenate([zeros, head], axis=0).astype(BF16)
        kpad[2 * kh] = ke_t
        kpad[2 * kh + 1] = ko_t
        k_tails += [ke_t[:, tm - w:], ko_t[:, tm - w:]]
        ve, vo = _pad_even_odd(v[:, j * LANES:(j + 1) * LANES], r)
        vpad[2 * kh, w:, :] = ve
        vpad[2 * kh + 1, w:, :] = vo

    @pl.when(t == pl.num_programs(1) - 1)
    def _():
        for j, blk in enumerate(kb):
            wk_ref[0, :, j * LANES:(j + 1) * LANES] = blk[tm - w:, :]
        wv_ref[0] = v[tm - w:, :]

    qi = lax.broadcasted_iota(jnp.int32, (2 * w, w), 0) % w
    kj = lax.broadcasted_iota(jnp.int32, (2 * w, w), 1)
    upper = kj > qi
    first_half = lax.broadcasted_iota(jnp.int32, (2 * w, 1), 0) < w

    def block(c):
        rows = pl.ds(c * w, w)
        keys = pl.ds(c * w, 2 * w)
        kkeys = pl.ds((c - 1) * w, 2 * w)
        prev_bias = jnp.where(t == 0, NEG, 0.0) if c == 0 else None
        for kh in range(C_KV_HEADS):
            if c == 0:
                ke = jnp.concatenate([kprev[2 * kh], kpad[2 * kh, :, 0:w]], axis=1)
                ko = jnp.concatenate([kprev[2 * kh + 1], kpad[2 * kh + 1, :, 0:w]], axis=1)
            else:
                ke, ko = kpad[2 * kh, :, kkeys], kpad[2 * kh + 1, :, kkeys]
            ve, vo = vpad[2 * kh, keys, :], vpad[2 * kh + 1, keys, :]
            c0 = (2 * kh) * LANES
            q2 = jnp.concatenate([q_s[rows, c0:c0 + LANES], q_s[rows, c0 + LANES:c0 + 2 * LANES]], 0)
            h0 = kh * (C_HEADS // C_KV_HEADS)
            sink_e = jnp.where(first_half, sink_ref[h0], sink_ref[h0 + 2]) * LOG2E
            sink_o = jnp.where(first_half, sink_ref[h0 + 1], sink_ref[h0 + 3]) * LOG2E
            o = _banded_attend(q2, ke, ko, ve, vo, upper, prev_bias, sink_e, sink_o)
            att_s[rows, c0:c0 + LANES] = o[:w].astype(BF16)
            att_s[rows, c0 + LANES:c0 + 2 * LANES] = o[w:].astype(BF16)

    for c in range(tm // w):
        block(c)
    xo_ref[0] = x + _dot(att_s[...], wout_ref[...]) + bout_ref[...]
    for i, tail in enumerate(k_tails):
        kprev[i] = tail
    vpad[:, 0:w, :] = vpad[:, tm:tm + w, :]


def _odd_prompt(x, j, sinks, g, w_qkv, b_qkv, w_out, b_out, cos, lo, hi):
    b, l, d = x.shape
    tm = TM_PROMPT_WIDE
    kvw = C_KV_HEADS * C_HEAD_DIM
    win_spec = pl.BlockSpec((1, CHUNK, kvw), lambda i, t: (i, 0, 0))
    tab = pl.BlockSpec((tm, LANES), lambda i, t: (t, 0))
    return pl.pallas_call(
        _odd_prompt_kernel,
        grid=(b, l // tm),
        in_specs=[pl.BlockSpec(memory_space=pltpu.SMEM),
                  pl.BlockSpec((1, tm, d), lambda i, t: (i, t, 0)),
                  _const((1, d)), _layer(j, (d, ODD_IN)), _const((1, ODD_IN)),
                  _layer(j, (d, d)), _const((1, d)), tab, tab, tab],
        out_specs=[pl.BlockSpec((1, tm, d), lambda i, t: (i, t, 0)), win_spec, win_spec],
        out_shape=[jax.ShapeDtypeStruct((b, l, d), F32),
                   jax.ShapeDtypeStruct((b, CHUNK, kvw), F32),
                   jax.ShapeDtypeStruct((b, CHUNK, kvw), F32)],
        scratch_shapes=[pltpu.VMEM((tm, d), BF16),
                        pltpu.VMEM((2 * C_KV_HEADS, LANES, tm), BF16),
                        pltpu.VMEM((2 * C_KV_HEADS, LANES, CHUNK), BF16),
                        pltpu.VMEM((2 * C_KV_HEADS, tm + CHUNK, LANES), BF16),
                        pltpu.VMEM((tm, d), BF16)],
        compiler_params=_params(2),
        name="odd_prompt",
    )(sinks, x, g, w_qkv, b_qkv, w_out, b_out, cos, lo, hi)


def _odd_sample_kernel(sink_ref, x_ref, ck_ref, cv_ref, g_ref, wqkv_ref, bqkv_ref, wout_ref, bout_ref,
                       cos_ref, lo_ref, hi_ref, xo_ref, wk_ref, wv_ref, q_s, k_s, v_s, att_s, *, ls):
    i = pl.program_id(0)
    bt, w, kvw = ck_ref.shape
    tt = bt * ls

    @pl.when(i == 0)
    def _():
        qb, kb, v = _odd_front(x_ref[...], g_ref, wqkv_ref, bqkv_ref, cos_ref, lo_ref, hi_ref)
        for j, blk in enumerate(qb):
            q_s[:, j * LANES:(j + 1) * LANES] = blk.astype(BF16)
        for j, blk in enumerate(kb):
            k_s[:, j * LANES:(j + 1) * LANES] = blk
        v_s[...] = v

    rows = pl.ds(pl.multiple_of(i * tt, tt), tt)
    knew = k_s[rows, :]
    vnew = v_s[rows, :]
    ck = ck_ref[...]
    cv = cv_ref[...]
    for bi in range(bt):
        wk_ref[bi, 0:w - ls, :] = ck[bi, ls:, :]
        wk_ref[bi, w - ls:, :] = knew[bi * ls:(bi + 1) * ls, :]
        wv_ref[bi, 0:w - ls, :] = cv[bi, ls:, :]
        wv_ref[bi, w - ls:, :] = vnew[bi * ls:(bi + 1) * ls, :]
    kall = jnp.concatenate([ck.reshape(bt * w, kvw), knew], axis=0)
    vall = jnp.concatenate([cv.reshape(bt * w, kvw), vnew], axis=0)
    ns = bt * w + tt

    tok = lax.broadcasted_iota(jnp.int32, (2 * tt, ns), 0) % tt
    key = lax.broadcasted_iota(jnp.int32, (2 * tt, ns), 1)
    tb, tl = tok // ls, tok % ls
    cached = key < bt * w
    nkey = key - bt * w
    valid = ((cached & ((key // w) == tb) & ((key % w) > tl))
             | ((key >= bt * w) & ((nkey // ls) == tb) & ((nkey % ls) <= tl)))
    first_half = lax.broadcasted_iota(jnp.int32, (2 * tt, 1), 0) < tt

    for kh in range(C_KV_HEADS):
        j, r = divmod(kh, LANES // C_HEAD_DIM)
        ke, ko = _pad_even_odd(kall[:, j * LANES:(j + 1) * LANES], r)
        ve, vo = _pad_even_odd(vall[:, j * LANES:(j + 1) * LANES], r)
        c0 = (2 * kh) * LANES
        q2 = jnp.concatenate([q_s[rows, c0:c0 + LANES], q_s[rows, c0 + LANES:c0 + 2 * LANES]], axis=0)
        h0 = kh * (C_HEADS // C_KV_HEADS)
        sink_e = jnp.where(first_half, sink_ref[h0], sink_ref[h0 + 2]) * LOG2E
        sink_o = jnp.where(first_half, sink_ref[h0 + 1], sink_ref[h0 + 3]) * LOG2E
        o = _window_attend(q2, ke, ko, ve, vo, valid, sink_e, sink_o)
        att_s[rows, c0:c0 + LANES] = o[:tt].astype(BF16)
        att_s[rows, c0 + LANES:c0 + 2 * LANES] = o[tt:].astype(BF16)

    @pl.when(i == pl.num_programs(0) - 1)
    def _():
        xo_ref[...] = x_ref[...] + _dot(att_s[...], wout_ref[...]) + bout_ref[...]


def _odd_sample(x, ck, cv, j, sinks, g, w_qkv, b_qkv, w_out, b_out, cos, lo, hi):
    n, d = x.shape
    nb, w, kvw = ck.shape
    bt = BT_SAMPLE
    cache_spec = pl.BlockSpec((bt, w, kvw), lambda i: (i, 0, 0))
    return pl.pallas_call(
        functools.partial(_odd_sample_kernel, ls=n // nb),
        grid=(nb // bt,),
        in_specs=[pl.BlockSpec(memory_space=pltpu.SMEM),
                  _const((n, d)), cache_spec, cache_spec,
                  _const((1, d)), _layer(j, (d, ODD_IN)), _const((1, ODD_IN)),
                  _layer(j, (d, d)), _const((1, d)),
                  _const((n, LANES)), _const((n, LANES)), _const((n, LANES))],
        out_specs=[pl.BlockSpec((n, d), lambda i: (0, 0)), cache_spec, cache_spec],
        out_shape=[jax.ShapeDtypeStruct((n, d), F32),
                   jax.ShapeDtypeStruct(ck.shape, F32), jax.ShapeDtypeStruct(cv.shape, F32)],
        scratch_shapes=[pltpu.VMEM((n, d), BF16), pltpu.VMEM((n, kvw), F32), pltpu.VMEM((n, kvw), F32),
                        pltpu.VMEM((n, d), BF16)],
        compiler_params=_params(1),
        name="odd_sample",
    )(sinks, x, ck, cv, g, w_qkv, b_qkv, w_out, b_out, cos, lo, hi)


def _cross_heads(q, mk, mv):
    outs = []
    for hd in range(MEM_HEADS):
        cols = slice(hd * MEM_HEAD_DIM, (hd + 1) * MEM_HEAD_DIM)
        s = _dot_nt(q[:, cols], mk[:, cols])
        p = jnp.exp(s - jnp.max(s, axis=-1, keepdims=True))
        den = jnp.sum(p, axis=-1, keepdims=True)
        outs.append(_dot(p.astype(BF16), mv[:, cols]) * (1.0 / den))
    return outs


def _mlp(x1, gf_ref, wup_ref, wdown_ref):
    h2 = _rms(x1, gf_ref[...]).astype(BF16)
    acc = x1
    for j in range(D_FF // D_MODEL):
        sl = slice(j * D_MODEL, (j + 1) * D_MODEL)
        a = jnp.square(jnp.maximum(_dot(h2, wup_ref[:, sl]), 0.0)).astype(BF16)
        acc = acc + _dot(a, wdown_ref[sl, :])
    return acc


def _cross_mlp_prompt_kernel(x_ref, mk_ref, mv_ref, gc_ref, wmq_ref, wmo_ref, gf_ref, wup_ref,
                             wdown_ref, gfin_ref, xo_ref, o_s, *, final):
    x = x_ref[0]
    h = _rms(x, gc_ref[...]).astype(BF16)
    q = (_dot(h, wmq_ref[...]) * (MEM_HEAD_DIM ** -0.5)).astype(BF16)
    outs = _cross_heads(q, mk_ref[0, 0], mv_ref[0, 0])
    for hd, o in enumerate(outs):
        o_s[:, hd * MEM_HEAD_DIM:(hd + 1) * MEM_HEAD_DIM] = o.astype(BF16)
    x1 = x + _dot(o_s[...], wmo_ref[...])
    y = _mlp(x1, gf_ref, wup_ref, wdown_ref)
    if final:
        y = _rms(y, gfin_ref[...])
    xo_ref[0] = y


def _cross_mlp_prompt(x, mk, mv, layer, g_cross, w_mq, w_mo, g_ffn, w_up, w_down, g_final, final):
    b, l, d = x.shape
    tm = TM_PROMPT_WIDE
    mem_spec = pl.BlockSpec((1, 1, N_MEM, d), lambda i, t: (layer, i, 0, 0))
    return pl.pallas_call(
        functools.partial(_cross_mlp_prompt_kernel, final=final),
        grid=(b, l // tm),
        in_specs=[pl.BlockSpec((1, tm, d), lambda i, t: (i, t, 0)), mem_spec, mem_spec,
                  _const((1, d)), _layer(layer, (d, d)), _layer(layer, (d, d)), _const((1, d)),
                  _const((d, D_FF)), _const((D_FF, d)), _const((1, d))],
        out_specs=pl.BlockSpec((1, tm, d), lambda i, t: (i, t, 0)),
        out_shape=jax.ShapeDtypeStruct((b, l, d), F32),
        scratch_shapes=[pltpu.VMEM((tm, d), BF16)],
        compiler_params=_params(2),
        name=f"cross_mlp_prompt_{layer}",
    )(x, mk, mv, g_cross, w_mq, w_mo, g_ffn, w_up, w_down, g_final)


def _cross_sample_kernel(x_ref, mk_ref, mv_ref, gc_ref, wmq_ref, wmo_ref, xo_ref, q_s, o_s):
    tt = x_ref.shape[0]
    bt = mk_ref.shape[1]
    ls = tt // bt
    nrow = N_MEM * MEM_HEADS
    x = x_ref[...]
    h = _rms(x, gc_ref[...]).astype(BF16)
    q_s[...] = _dot(h, wmq_ref[...]) * (MEM_HEAD_DIM ** -0.5)
    qhead = lax.broadcasted_iota(jnp.int32, (MEM_HEADS * ls, nrow), 0) // ls
    khead = lax.broadcasted_iota(jnp.int32, (MEM_HEADS * ls, nrow), 1) % MEM_HEADS
    own = qhead == khead

    def per_batch(bi, carry):
        rows = pl.ds(pl.multiple_of(bi * ls, ls), ls)
        qb = q_s[rows, :]
        q4 = jnp.concatenate([qb[:, hd * MEM_HEAD_DIM:(hd + 1) * MEM_HEAD_DIM]
                              for hd in range(MEM_HEADS)], axis=0).astype(BF16)
        k2 = mk_ref[0, bi].reshape(nrow, MEM_HEAD_DIM).astype(BF16)
        v2 = mv_ref[0, bi].reshape(nrow, MEM_HEAD_DIM).astype(BF16)
        s = jnp.where(own, _dot_nt(q4, k2), NEG)
        p = jnp.exp(s - jnp.max(s, axis=-1, keepdims=True))
        den = jnp.sum(p, axis=-1, keepdims=True)
        o4 = _dot(p.astype(BF16), v2) * (1.0 / den)
        for hd in range(MEM_HEADS):
            o_s[rows, hd * MEM_HEAD_DIM:(hd + 1) * MEM_HEAD_DIM] = o4[hd * ls:(hd + 1) * ls]
        return carry

    lax.fori_loop(0, bt, per_batch, 0, unroll=True)
    xo_ref[...] = x + _dot(o_s[...].astype(BF16), wmo_ref[...])


def _cross_sample(x, mk, mv, layer, g_cross, w_mq, w_mo):
    n, d = x.shape
    nb = mk.shape[1]
    ls = n // nb
    bt = BT_SAMPLE
    tt = bt * ls
    mem_spec = pl.BlockSpec((1, bt, N_MEM, MEM_HEADS, MEM_HEAD_DIM), lambda i: (layer, i, 0, 0, 0))
    return pl.pallas_call(
        _cross_sample_kernel,
        grid=(nb // bt,),
        in_specs=[pl.BlockSpec((tt, d), lambda i: (i, 0)), mem_spec, mem_spec,
                  _const((1, d)), _layer(layer, (d, d)), _layer(layer, (d, d))],
        out_specs=pl.BlockSpec((tt, d), lambda i: (i, 0)),
        out_shape=jax.ShapeDtypeStruct((n, d), F32),
        scratch_shapes=[pltpu.VMEM((tt, d), F32), pltpu.VMEM((tt, d), F32)],
        compiler_params=_params(1),
        name=f"cross_sample_{layer}",
    )(x, mk, mv, g_cross, w_mq, w_mo)


def _mlp_stream_kernel(x_ref, gf_ref, wup_ref, wdown_ref, gfin_ref, xo_ref, wup_b_ref, wdown_b_ref,
                       h_s, acc_s, *, final):
    j = pl.program_id(0)

    @pl.when(j == 0)
    def _():
        x = x_ref[...]
        h_s[...] = _rms(x, gf_ref[...]).astype(BF16)
        acc_s[...] = x

    wu = wup_ref[...].astype(BF16)
    wd = wdown_ref[...].astype(BF16)
    wup_b_ref[...] = wu
    wdown_b_ref[...] = wd
    a = jnp.square(jnp.maximum(_dot(h_s[...], wu), 0.0)).astype(BF16)
    acc_s[...] += _dot(a, wd)

    @pl.when(j == pl.num_programs(0) - 1)
    def _():
        y = acc_s[...]
        if final:
            y = _rms(y, gfin_ref[...])
        xo_ref[...] = y


def _mlp_sample(x, layer, g_ffn, w_up, w_down, g_final, final):
    n, d = x.shape
    fc = FF_CHUNK
    return pl.pallas_call(
        functools.partial(_mlp_stream_kernel, final=final),
        grid=(D_FF // fc,),
        in_specs=[_const((n, d)), _const((1, d)),
                  pl.BlockSpec((None, d, fc), lambda c: (layer, 0, c)),
                  pl.BlockSpec((None, fc, d), lambda c: (layer, c, 0)),
                  _const((1, d))],
        out_specs=[pl.BlockSpec((n, d), lambda c: (0, 0)),
                   pl.BlockSpec((d, fc), lambda c: (0, c)),
                   pl.BlockSpec((fc, d), lambda c: (c, 0))],
        out_shape=[jax.ShapeDtypeStruct((n, d), F32),
                   jax.ShapeDtypeStruct((d, D_FF), BF16),
                   jax.ShapeDtypeStruct((D_FF, d), BF16)],
        scratch_shapes=[pltpu.VMEM((n, d), BF16), pltpu.VMEM((n, d), F32)],
        compiler_params=_params(1),
        name=f"mlp_sample_{layer}",
    )(x, g_ffn, w_up, w_down, g_final)


def kernel(x_prompt, x_sample, state_ret, cache_win_k, cache_win_v, cache_mem_k, cache_mem_v, mem_prompt, g_mix, w_in_e, sgu_ln_g, sgu_ln_b, w_spatial, b_spatial, w_out_e, w_qkv_o, b_qkv_o, sinks, w_out_o, b_out_o, g_cross, g_mem, w_mq, w_mk, w_mv, w_mo, g_ffn, w_up, w_down, g_final):
    bp, lp, d = x_prompt.shape
    bs, ls, _ = x_sample.shape
    assert d == D_MODEL and lp % TM_PROMPT == 0 and lp % TM_PROMPT_WIDE == 0 and bs % BT_SAMPLE == 0 and bs % BT_EVEN_SAMPLE == 0
    assert ls < CHUNK and CHUNK % ls == 0 and D_FF % FF_CHUNK == 0 and bp % MEMKV_BT == 0
    ns = bs * ls
    kvw = C_KV_HEADS * C_HEAD_DIM

    def row(v):
        return v.reshape(1, -1).astype(F32)

    def bf(w):
        return w.astype(BF16)

    pos_p = np.arange(lp)
    ret_tab_p = _ret_rope_tables(pos_p)
    tt_e = BT_EVEN_SAMPLE * ls
    ret_tab_s = _ret_rope_tables(PAST_LEN + (np.arange(tt_e) % ls))
    win_tab_p = _win_rope_tables(pos_p)
    win_tab_s = _win_rope_tables(PAST_LEN + (np.arange(ns) % ls))

    w_in_b, w_out_e_b, w_qkv_b, w_out_o_b = bf(w_in_e), bf(w_out_e), bf(w_qkv_o), bf(w_out_o)
    w_mq_b, w_mo_b = bf(w_mq), bf(w_mo)

    mk_p, mv_p, mk_pb, mv_pb = _memkv(mem_prompt, g_mem, w_mk, w_mv)
    mk_s, mv_s = cache_mem_k, cache_mem_v

    xp = x_prompt
    xs = x_sample.reshape(ns, d)
    final_g = row(g_final)
    outs = {}
    for layer in range(DEPTH):
        j = layer // 2
        g = row(g_mix[layer])
        if layer % 2 == 0:
            ln_g, ln_b = row(sgu_ln_g[j]), row(sgu_ln_b[j])
            tril = jnp.tril(jnp.ones((CHUNK, CHUNK), bool))
            w_s = bf(jnp.where(tril[None], w_spatial[j], 0.0))
            b_s = b_spatial[j].reshape(SGU_GROUPS, CHUNK, 1)
            xp, ret_p = _even_prompt(xp, j, g, w_in_b, ln_g, ln_b, w_s, b_s, w_out_e_b, *ret_tab_p)
            w_small = w_s[:, :ls, :ls]
            eye = jnp.eye(BT_EVEN_SAMPLE, dtype=BF16)
            w_bd = jnp.einsum("ab,gts->gatbs", eye, w_small).reshape(SGU_GROUPS, tt_e, tt_e)
            b_bd = jnp.tile(b_spatial[j][:, :ls], (1, BT_EVEN_SAMPLE)).reshape(SGU_GROUPS, tt_e, 1)
            xs, ret_s, v_rows = _even_sample(xs, state_ret[j], j, g, w_in_b, ln_g, ln_b, w_bd, b_bd,
                                             w_out_e_b, *ret_tab_s)
            outs["ret_p"], outs["ret_s"], outs["v_rows"] = ret_p, ret_s, v_rows
        else:
            b_qkv, b_out = row(b_qkv_o[j]), row(b_out_o[j])
            xp, wk_p, wv_p = _odd_prompt(xp, j, sinks[j], g, w_qkv_b, b_qkv, w_out_o_b, b_out,
                                         *win_tab_p)
            xs, wk_s, wv_s = _odd_sample(xs, cache_win_k[j].reshape(bs, CHUNK, kvw),
                                         cache_win_v[j].reshape(bs, CHUNK, kvw), j, sinks[j],
                                         g, w_qkv_b, b_qkv, w_out_o_b, b_out, *win_tab_s)
            outs["wk_p"], outs["wv_p"], outs["wk_s"], outs["wv_s"] = wk_p, wv_p, wk_s, wv_s
        final = layer == DEPTH - 1
        gc, gf = row(g_cross[layer]), row(g_ffn[layer])
        xs = _cross_sample(xs, mk_s, mv_s, layer, gc, w_mq_b, w_mo_b)
        xs, w_up_l, w_down_l = _mlp_sample(xs, layer, gf, w_up, w_down, final_g, final)
        xp = _cross_mlp_prompt(xp, mk_pb, mv_pb, layer, gc, w_mq_b, w_mo_b, gf, w_up_l, w_down_l,
                               final_g, final)

    n_even, n_odd = (DEPTH + 1) // 2, DEPTH // 2
    return (xp,
            xs.reshape(bs, ls, d),
            mk_p,
            mv_p,
            outs["ret_p"].reshape(n_even, bp, RET_HEADS, RET_DK, RET_DV),
            outs["ret_s"].reshape(n_even, bs, RET_HEADS, RET_DK, RET_DV),
            outs["v_rows"].reshape(n_even, bs, ls, SGU_GROUPS, SGU_GROUP_DIM),
            outs["wk_p"].reshape(n_odd, bp, CHUNK, C_KV_HEADS, C_HEAD_DIM),
            outs["wv_p"].reshape(n_odd, bp, CHUNK, C_KV_HEADS, C_HEAD_DIM),
            outs["wk_s"].reshape(n_odd, bs, CHUNK, C_KV_HEADS, C_HEAD_DIM),
            outs["wv_s"].reshape(n_odd, bs, CHUNK, C_KV_HEADS, C_HEAD_DIM))
```

```python
import functools
import math

import numpy as np
import jax
import jax.numpy as jnp
from jax import lax
from jax.experimental import pallas as pl
from jax.experimental.pallas import tpu as pltpu

F32 = jnp.float32
BF16 = jnp.bfloat16

D_MODEL = 1024
DEPTH = 2
PAST_LEN = 16384
EPS = 1e-6

SGU_GROUPS = 4
SGU_GROUP_DIM = 256
RET_HEADS = 4
RET_DK = 128
RET_DV = 256
RET_ROPE_BASE = 10000.0
CHUNK = 128
RET_CHUNK_PROMPT = 256
C_HEADS = 16
C_KV_HEADS = 4
C_HEAD_DIM = 64
C_ROPE_BASE = 150000.0
N_MEM = 256
MEM_HEADS = 4
MEM_HEAD_DIM = 256
D_FF = 4096
EVEN_IN = 5120
EVEN_OUT = 2048
ODD_IN = 1536

LANES = 128
VMEM_LIMIT = 56 * 2 ** 20
NEG = -1e30
LOG2E = math.log2(math.e)

TM_PROMPT = 512
TM_PROMPT_WIDE = 1024
BT_SAMPLE = 8
BT_EVEN_SAMPLE = 16
FF_CHUNK = 512
MEMKV_BT = 2


def _params(n_axes):
    return pltpu.CompilerParams(dimension_semantics=("arbitrary",) * n_axes,
                                vmem_limit_bytes=VMEM_LIMIT)


def _const(shape):
    nd = len(shape)
    return pl.BlockSpec(shape, lambda *_: (0,) * nd, pipeline_mode=pl.Buffered(1))


def _layer(layer, shape):
    nd = len(shape)
    return pl.BlockSpec((None,) + tuple(shape), lambda *_: (layer,) + (0,) * nd,
                        pipeline_mode=pl.Buffered(1))


def _rms(x, g):
    return x * lax.rsqrt(jnp.mean(x * x, axis=-1, keepdims=True) + EPS) * g


def _dot(a, b):
    return jnp.dot(a, b, preferred_element_type=F32)


def _dot_nt(a, b):
    return lax.dot_general(a, b, (((1,), (1,)), ((), ())), preferred_element_type=F32)


def _dot_tn(a, b):
    return lax.dot_general(a, b, (((0,), (0,)), ((), ())), preferred_element_type=F32)


def _ret_rope_tables(pos):
    half = RET_DK // 2
    inv = RET_ROPE_BASE ** (-np.arange(half, dtype=np.float64) / half)
    ang = pos.astype(np.float64)[:, None] * inv[None, :]
    cos, sin = np.cos(ang), np.sin(ang)
    return (jnp.asarray(np.concatenate([cos, cos], -1), F32),
            jnp.asarray(np.concatenate([-sin, sin], -1), F32))


def _win_rope_tables(pos):
    half = C_HEAD_DIM // 2
    inv = C_ROPE_BASE ** (-np.arange(half, dtype=np.float64) / half)
    ang = pos.astype(np.float64)[:, None] * inv[None, :]
    lane = np.arange(LANES)
    cos = np.cos(ang)[:, lane % half]
    sin = np.sin(ang)[:, lane % half]
    low = (lane % C_HEAD_DIM) < half
    return (jnp.asarray(cos, F32),
            jnp.asarray(np.where(low[None], -sin, 0.0), F32),
            jnp.asarray(np.where(low[None], 0.0, sin), F32))


def _ret_rotate(x, cosf, sinf):
    return x * cosf + pltpu.roll(x, RET_DK // 2, 1) * sinf


def _win_rotate(x, cos, sin_lo, sin_hi):
    return (x * cos + pltpu.roll(x, LANES - C_HEAD_DIM // 2, 1) * sin_lo
            + pltpu.roll(x, C_HEAD_DIM // 2, 1) * sin_hi)


def _log_gamma(h):
    return math.log1p(-2.0 ** (-5.0 - h))


def _memkv_kernel(mem_ref, g_ref, wk_ref, wv_ref, k_ref, v_ref, kb_ref, vb_ref, wk_s, wv_s):
    @pl.when(pl.program_id(1) == 0)
    def _():
        wk_s[...] = wk_ref[0].astype(BF16)
        wv_s[...] = wv_ref[0].astype(BF16)

    nb = mem_ref.shape[0]
    m = _rms(mem_ref[...].reshape(nb * N_MEM, D_MODEL), g_ref[0]).astype(BF16)
    for w_s, out_ref, bf_ref in ((wk_s, k_ref, kb_ref), (wv_s, v_ref, vb_ref)):
        kv = _dot(m, w_s[...])
        bf_ref[0] = kv.astype(BF16).reshape(nb, N_MEM, D_MODEL)
        for bi in range(nb):
            for hd in range(MEM_HEADS):
                out_ref[0, bi, :, hd, :] = kv[bi * N_MEM:(bi + 1) * N_MEM,
                                              hd * MEM_HEAD_DIM:(hd + 1) * MEM_HEAD_DIM]


def _memkv(mem, g_mem, w_mk, w_mv):
    b = mem.shape[0]
    nb = MEMKV_BT
    out = jax.ShapeDtypeStruct((DEPTH, b, N_MEM, MEM_HEADS, MEM_HEAD_DIM), F32)
    out_b = jax.ShapeDtypeStruct((DEPTH, b, N_MEM, D_MODEL), BF16)
    spec5 = pl.BlockSpec((1, nb, N_MEM, MEM_HEADS, MEM_HEAD_DIM), lambda l, i: (l, i, 0, 0, 0))
    spec_b = pl.BlockSpec((1, nb, N_MEM, D_MODEL), lambda l, i: (l, i, 0, 0))
    return pl.pallas_call(
        _memkv_kernel,
        grid=(DEPTH, b // nb),
        in_specs=[pl.BlockSpec((nb, N_MEM, D_MODEL), lambda l, i: (i, 0, 0)),
                  pl.BlockSpec((1, 1, D_MODEL), lambda l, i: (l, 0, 0)),
                  pl.BlockSpec((1, D_MODEL, D_MODEL), lambda l, i: (l, 0, 0)),
                  pl.BlockSpec((1, D_MODEL, D_MODEL), lambda l, i: (l, 0, 0))],
        out_specs=[spec5, spec5, spec_b, spec_b],
        out_shape=[out, out, out_b, out_b],
        scratch_shapes=[pltpu.VMEM((D_MODEL, D_MODEL), BF16), pltpu.VMEM((D_MODEL, D_MODEL), BF16)],
        compiler_params=_params(2),
        name="memkv",
    )(mem, g_mem.reshape(DEPTH, 1, D_MODEL), w_mk, w_mv)


def _even_front(x, g_ref, win_ref, lng_ref, lnb_ref, cos_ref, sin_ref):
    h = _rms(x, g_ref[...]).astype(BF16)

    def proj(lo, hi):
        return _dot(h, win_ref[:, lo:hi])

    u = jax.nn.gelu(proj(0, 1024))
    v = jax.nn.gelu(proj(1024, 2048))
    vc = v - jnp.mean(v, axis=-1, keepdims=True)
    vn = vc * lax.rsqrt(jnp.mean(vc * vc, axis=-1, keepdims=True) + EPS) * lng_ref[...] + lnb_ref[...]
    q = proj(2048, 2560)
    k = proj(2560, 3072)
    vr = proj(3072, 4096)
    gate = proj(4096, 5120)
    cosf, sinf = cos_ref[...], sin_ref[...]
    qs, ks = [], []
    for hd in range(RET_HEADS):
        cols = slice(hd * RET_DK, (hd + 1) * RET_DK)
        qs.append(_ret_rotate(q[:, cols], cosf, sinf))
        ks.append(_ret_rotate(k[:, cols], cosf, sinf) * (RET_DK ** -0.5))
    return u, vn, qs, ks, vr, gate


def _head_norm_gate(o, gate):
    on = o * lax.rsqrt(jnp.mean(o * o, axis=-1, keepdims=True) + EPS)
    return on * jax.nn.silu(gate)


def _even_prompt_kernel(x_ref, g_ref, win_ref, lng_ref, lnb_ref, ws_ref, bs_ref, wout_ref,
                        cos_ref, sin_ref, xo_ref, sout_ref,
                        state, u_s, vn_s, q_s, k_s, vr_s, gate_s, cat_s, oi_s):
    t = pl.program_id(1)
    tm = x_ref.shape[1]

    @pl.when(t == 0)
    def _():
        state[...] = jnp.zeros_like(state)

    x = x_ref[0]
    u, vn, qs, ks, vr, gate = _even_front(x, g_ref, win_ref, lng_ref, lnb_ref, cos_ref, sin_ref)
    u_s[...] = u
    vn_s[...] = vn.astype(BF16)
    for hd in range(RET_HEADS):
        cols = slice(hd * RET_DK, (hd + 1) * RET_DK)
        q_s[:, cols] = qs[hd].astype(BF16)
        k_s[cols, :] = ks[hd].T
    vr_s[...] = vr.astype(BF16)
    gate_s[...] = gate

    rc = RET_CHUNK_PROMPT
    row = lax.broadcasted_iota(jnp.int32, (rc, rc), 0)
    col = lax.broadcasted_iota(jnp.int32, (rc, rc), 1)
    diff = (row - col).astype(F32)
    ridx = lax.broadcasted_iota(jnp.int32, (rc, 1), 0).astype(F32)
    cidx = lax.broadcasted_iota(jnp.int32, (1, rc), 1).astype(F32)
    decay, xi, zeta, cdec = [], [], [], []
    for hd in range(RET_HEADS):
        lg = _log_gamma(hd)
        decay.append(jnp.where(diff >= 0, jnp.exp(jnp.maximum(diff, 0.0) * lg), 0.0))
        xi.append(jnp.exp((ridx + 1.0) * lg))
        zeta.append(jnp.exp((rc - 1.0 - cidx) * lg))
        cdec.append(math.exp(rc * lg))

    for c in range(tm // CHUNK):
        rows = pl.ds(c * CHUNK, CHUNK)
        for g in range(SGU_GROUPS):
            cols = slice(g * SGU_GROUP_DIM, (g + 1) * SGU_GROUP_DIM)
            mixed = _dot(ws_ref[g], vn_s[rows, cols]) + bs_ref[g]
            cat_s[rows, cols] = (u_s[rows, cols] * mixed).astype(BF16)
    for c in range(tm // rc):
        rows = pl.ds(c * rc, rc)
        for hd in range(RET_HEADS):
            kcols = slice(hd * RET_DK, (hd + 1) * RET_DK)
            vcols = slice(hd * RET_DV, (hd + 1) * RET_DV)
            sc = _dot(q_s[rows, kcols], k_s[kcols, rows].astype(BF16)) * decay[hd]
            oi_s[rows, vcols] = _dot(sc.astype(BF16), vr_s[rows, vcols])
    for c in range(tm // rc):
        rows = pl.ds(c * rc, rc)
        for hd in range(RET_HEADS):
            kcols = slice(hd * RET_DK, (hd + 1) * RET_DK)
            vcols = slice(hd * RET_DV, (hd + 1) * RET_DV)
            qc = q_s[rows, kcols]
            kt = k_s[kcols, rows]
            vc = vr_s[rows, vcols]
            s_prev = state[hd]
            o = oi_s[rows, vcols] + _dot(qc, s_prev.astype(BF16)) * xi[hd]
            state[hd] = s_prev * cdec[hd] + _dot((kt * zeta[hd]).astype(BF16), vc)
            ocols = slice(D_MODEL + hd * RET_DV, D_MODEL + (hd + 1) * RET_DV)
            cat_s[rows, ocols] = _head_norm_gate(o, gate_s[rows, vcols]).astype(BF16)
    xo_ref[0] = x + _dot(cat_s[...], wout_ref[...])

    @pl.when(t == pl.num_programs(1) - 1)
    def _():
        sout_ref[0] = state[...]


def _even_prompt(x, j, g, w_in, ln_g, ln_b, w_s, b_s, w_out, cosf, sinf):
    b, l, d = x.shape
    tm = TM_PROMPT
    return pl.pallas_call(
        _even_prompt_kernel,
        grid=(b, l // tm),
        in_specs=[pl.BlockSpec((1, tm, d), lambda i, t: (i, t, 0)),
                  _const((1, d)), _layer(j, (d, EVEN_IN)), _const((1, d)), _const((1, d)),
                  _const((SGU_GROUPS, CHUNK, CHUNK)), _const((SGU_GROUPS, CHUNK, 1)),
                  _layer(j, (EVEN_OUT, d)),
                  pl.BlockSpec((tm, RET_DK), lambda i, t: (t, 0)),
                  pl.BlockSpec((tm, RET_DK), lambda i, t: (t, 0))],
        out_specs=[pl.BlockSpec((1, tm, d), lambda i, t: (i, t, 0)),
                   pl.BlockSpec((1, RET_HEADS, RET_DK, RET_DV), lambda i, t: (i, 0, 0, 0))],
        out_shape=[jax.ShapeDtypeStruct((b, l, d), F32),
                   jax.ShapeDtypeStruct((b, RET_HEADS, RET_DK, RET_DV), F32)],
        scratch_shapes=[pltpu.VMEM((RET_HEADS, RET_DK, RET_DV), F32),
                        pltpu.VMEM((tm, d), F32), pltpu.VMEM((tm, d), BF16),
                        pltpu.VMEM((tm, RET_HEADS * RET_DK), BF16),
                        pltpu.VMEM((RET_HEADS * RET_DK, tm), F32),
                        pltpu.VMEM((tm, d), BF16), pltpu.VMEM((tm, d), F32),
                        pltpu.VMEM((tm, EVEN_OUT), BF16), pltpu.VMEM((tm, d), F32)],
        compiler_params=_params(2),
        name="even_prompt",
    )(x, g, w_in, ln_g, ln_b, w_s, b_s, w_out, cosf, sinf)


def _even_sample_kernel(x_ref, s0_ref, g_ref, win_ref, lng_ref, lnb_ref, ws_ref, bs_ref, wout_ref,
                        cos_ref, sin_ref, xo_ref, sout_ref, vrow_ref,
                        q_s, kz_s, v_s, o_s):
    tt = x_ref.shape[0]
    ls = tt // s0_ref.shape[0]
    x = x_ref[...]
    u, vn, qs, ks, vr, gate = _even_front(x, g_ref, win_ref, lng_ref, lnb_ref, cos_ref, sin_ref)
    vrow_ref[...] = vn
    vn_b = vn.astype(BF16)
    vr_b = vr.astype(BF16)

    row = lax.broadcasted_iota(jnp.int32, (tt, tt), 0)
    col = lax.broadcasted_iota(jnp.int32, (tt, tt), 1)
    same = (row // ls) == (col // ls)
    diff = (row - col).astype(F32)
    rloc = (lax.broadcasted_iota(jnp.int32, (tt, 1), 0) % ls).astype(F32)

    cat = []
    for g in range(SGU_GROUPS):
        cols = slice(g * SGU_GROUP_DIM, (g + 1) * SGU_GROUP_DIM)
        mixed = _dot(ws_ref[g], vn_b[:, cols]) + bs_ref[g]
        cat.append((u[:, cols] * mixed).astype(BF16))

    xi, cdec = [], []
    for hd in range(RET_HEADS):
        lg = _log_gamma(hd)
        kcols = slice(hd * RET_DK, (hd + 1) * RET_DK)
        vcols = slice(hd * RET_DV, (hd + 1) * RET_DV)
        decay = jnp.where(same & (diff >= 0), jnp.exp(jnp.maximum(diff, 0.0) * lg), 0.0)
        xi.append(jnp.exp((rloc[:ls] + 1.0) * lg))
        cdec.append(math.exp(ls * lg))
        zeta = jnp.exp((ls - 1.0 - rloc) * lg)
        sc = _dot_nt(qs[hd].astype(BF16), ks[hd].astype(BF16)) * decay
        o_s[:, vcols] = _dot(sc.astype(BF16), vr_b[:, vcols])
        q_s[:, kcols] = qs[hd]
        kz_s[:, kcols] = ks[hd] * zeta
    v_s[...] = vr

    def per_batch(bi, carry):
        rows = pl.ds(pl.multiple_of(bi * ls, ls), ls)
        for hd in range(RET_HEADS):
            kcols = slice(hd * RET_DK, (hd + 1) * RET_DK)
            vcols = slice(hd * RET_DV, (hd + 1) * RET_DV)
            s_prev = s0_ref[bi, hd]
            qb = q_s[rows, kcols].astype(BF16)
            o_s[rows, vcols] += _dot(qb, s_prev.astype(BF16)) * xi[hd]
            sout_ref[bi, hd] = s_prev * cdec[hd] + _dot_tn(kz_s[rows, kcols].astype(BF16),
                                                          v_s[rows, vcols].astype(BF16))
        return carry

    lax.fori_loop(0, s0_ref.shape[0], per_batch, 0, unroll=2)
    for hd in range(RET_HEADS):
        vcols = slice(hd * RET_DV, (hd + 1) * RET_DV)
        cat.append(_head_norm_gate(o_s[:, vcols], gate[:, vcols]).astype(BF16))
    xo_ref[...] = x + _dot(jnp.concatenate(cat, axis=-1), wout_ref[...])


def _even_sample(x, s0, j, g, w_in, ln_g, ln_b, w_s_bd, b_s_bd, w_out, cosf, sinf):
    n, d = x.shape
    nb = s0.shape[0]
    ls = n // nb
    bt = BT_EVEN_SAMPLE
    tt = bt * ls
    state_spec = pl.BlockSpec((bt, RET_HEADS, RET_DK, RET_DV), lambda i: (i, 0, 0, 0))
    return pl.pallas_call(
        _even_sample_kernel,
        grid=(nb // bt,),
        in_specs=[pl.BlockSpec((tt, d), lambda i: (i, 0)), state_spec,
                  _const((1, d)), _layer(j, (d, EVEN_IN)), _const((1, d)), _const((1, d)),
                  _const((SGU_GROUPS, tt, tt)), _const((SGU_GROUPS, tt, 1)),
                  _layer(j, (EVEN_OUT, d)), _const((tt, RET_DK)), _const((tt, RET_DK))],
        out_specs=[pl.BlockSpec((tt, d), lambda i: (i, 0)), state_spec,
                   pl.BlockSpec((tt, d), lambda i: (i, 0))],
        out_shape=[jax.ShapeDtypeStruct((n, d), F32),
                   jax.ShapeDtypeStruct(s0.shape, F32),
                   jax.ShapeDtypeStruct((n, d), F32)],
        scratch_shapes=[pltpu.VMEM((tt, RET_HEADS * RET_DK), F32),
                        pltpu.VMEM((tt, RET_HEADS * RET_DK), F32),
                        pltpu.VMEM((tt, d), F32), pltpu.VMEM((tt, d), F32)],
        compiler_params=_params(1),
        name="even_sample",
    )(x, s0, g, w_in, ln_g, ln_b, w_s_bd, b_s_bd, w_out, cosf, sinf)


def _pad_even_odd(blk, kv_in_block):
    lane = lax.broadcasted_iota(jnp.int32, blk.shape, 1)
    rolled = pltpu.roll(blk, C_HEAD_DIM, 1)
    if kv_in_block == 0:
        even = jnp.where(lane < C_HEAD_DIM, blk, 0.0)
        odd = jnp.where(lane >= C_HEAD_DIM, rolled, 0.0)
    else:
        odd = jnp.where(lane >= C_HEAD_DIM, blk, 0.0)
        even = jnp.where(lane < C_HEAD_DIM, rolled, 0.0)
    return even.astype(BF16), odd.astype(BF16)


def _pair_normalise(outs, dens):
    lane = lax.broadcasted_iota(jnp.int32, outs[0].shape, 1)
    inv = jnp.where(lane < C_HEAD_DIM, 1.0 / dens[0], 1.0 / dens[1])
    return (outs[0] + outs[1]) * inv


def _window_attend(q2, ke, ko, ve, vo, valid, sink_e, sink_o):
    outs, dens = [], []
    for kk, vv, sink in ((ke, ve, sink_e), (ko, vo, sink_o)):
        s = jnp.where(valid, _dot_nt(q2, kk), NEG)
        m = jnp.maximum(jnp.max(s, axis=-1, keepdims=True), sink)
        p = jnp.exp2(s - m)
        dens.append(jnp.sum(p, axis=-1, keepdims=True) + jnp.exp2(sink - m))
        outs.append(_dot(p.astype(BF16), vv))
    return _pair_normalise(outs, dens)


def _banded_attend(q2, ke, ko, ve, vo, upper, prev_bias, sink_e, sink_o):
    w = upper.shape[1]
    zero = jnp.zeros((), F32)
    outs, dens = [], []
    for kk, vv, sink in ((ke, ve, sink_e), (ko, vo, sink_o)):
        s2 = _dot(q2, kk)
        s_prev = s2[:, :w] if prev_bias is None else s2[:, :w] + prev_bias
        s = jnp.where(upper, s_prev, s2[:, w:])
        m = jnp.maximum(jnp.max(s, axis=-1, keepdims=True), sink)
        p = jnp.exp2(s - m)
        dens.append(jnp.sum(p, axis=-1, keepdims=True) + jnp.exp2(sink - m))
        pp = jnp.concatenate([jnp.where(upper, p, zero), jnp.where(upper, zero, p)], axis=1)
        outs.append(_dot(pp.astype(BF16), vv))
    return _pair_normalise(outs, dens)


def _odd_front(x, g_ref, wqkv_ref, bqkv_ref, cos_ref, lo_ref, hi_ref):
    h = _rms(x, g_ref[...]).astype(BF16)
    qkv = _dot(h, wqkv_ref[...]) + bqkv_ref[...]
    cos, lo, hi = cos_ref[...], lo_ref[...], hi_ref[...]
    nq = C_HEADS * C_HEAD_DIM
    nk = C_KV_HEADS * C_HEAD_DIM
    qb = [_win_rotate(qkv[:, j * LANES:(j + 1) * LANES], cos, lo, hi) * (C_HEAD_DIM ** -0.5 * LOG2E)
          for j in range(nq // LANES)]
    kb = [_win_rotate(qkv[:, nq + j * LANES:nq + (j + 1) * LANES], cos, lo, hi)
          for j in range(nk // LANES)]
    v = qkv[:, nq + nk:]
    return qb, kb, v


def _odd_prompt_kernel(sink_ref, x_ref, g_ref, wqkv_ref, bqkv_ref, wout_ref, bout_ref,
                       cos_ref, lo_ref, hi_ref, xo_ref, wk_ref, wv_ref,
                       q_s, kpad, kprev, vpad, att_s):
    t = pl.program_id(1)
    tm = x_ref.shape[1]
    w = CHUNK
    x = x_ref[0]
    qb, kb, v = _odd_front(x, g_ref, wqkv_ref, bqkv_ref, cos_ref, lo_ref, hi_ref)
    for j, blk in enumerate(qb):
        q_s[:, j * LANES:(j + 1) * LANES] = blk.astype(BF16)

    @pl.when(t == 0)
    def _():
        kprev[...] = jnp.zeros((2 * C_KV_HEADS, LANES, w), BF16)
        vpad[:, 0:w, :] = jnp.zeros((2 * C_KV_HEADS, w, LANES), BF16)

    kt = [blk.T for blk in kb]
    k_tails = []
    for kh in range(C_KV_HEADS):
        j, r = divmod(kh, LANES // C_HEAD_DIM)
        head = kt[j][r * C_HEAD_DIM:(r + 1) * C_HEAD_DIM, :]
        zeros = jnp.zeros_like(head)
        ke_t = jnp.concatenate([head, zeros], axis=0).astype(BF16)
        ko_t = jnp.concatenate([zeros, head], axis=0).astype(BF16)
        kpad[2 * kh] = ke_t
        kpad[2 * kh + 1] = ko_t
        k_tails += [ke_t[:, tm - w:], ko_t[:, tm - w:]]
        ve, vo = _pad_even_odd(v[:, j * LANES:(j + 1) * LANES], r)
        vpad[2 * kh, w:, :] = ve
        vpad[2 * kh + 1, w:, :] = vo

    @pl.when(t == pl.num_programs(1) - 1)
    def _():
        for j, blk in enumerate(kb):
            wk_ref[0, :, j * LANES:(j + 1) * LANES] = blk[tm - w:, :]
        wv_ref[0] = v[tm - w:, :]

    qi = lax.broadcasted_iota(jnp.int32, (2 * w, w), 0) % w
    kj = lax.broadcasted_iota(jnp.int32, (2 * w, w), 1)
    upper = kj > qi
    first_half = lax.broadcasted_iota(jnp.int32, (2 * w, 1), 0) < w

    def block(c):
        rows = pl.ds(c * w, w)
        keys = pl.ds(c * w, 2 * w)
        kkeys = pl.ds((c - 1) * w, 2 * w)
        prev_bias = jnp.where(t == 0, NEG, 0.0) if c == 0 else None
        for kh in range(C_KV_HEADS):
            if c == 0:
                ke = jnp.concatenate([kprev[2 * kh], kpad[2 * kh, :, 0:w]], axis=1)
                ko = jnp.concatenate([kprev[2 * kh + 1], kpad[2 * kh + 1, :, 0:w]], axis=1)
            else:
                ke, ko = kpad[2 * kh, :, kkeys], kpad[2 * kh + 1, :, kkeys]
            ve, vo = vpad[2 * kh, keys, :], vpad[2 * kh + 1, keys, :]
            c0 = (2 * kh) * LANES
            q2 = jnp.concatenate([q_s[rows, c0:c0 + LANES], q_s[rows, c0 + LANES:c0 + 2 * LANES]], 0)
            h0 = kh * (C_HEADS // C_KV_HEADS)
            sink_e = jnp.where(first_half, sink_ref[h0], sink_ref[h0 + 2]) * LOG2E
            sink_o = jnp.where(first_half, sink_ref[h0 + 1], sink_ref[h0 + 3]) * LOG2E
            o = _banded_attend(q2, ke, ko, ve, vo, upper, prev_bias, sink_e, sink_o)
            att_s[rows, c0:c0 + LANES] = o[:w].astype(BF16)
            att_s[rows, c0 + LANES:c0 + 2 * LANES] = o[w:].astype(BF16)

    for c in range(tm // w):
        block(c)
    xo_ref[0] = x + _dot(att_s[...], wout_ref[...]) + bout_ref[...]
    for i, tail in enumerate(k_tails):
        kprev[i] = tail
    vpad[:, 0:w, :] = vpad[:, tm:tm + w, :]


def _odd_prompt(x, j, sinks, g, w_qkv, b_qkv, w_out, b_out, cos, lo, hi):
    b, l, d = x.shape
    tm = TM_PROMPT_WIDE
    kvw = C_KV_HEADS * C_HEAD_DIM
    win_spec = pl.BlockSpec((1, CHUNK, kvw), lambda i, t: (i, 0, 0))
    tab = pl.BlockSpec((tm, LANES), lambda i, t: (t, 0))
    return pl.pallas_call(
        _odd_prompt_kernel,
        grid=(b, l // tm),
        in_specs=[pl.BlockSpec(memory_space=pltpu.SMEM),
                  pl.BlockSpec((1, tm, d), lambda i, t: (i, t, 0)),
                  _const((1, d)), _layer(j, (d, ODD_IN)), _const((1, ODD_IN)),
                  _layer(j, (d, d)), _const((1, d)), tab, tab, tab],
        out_specs=[pl.BlockSpec((1, tm, d), lambda i, t: (i, t, 0)), win_spec, win_spec],
        out_shape=[jax.ShapeDtypeStruct((b, l, d), F32),
                   jax.ShapeDtypeStruct((b, CHUNK, kvw), F32),
                   jax.ShapeDtypeStruct((b, CHUNK, kvw), F32)],
        scratch_shapes=[pltpu.VMEM((tm, d), BF16),
                        pltpu.VMEM((2 * C_KV_HEADS, LANES, tm), BF16),
                        pltpu.VMEM((2 * C_KV_HEADS, LANES, CHUNK), BF16),
                        pltpu.VMEM((2 * C_KV_HEADS, tm + CHUNK, LANES), BF16),
                        pltpu.VMEM((tm, d), BF16)],
        compiler_params=_params(2),
        name="odd_prompt",
    )(sinks, x, g, w_qkv, b_qkv, w_out, b_out, cos, lo, hi)


def _odd_sample_kernel(sink_ref, x_ref, ck_ref, cv_ref, g_ref, wqkv_ref, bqkv_ref, wout_ref, bout_ref,
                       cos_ref, lo_ref, hi_ref, xo_ref, wk_ref, wv_ref, q_s, k_s, v_s, att_s, *, ls):
    i = pl.program_id(0)
    bt, w, kvw = ck_ref.shape
    tt = bt * ls

    @pl.when(i == 0)
    def _():
        qb, kb, v = _odd_front(x_ref[...], g_ref, wqkv_ref, bqkv_ref, cos_ref, lo_ref, hi_ref)
        for j, blk in enumerate(qb):
            q_s[:, j * LANES:(j + 1) * LANES] = blk.astype(BF16)
        for j, blk in enumerate(kb):
            k_s[:, j * LANES:(j + 1) * LANES] = blk
        v_s[...] = v

    rows = pl.ds(pl.multiple_of(i * tt, tt), tt)
    knew = k_s[rows, :]
    vnew = v_s[rows, :]
    ck = ck_ref[...]
    cv = cv_ref[...]
    for bi in range(bt):
        wk_ref[bi, 0:w - ls, :] = ck[bi, ls:, :]
        wk_ref[bi, w - ls:, :] = knew[bi * ls:(bi + 1) * ls, :]
        wv_ref[bi, 0:w - ls, :] = cv[bi, ls:, :]
        wv_ref[bi, w - ls:, :] = vnew[bi * ls:(bi + 1) * ls, :]
    kall = jnp.concatenate([ck.reshape(bt * w, kvw), knew], axis=0)
    vall = jnp.concatenate([cv.reshape(bt * w, kvw), vnew], axis=0)
    ns = bt * w + tt

    tok = lax.broadcasted_iota(jnp.int32, (2 * tt, ns), 0) % tt
    key = lax.broadcasted_iota(jnp.int32, (2 * tt, ns), 1)
    tb, tl = tok // ls, tok % ls
    cached = key < bt * w
    nkey = key - bt * w
    valid = ((cached & ((key // w) == tb) & ((key % w) > tl))
             | ((key >= bt * w) & ((nkey // ls) == tb) & ((nkey % ls) <= tl)))
    first_half = lax.broadcasted_iota(jnp.int32, (2 * tt, 1), 0) < tt

    for kh in range(C_KV_HEADS):
        j, r = divmod(kh, LANES // C_HEAD_DIM)
        ke, ko = _pad_even_odd(kall[:, j * LANES:(j + 1) * LANES], r)
        ve, vo = _pad_even_odd(vall[:, j * LANES:(j + 1) * LANES], r)
        c0 = (2 * kh) * LANES
        q2 = jnp.concatenate([q_s[rows, c0:c0 + LANES], q_s[rows, c0 + LANES:c0 + 2 * LANES]], axis=0)
        h0 = kh * (C_HEADS // C_KV_HEADS)
        sink_e = jnp.where(first_half, sink_ref[h0], sink_ref[h0 + 2]) * LOG2E
        sink_o = jnp.where(first_half, sink_ref[h0 + 1], sink_ref[h0 + 3]) * LOG2E
        o = _window_attend(q2, ke, ko, ve, vo, valid, sink_e, sink_o)
        att_s[rows, c0:c0 + LANES] = o[:tt].astype(BF16)
        att_s[rows, c0 + LANES:c0 + 2 * LANES] = o[tt:].astype(BF16)

    @pl.when(i == pl.num_programs(0) - 1)
    def _():
        xo_ref[...] = x_ref[...] + _dot(att_s[...], wout_ref[...]) + bout_ref[...]


def _odd_sample(x, ck, cv, j, sinks, g, w_qkv, b_qkv, w_out, b_out, cos, lo, hi):
    n, d = x.shape
    nb, w, kvw = ck.shape
    bt = BT_SAMPLE
    cache_spec = pl.BlockSpec((bt, w, kvw), lambda i: (i, 0, 0))
    return pl.pallas_call(
        functools.partial(_odd_sample_kernel, ls=n // nb),
        grid=(nb // bt,),
        in_specs=[pl.BlockSpec(memory_space=pltpu.SMEM),
                  _const((n, d)), cache_spec, cache_spec,
                  _const((1, d)), _layer(j, (d, ODD_IN)), _const((1, ODD_IN)),
                  _layer(j, (d, d)), _const((1, d)),
                  _const((n, LANES)), _const((n, LANES)), _const((n, LANES))],
        out_specs=[pl.BlockSpec((n, d), lambda i: (0, 0)), cache_spec, cache_spec],
        out_shape=[jax.ShapeDtypeStruct((n, d), F32),
                   jax.ShapeDtypeStruct(ck.shape, F32), jax.ShapeDtypeStruct(cv.shape, F32)],
        scratch_shapes=[pltpu.VMEM((n, d), BF16), pltpu.VMEM((n, kvw), F32), pltpu.VMEM((n, kvw), F32),
                        pltpu.VMEM((n, d), BF16)],
        compiler_params=_params(1),
        name="odd_sample",
    )(sinks, x, ck, cv, g, w_qkv, b_qkv, w_out, b_out, cos, lo, hi)


def _cross_heads(q, mk, mv):
    outs = []
    for hd in range(MEM_HEADS):
        cols = slice(hd * MEM_HEAD_DIM, (hd + 1) * MEM_HEAD_DIM)
        s = _dot_nt(q[:, cols], mk[:, cols])
        p = jnp.exp(s - jnp.max(s, axis=-1, keepdims=True))
        den = jnp.sum(p, axis=-1, keepdims=True)
        outs.append(_dot(p.astype(BF16), mv[:, cols]) * (1.0 / den))
    return outs


def _mlp(x1, gf_ref, wup_ref, wdown_ref):
    h2 = _rms(x1, gf_ref[...]).astype(BF16)
    acc = x1
    for j in range(D_FF // D_MODEL):
        sl = slice(j * D_MODEL, (j + 1) * D_MODEL)
        a = jnp.square(jnp.maximum(_dot(h2, wup_ref[:, sl]), 0.0)).astype(BF16)
        acc = acc + _dot(a, wdown_ref[sl, :])
    return acc


def _cross_mlp_prompt_kernel(x_ref, mk_ref, mv_ref, gc_ref, wmq_ref, wmo_ref, gf_ref, wup_ref,
                             wdown_ref, gfin_ref, xo_ref, o_s, *, final):
    x = x_ref[0]
    h = _rms(x, gc_ref[...]).astype(BF16)
    q = (_dot(h, wmq_ref[...]) * (MEM_HEAD_DIM ** -0.5)).astype(BF16)
    outs = _cross_heads(q, mk_ref[0, 0], mv_ref[0, 0])
    for hd, o in enumerate(outs):
        o_s[:, hd * MEM_HEAD_DIM:(hd + 1) * MEM_HEAD_DIM] = o.astype(BF16)
    x1 = x + _dot(o_s[...], wmo_ref[...])
    y = _mlp(x1, gf_ref, wup_ref, wdown_ref)
    if final:
        y = _rms(y, gfin_ref[...])
    xo_ref[0] = y


def _cross_mlp_prompt(x, mk, mv, layer, g_cross, w_mq, w_mo, g_ffn, w_up, w_down, g_final, final):
    b, l, d = x.shape
    tm = TM_PROMPT_WIDE
    mem_spec = pl.BlockSpec((1, 1, N_MEM, d), lambda i, t: (layer, i, 0, 0))
    return pl.pallas_call(
        functools.partial(_cross_mlp_prompt_kernel, final=final),
        grid=(b, l // tm),
        in_specs=[pl.BlockSpec((1, tm, d), lambda i, t: (i, t, 0)), mem_spec, mem_spec,
                  _const((1, d)), _layer(layer, (d, d)), _layer(layer, (d, d)), _const((1, d)),
                  _const((d, D_FF)), _const((D_FF, d)), _const((1, d))],
        out_specs=pl.BlockSpec((1, tm, d), lambda i, t: (i, t, 0)),
        out_shape=jax.ShapeDtypeStruct((b, l, d), F32),
        scratch_shapes=[pltpu.VMEM((tm, d), BF16)],
        compiler_params=_params(2),
        name=f"cross_mlp_prompt_{layer}",
    )(x, mk, mv, g_cross, w_mq, w_mo, g_ffn, w_up, w_down, g_final)


def _cross_sample_kernel(x_ref, mk_ref, mv_ref, gc_ref, wmq_ref, wmo_ref, xo_ref, q_s, o_s):
    tt = x_ref.shape[0]
    bt = mk_ref.shape[1]
    ls = tt // bt
    nrow = N_MEM * MEM_HEADS
    x = x_ref[...]
    h = _rms(x, gc_ref[...]).astype(BF16)
    q_s[...] = _dot(h, wmq_ref[...]) * (MEM_HEAD_DIM ** -0.5)
    qhead = lax.broadcasted_iota(jnp.int32, (MEM_HEADS * ls, nrow), 0) // ls
    khead = lax.broadcasted_iota(jnp.int32, (MEM_HEADS * ls, nrow), 1) % MEM_HEADS
    own = qhead == khead

    def per_batch(bi, carry):
        rows = pl.ds(pl.multiple_of(bi * ls, ls), ls)
        qb = q_s[rows, :]
        q4 = jnp.concatenate([qb[:, hd * MEM_HEAD_DIM:(hd + 1) * MEM_HEAD_DIM]
                              for hd in range(MEM_HEADS)], axis=0).astype(BF16)
        k2 = mk_ref[0, bi].reshape(nrow, MEM_HEAD_DIM).astype(BF16)
        v2 = mv_ref[0, bi].reshape(nrow, MEM_HEAD_DIM).astype(BF16)
        s = jnp.where(own, _dot_nt(q4, k2), NEG)
        p = jnp.exp(s - jnp.max(s, axis=-1, keepdims=True))
        den = jnp.sum(p, axis=-1, keepdims=True)
        o4 = _dot(p.astype(BF16), v2) * (1.0 / den)
        for hd in range(MEM_HEADS):
            o_s[rows, hd * MEM_HEAD_DIM:(hd + 1) * MEM_HEAD_DIM] = o4[hd * ls:(hd + 1) * ls]
        return carry

    lax.fori_loop(0, bt, per_batch, 0, unroll=True)
    xo_ref[...] = x + _dot(o_s[...].astype(BF16), wmo_ref[...])


def _cross_sample(x, mk, mv, layer, g_cross, w_mq, w_mo):
    n, d = x.shape
    nb = mk.shape[1]
    ls = n // nb
    bt = BT_SAMPLE
    tt = bt * ls
    mem_spec = pl.BlockSpec((1, bt, N_MEM, MEM_HEADS, MEM_HEAD_DIM), lambda i: (layer, i, 0, 0, 0))
    return pl.pallas_call(
        _cross_sample_kernel,
        grid=(nb // bt,),
        in_specs=[pl.BlockSpec((tt, d), lambda i: (i, 0)), mem_spec, mem_spec,
                  _const((1, d)), _layer(layer, (d, d)), _layer(layer, (d, d))],
        out_specs=pl.BlockSpec((tt, d), lambda i: (i, 0)),
        out_shape=jax.ShapeDtypeStruct((n, d), F32),
        scratch_shapes=[pltpu.VMEM((tt, d), F32), pltpu.VMEM((tt, d), F32)],
        compiler_params=_params(1),
        name=f"cross_sample_{layer}",
    )(x, mk, mv, g_cross, w_mq, w_mo)


def _mlp_stream_kernel(x_ref, gf_ref, wup_ref, wdown_ref, gfin_ref, xo_ref, wup_b_ref, wdown_b_ref,
                       h_s, acc_s, *, final):
    j = pl.program_id(0)

    @pl.when(j == 0)
    def _():
        x = x_ref[...]
        h_s[...] = _rms(x, gf_ref[...]).astype(BF16)
        acc_s[...] = x

    wu = wup_ref[...].astype(BF16)
    wd = wdown_ref[...].astype(BF16)
    wup_b_ref[...] = wu
    wdown_b_ref[...] = wd
    a = jnp.square(jnp.maximum(_dot(h_s[...], wu), 0.0)).astype(BF16)
    acc_s[...] += _dot(a, wd)

    @pl.when(j == pl.num_programs(0) - 1)
    def _():
        y = acc_s[...]
        if final:
            y = _rms(y, gfin_ref[...])
        xo_ref[...] = y


def _mlp_sample(x, layer, g_ffn, w_up, w_down, g_final, final):
    n, d = x.shape
    fc = FF_CHUNK
    return pl.pallas_call(
        functools.partial(_mlp_stream_kernel, final=final),
        grid=(D_FF // fc,),
        in_specs=[_const((n, d)), _const((1, d)),
                  pl.BlockSpec((None, d, fc), lambda c: (layer, 0, c)),
                  pl.BlockSpec((None, fc, d), lambda c: (layer, c, 0)),
                  _const((1, d))],
        out_specs=[pl.BlockSpec((n, d), lambda c: (0, 0)),
                   pl.BlockSpec((d, fc), lambda c: (0, c)),
                   pl.BlockSpec((fc, d), lambda c: (c, 0))],
        out_shape=[jax.ShapeDtypeStruct((n, d), F32),
                   jax.ShapeDtypeStruct((d, D_FF), BF16),
                   jax.ShapeDtypeStruct((D_FF, d), BF16)],
        scratch_shapes=[pltpu.VMEM((n, d), BF16), pltpu.VMEM((n, d), F32)],
        compiler_params=_params(1),
        name=f"mlp_sample_{layer}",
    )(x, g_ffn, w_up, w_down, g_final)


def kernel(x_prompt, x_sample, state_ret, cache_win_k, cache_win_v, cache_mem_k, cache_mem_v, mem_prompt, g_mix, w_in_e, sgu_ln_g, sgu_ln_b, w_spatial, b_spatial, w_out_e, w_qkv_o, b_qkv_o, sinks, w_out_o, b_out_o, g_cross, g_mem, w_mq, w_mk, w_mv, w_mo, g_ffn, w_up, w_down, g_final):
    bp, lp, d = x_prompt.shape
    bs, ls, _ = x_sample.shape
    assert TM_PROMPT % RET_CHUNK_PROMPT == 0 and RET_CHUNK_PROMPT % CHUNK == 0
    assert d == D_MODEL and lp % TM_PROMPT == 0 and lp % TM_PROMPT_WIDE == 0 and bs % BT_SAMPLE == 0 and bs % BT_EVEN_SAMPLE == 0
    assert ls < CHUNK and CHUNK % ls == 0 and D_FF % FF_CHUNK == 0 and bp % MEMKV_BT == 0
    ns = bs * ls
    kvw = C_KV_HEADS * C_HEAD_DIM

    def row(v):
        return v.reshape(1, -1).astype(F32)

    def bf(w):
        return w.astype(BF16)

    pos_p = np.arange(lp)
    ret_tab_p = _ret_rope_tables(pos_p)
    tt_e = BT_EVEN_SAMPLE * ls
    ret_tab_s = _ret_rope_tables(PAST_LEN + (np.arange(tt_e) % ls))
    win_tab_p = _win_rope_tables(pos_p)
    win_tab_s = _win_rope_tables(PAST_LEN + (np.arange(ns) % ls))

    w_in_b, w_out_e_b, w_qkv_b, w_out_o_b = bf(w_in_e), bf(w_out_e), bf(w_qkv_o), bf(w_out_o)
    w_mq_b, w_mo_b = bf(w_mq), bf(w_mo)

    mk_p, mv_p, mk_pb, mv_pb = _memkv(mem_prompt, g_mem, w_mk, w_mv)
    mk_s, mv_s = cache_mem_k, cache_mem_v

    xp = x_prompt
    xs = x_sample.reshape(ns, d)
    final_g = row(g_final)
    outs = {}
    for layer in range(DEPTH):
        j = layer // 2
        g = row(g_mix[layer])
        if layer % 2 == 0:
            ln_g, ln_b = row(sgu_ln_g[j]), row(sgu_ln_b[j])
            tril = jnp.tril(jnp.ones((CHUNK, CHUNK), bool))
            w_s = bf(jnp.where(tril[None], w_spatial[j], 0.0))
            b_s = b_spatial[j].reshape(SGU_GROUPS, CHUNK, 1)
            xp, ret_p = _even_prompt(xp, j, g, w_in_b, ln_g, ln_b, w_s, b_s, w_out_e_b, *ret_tab_p)
            w_small = w_s[:, :ls, :ls]
            eye = jnp.eye(BT_EVEN_SAMPLE, dtype=BF16)
            w_bd = jnp.einsum("ab,gts->gatbs", eye, w_small).reshape(SGU_GROUPS, tt_e, tt_e)
            b_bd = jnp.tile(b_spatial[j][:, :ls], (1, BT_EVEN_SAMPLE)).reshape(SGU_GROUPS, tt_e, 1)
            xs, ret_s, v_rows = _even_sample(xs, state_ret[j], j, g, w_in_b, ln_g, ln_b, w_bd, b_bd,
                                             w_out_e_b, *ret_tab_s)
            outs["ret_p"], outs["ret_s"], outs["v_rows"] = ret_p, ret_s, v_rows
        else:
            b_qkv, b_out = row(b_qkv_o[j]), row(b_out_o[j])
            xp, wk_p, wv_p = _odd_prompt(xp, j, sinks[j], g, w_qkv_b, b_qkv, w_out_o_b, b_out,
                                         *win_tab_p)
            xs, wk_s, wv_s = _odd_sample(xs, cache_win_k[j].reshape(bs, CHUNK, kvw),
                                         cache_win_v[j].reshape(bs, CHUNK, kvw), j, sinks[j],
                                         g, w_qkv_b, b_qkv, w_out_o_b, b_out, *win_tab_s)
            outs["wk_p"], outs["wv_p"], outs["wk_s"], outs["wv_s"] = wk_p, wv_p, wk_s, wv_s
        final = layer == DEPTH - 1
        gc, gf = row(g_cross[layer]), row(g_ffn[layer])
        xs = _cross_sample(xs, mk_s, mv_s, layer, gc, w_mq_b, w_mo_b)
        xs, w_up_l, w_down_l = _mlp_sample(xs, layer, gf, w_up, w_down, final_g, final)
        xp = _cross_mlp_prompt(xp, mk_pb, mv_pb, layer, gc, w_mq_b, w_mo_b, gf, w_up_l, w_down_l,
                               final_g, final)

    n_even, n_odd = (DEPTH + 1) // 2, DEPTH // 2
    return (xp,
            xs.reshape(bs, ls, d),
            mk_p,
            mv_p,
            outs["ret_p"].reshape(n_even, bp, RET_HEADS, RET_DK, RET_DV),
            outs["ret_s"].reshape(n_even, bs, RET_HEADS, RET_DK, RET_DV),
            outs["v_rows"].reshape(n_even, bs, ls, SGU_GROUPS, SGU_GROUP_DIM),
            outs["wk_p"].reshape(n_odd, bp, CHUNK, C_KV_HEADS, C_HEAD_DIM),
            outs["wv_p"].reshape(n_odd, bp, CHUNK, C_KV_HEADS, C_HEAD_DIM),
            outs["wk_s"].reshape(n_odd, bs, CHUNK, C_KV_HEADS, C_HEAD_DIM),
            outs["wv_s"].reshape(n_odd, bs, CHUNK, C_KV_HEADS, C_HEAD_DIM))
```

```python
import functools
import math

import numpy as np
import jax
import jax.numpy as jnp
from jax import lax
from jax.experimental import pallas as pl
from jax.experimental.pallas import tpu as pltpu

F32 = jnp.float32
BF16 = jnp.bfloat16

D_MODEL = 1024
DEPTH = 2
PAST_LEN = 16384
EPS = 1e-6

SGU_GROUPS = 4
SGU_GROUP_DIM = 256
RET_HEADS = 4
RET_DK = 128
RET_DV = 256
RET_ROPE_BASE = 10000.0
CHUNK = 128
RET_CHUNK_PROMPT = 256
C_HEADS = 16
C_KV_HEADS = 4
C_HEAD_DIM = 64
C_ROPE_BASE = 150000.0
N_MEM = 256
MEM_HEADS = 4
MEM_HEAD_DIM = 256
D_FF = 4096
EVEN_IN = 5120
EVEN_OUT = 2048
ODD_IN = 1536

LANES = 128
VMEM_LIMIT = 56 * 2 ** 20
NEG = -1e30
LOG2E = math.log2(math.e)

TM_PROMPT = 512
TM_PROMPT_WIDE = 1024
BT_SAMPLE = 8
BT_EVEN_SAMPLE = 16
FF_CHUNK = 512
MEMKV_BT = 2


def _params(n_axes):
    return pltpu.CompilerParams(dimension_semantics=("arbitrary",) * n_axes,
                                vmem_limit_bytes=VMEM_LIMIT)


def _const(shape):
    nd = len(shape)
    return pl.BlockSpec(shape, lambda *_: (0,) * nd, pipeline_mode=pl.Buffered(1))


def _layer(layer, shape):
    nd = len(shape)
    return pl.BlockSpec((None,) + tuple(shape), lambda *_: (layer,) + (0,) * nd,
                        pipeline_mode=pl.Buffered(1))


def _rms(x, g):
    return x * lax.rsqrt(jnp.mean(x * x, axis=-1, keepdims=True) + EPS) * g


def _dot(a, b):
    return jnp.dot(a, b, preferred_element_type=F32)


def _dot_nt(a, b):
    return lax.dot_general(a, b, (((1,), (1,)), ((), ())), preferred_element_type=F32)


def _dot_tn(a, b):
    return lax.dot_general(a, b, (((0,), (0,)), ((), ())), preferred_element_type=F32)


def _ret_rope_tables(pos):
    half = RET_DK // 2
    inv = RET_ROPE_BASE ** (-np.arange(half, dtype=np.float64) / half)
    ang = pos.astype(np.float64)[:, None] * inv[None, :]
    cos, sin = np.cos(ang), np.sin(ang)
    return (jnp.asarray(np.concatenate([cos, cos], -1), F32),
            jnp.asarray(np.concatenate([-sin, sin], -1), F32))


def _win_rope_tables(pos):
    half = C_HEAD_DIM // 2
    inv = C_ROPE_BASE ** (-np.arange(half, dtype=np.float64) / half)
    ang = pos.astype(np.float64)[:, None] * inv[None, :]
    lane = np.arange(LANES)
    cos = np.cos(ang)[:, lane % half]
    sin = np.sin(ang)[:, lane % half]
    low = (lane % C_HEAD_DIM) < half
    return (jnp.asarray(cos, F32),
            jnp.asarray(np.where(low[None], -sin, 0.0), F32),
            jnp.asarray(np.where(low[None], 0.0, sin), F32))


def _ret_rotate(x, cosf, sinf):
    return x * cosf + pltpu.roll(x, RET_DK // 2, 1) * sinf


def _win_rotate(x, cos, sin_lo, sin_hi):
    return (x * cos + pltpu.roll(x, LANES - C_HEAD_DIM // 2, 1) * sin_lo
            + pltpu.roll(x, C_HEAD_DIM // 2, 1) * sin_hi)


def _log_gamma(h):
    return math.log1p(-2.0 ** (-5.0 - h))


def _memkv_kernel(mem_ref, g_ref, wk_ref, wv_ref, k_ref, v_ref, kb_ref, vb_ref, wk_s, wv_s):
    @pl.when(pl.program_id(1) == 0)
    def _():
        wk_s[...] = wk_ref[0].astype(BF16)
        wv_s[...] = wv_ref[0].astype(BF16)

    nb = mem_ref.shape[0]
    m = _rms(mem_ref[...].reshape(nb * N_MEM, D_MODEL), g_ref[0]).astype(BF16)
    for w_s, out_ref, bf_ref in ((wk_s, k_ref, kb_ref), (wv_s, v_ref, vb_ref)):
        kv = _dot(m, w_s[...])
        bf_ref[0] = kv.astype(BF16).reshape(nb, N_MEM, D_MODEL)
        for bi in range(nb):
            for hd in range(MEM_HEADS):
                out_ref[0, bi, :, hd, :] = kv[bi * N_MEM:(bi + 1) * N_MEM,
                                              hd * MEM_HEAD_DIM:(hd + 1) * MEM_HEAD_DIM]


def _memkv(mem, g_mem, w_mk, w_mv):
    b = mem.shape[0]
    nb = MEMKV_BT
    out = jax.ShapeDtypeStruct((DEPTH, b, N_MEM, MEM_HEADS, MEM_HEAD_DIM), F32)
    out_b = jax.ShapeDtypeStruct((DEPTH, b, N_MEM, D_MODEL), BF16)
    spec5 = pl.BlockSpec((1, nb, N_MEM, MEM_HEADS, MEM_HEAD_DIM), lambda l, i: (l, i, 0, 0, 0))
    spec_b = pl.BlockSpec((1, nb, N_MEM, D_MODEL), lambda l, i: (l, i, 0, 0))
    return pl.pallas_call(
        _memkv_kernel,
        grid=(DEPTH, b // nb),
        in_specs=[pl.BlockSpec((nb, N_MEM, D_MODEL), lambda l, i: (i, 0, 0)),
                  pl.BlockSpec((1, 1, D_MODEL), lambda l, i: (l, 0, 0)),
                  pl.BlockSpec((1, D_MODEL, D_MODEL), lambda l, i: (l, 0, 0)),
                  pl.BlockSpec((1, D_MODEL, D_MODEL), lambda l, i: (l, 0, 0))],
        out_specs=[spec5, spec5, spec_b, spec_b],
        out_shape=[out, out, out_b, out_b],
        scratch_shapes=[pltpu.VMEM((D_MODEL, D_MODEL), BF16), pltpu.VMEM((D_MODEL, D_MODEL), BF16)],
        compiler_params=_params(2),
        name="memkv",
    )(mem, g_mem.reshape(DEPTH, 1, D_MODEL), w_mk, w_mv)


def _even_front(x, g_ref, win_ref, lng_ref, lnb_ref, cos_ref, sin_ref):
    h = _rms(x, g_ref[...]).astype(BF16)

    def proj(lo, hi):
        return _dot(h, win_ref[:, lo:hi])

    u = jax.nn.gelu(proj(0, 1024))
    v = jax.nn.gelu(proj(1024, 2048))
    vc = v - jnp.mean(v, axis=-1, keepdims=True)
    vn = vc * lax.rsqrt(jnp.mean(vc * vc, axis=-1, keepdims=True) + EPS) * lng_ref[...] + lnb_ref[...]
    q = proj(2048, 2560)
    k = proj(2560, 3072)
    vr = proj(3072, 4096)
    gate = proj(4096, 5120)
    cosf, sinf = cos_ref[...], sin_ref[...]
    qs, ks = [], []
    for hd in range(RET_HEADS):
        cols = slice(hd * RET_DK, (hd + 1) * RET_DK)
        qs.append(_ret_rotate(q[:, cols], cosf, sinf))
        ks.append(_ret_rotate(k[:, cols], cosf, sinf) * (RET_DK ** -0.5))
    return u, vn, qs, ks, vr, gate


def _head_norm_gate(o, gate):
    on = o * lax.rsqrt(jnp.mean(o * o, axis=-1, keepdims=True) + EPS)
    return on * jax.nn.silu(gate)


def _even_prompt_kernel(x_ref, g_ref, win_ref, lng_ref, lnb_ref, ws_ref, bs_ref, wout_ref,
                        cos_ref, sin_ref, xo_ref, sout_ref,
                        state, u_s, vn_s, q_s, k_s, vr_s, gate_s, cat_s, oi_s):
    t = pl.program_id(1)
    tm = x_ref.shape[1]

    @pl.when(t == 0)
    def _():
        state[...] = jnp.zeros_like(state)

    x = x_ref[0]
    u, vn, qs, ks, vr, gate = _even_front(x, g_ref, win_ref, lng_ref, lnb_ref, cos_ref, sin_ref)
    u_s[...] = u
    vn_s[...] = vn.astype(BF16)
    for hd in range(RET_HEADS):
        cols = slice(hd * RET_DK, (hd + 1) * RET_DK)
        q_s[:, cols] = qs[hd].astype(BF16)
        k_s[cols, :] = ks[hd].T
    vr_s[...] = vr.astype(BF16)
    gate_s[...] = gate

    rc = RET_CHUNK_PROMPT
    row = lax.broadcasted_iota(jnp.int32, (rc, rc), 0)
    col = lax.broadcasted_iota(jnp.int32, (rc, rc), 1)
    diff = (row - col).astype(F32)
    ridx = lax.broadcasted_iota(jnp.int32, (rc, 1), 0).astype(F32)
    cidx = lax.broadcasted_iota(jnp.int32, (1, rc), 1).astype(F32)
    decay, xi, zeta, cdec = [], [], [], []
    for hd in range(RET_HEADS):
        lg = _log_gamma(hd)
        decay.append(jnp.where(diff >= 0, jnp.exp(jnp.maximum(diff, 0.0) * lg), 0.0))
        xi.append(jnp.exp((ridx + 1.0) * lg))
        zeta.append(jnp.exp((rc - 1.0 - cidx) * lg))
        cdec.append(math.exp(rc * lg))

    for c in range(tm // CHUNK):
        rows = pl.ds(c * CHUNK, CHUNK)
        for g in range(SGU_GROUPS):
            cols = slice(g * SGU_GROUP_DIM, (g + 1) * SGU_GROUP_DIM)
            mixed = _dot(ws_ref[g], vn_s[rows, cols]) + bs_ref[g]
            cat_s[rows, cols] = (u_s[rows, cols] * mixed).astype(BF16)
    for c in range(tm // rc):
        rows = pl.ds(c * rc, rc)
        for hd in range(RET_HEADS):
            kcols = slice(hd * RET_DK, (hd + 1) * RET_DK)
            vcols = slice(hd * RET_DV, (hd + 1) * RET_DV)
            sc = _dot(q_s[rows, kcols], k_s[kcols, rows].astype(BF16)) * decay[hd]
            oi_s[rows, vcols] = _dot(sc.astype(BF16), vr_s[rows, vcols])
    for c in range(tm // rc):
        rows = pl.ds(c * rc, rc)
        for hd in range(RET_HEADS):
            kcols = slice(hd * RET_DK, (hd + 1) * RET_DK)
            vcols = slice(hd * RET_DV, (hd + 1) * RET_DV)
            qc = q_s[rows, kcols]
            kt = k_s[kcols, rows]
            vc = vr_s[rows, vcols]
            s_prev = state[hd]
            o = oi_s[rows, vcols] + _dot(qc, s_prev.astype(BF16)) * xi[hd]
            state[hd] = s_prev * cdec[hd] + _dot((kt * zeta[hd]).astype(BF16), vc)
            ocols = slice(D_MODEL + hd * RET_DV, D_MODEL + (hd + 1) * RET_DV)
            cat_s[rows, ocols] = _head_norm_gate(o, gate_s[rows, vcols]).astype(BF16)
    xo_ref[0] = x + _dot(cat_s[...], wout_ref[...])

    @pl.when(t == pl.num_programs(1) - 1)
    def _():
        sout_ref[0] = state[...]


def _even_prompt(x, j, g, w_in, ln_g, ln_b, w_s, b_s, w_out, cosf, sinf):
    b, l, d = x.shape
    tm = TM_PROMPT
    return pl.pallas_call(
        _even_prompt_kernel,
        grid=(b, l // tm),
        in_specs=[pl.BlockSpec((1, tm, d), lambda i, t: (i, t, 0)),
                  _const((1, d)), _layer(j, (d, EVEN_IN)), _const((1, d)), _const((1, d)),
                  _const((SGU_GROUPS, CHUNK, CHUNK)), _const((SGU_GROUPS, CHUNK, 1)),
                  _layer(j, (EVEN_OUT, d)),
                  pl.BlockSpec((tm, RET_DK), lambda i, t: (t, 0)),
                  pl.BlockSpec((tm, RET_DK), lambda i, t: (t, 0))],
        out_specs=[pl.BlockSpec((1, tm, d), lambda i, t: (i, t, 0)),
                   pl.BlockSpec((1, RET_HEADS, RET_DK, RET_DV), lambda i, t: (i, 0, 0, 0))],
        out_shape=[jax.ShapeDtypeStruct((b, l, d), F32),
                   jax.ShapeDtypeStruct((b, RET_HEADS, RET_DK, RET_DV), F32)],
        scratch_shapes=[pltpu.VMEM((RET_HEADS, RET_DK, RET_DV), F32),
                        pltpu.VMEM((tm, d), F32), pltpu.VMEM((tm, d), BF16),
                        pltpu.VMEM((tm, RET_HEADS * RET_DK), BF16),
                        pltpu.VMEM((RET_HEADS * RET_DK, tm), F32),
                        pltpu.VMEM((tm, d), BF16), pltpu.VMEM((tm, d), F32),
                        pltpu.VMEM((tm, EVEN_OUT), BF16), pltpu.VMEM((tm, d), F32)],
        compiler_params=_params(2),
        name="even_prompt",
    )(x, g, w_in, ln_g, ln_b, w_s, b_s, w_out, cosf, sinf)


def _even_sample_kernel(x_ref, s0_ref, g_ref, win_ref, lng_ref, lnb_ref, ws_ref, bs_ref, wout_ref,
                        cos_ref, sin_ref, xo_ref, sout_ref, vrow_ref,
                        q_s, kz_s, v_s, o_s):
    tt = x_ref.shape[0]
    ls = tt // s0_ref.shape[0]
    x = x_ref[...]
    u, vn, qs, ks, vr, gate = _even_front(x, g_ref, win_ref, lng_ref, lnb_ref, cos_ref, sin_ref)
    vrow_ref[...] = vn
    vn_b = vn.astype(BF16)
    vr_b = vr.astype(BF16)

    row = lax.broadcasted_iota(jnp.int32, (tt, tt), 0)
    col = lax.broadcasted_iota(jnp.int32, (tt, tt), 1)
    same = (row // ls) == (col // ls)
    diff = (row - col).astype(F32)
    rloc = (lax.broadcasted_iota(jnp.int32, (tt, 1), 0) % ls).astype(F32)

    cat = []
    for g in range(SGU_GROUPS):
        cols = slice(g * SGU_GROUP_DIM, (g + 1) * SGU_GROUP_DIM)
        mixed = _dot(ws_ref[g], vn_b[:, cols]) + bs_ref[g]
        cat.append((u[:, cols] * mixed).astype(BF16))

    xi, cdec = [], []
    for hd in range(RET_HEADS):
        lg = _log_gamma(hd)
        kcols = slice(hd * RET_DK, (hd + 1) * RET_DK)
        vcols = slice(hd * RET_DV, (hd + 1) * RET_DV)
        decay = jnp.where(same & (diff >= 0), jnp.exp(jnp.maximum(diff, 0.0) * lg), 0.0)
        xi.append(jnp.exp((rloc[:ls] + 1.0) * lg))
        cdec.append(math.exp(ls * lg))
        zeta = jnp.exp((ls - 1.0 - rloc) * lg)
        sc = _dot_nt(qs[hd].astype(BF16), ks[hd].astype(BF16)) * decay
        o_s[:, vcols] = _dot(sc.astype(BF16), vr_b[:, vcols])
        q_s[:, kcols] = qs[hd]
        kz_s[:, kcols] = ks[hd] * zeta
    v_s[...] = vr

    def per_batch(bi, carry):
        rows = pl.ds(pl.multiple_of(bi * ls, ls), ls)
        for hd in range(RET_HEADS):
            kcols = slice(hd * RET_DK, (hd + 1) * RET_DK)
            vcols = slice(hd * RET_DV, (hd + 1) * RET_DV)
            s_prev = s0_ref[bi, hd]
            qb = q_s[rows, kcols].astype(BF16)
            o_s[rows, vcols] += _dot(qb, s_prev.astype(BF16)) * xi[hd]
            sout_ref[bi, hd] = s_prev * cdec[hd] + _dot_tn(kz_s[rows, kcols].astype(BF16),
                                                          v_s[rows, vcols].astype(BF16))
        return carry

    lax.fori_loop(0, s0_ref.shape[0], per_batch, 0, unroll=True)
    for hd in range(RET_HEADS):
        vcols = slice(hd * RET_DV, (hd + 1) * RET_DV)
        cat.append(_head_norm_gate(o_s[:, vcols], gate[:, vcols]).astype(BF16))
    xo_ref[...] = x + _dot(jnp.concatenate(cat, axis=-1), wout_ref[...])


def _even_sample(x, s0, j, g, w_in, ln_g, ln_b, w_s_bd, b_s_bd, w_out, cosf, sinf):
    n, d = x.shape
    nb = s0.shape[0]
    ls = n // nb
    bt = BT_EVEN_SAMPLE
    tt = bt * ls
    state_spec = pl.BlockSpec((bt, RET_HEADS, RET_DK, RET_DV), lambda i: (i, 0, 0, 0))
    return pl.pallas_call(
        _even_sample_kernel,
        grid=(nb // bt,),
        in_specs=[pl.BlockSpec((tt, d), lambda i: (i, 0)), state_spec,
                  _const((1, d)), _layer(j, (d, EVEN_IN)), _const((1, d)), _const((1, d)),
                  _const((SGU_GROUPS, tt, tt)), _const((SGU_GROUPS, tt, 1)),
                  _layer(j, (EVEN_OUT, d)), _const((tt, RET_DK)), _const((tt, RET_DK))],
        out_specs=[pl.BlockSpec((tt, d), lambda i: (i, 0)), state_spec,
                   pl.BlockSpec((tt, d), lambda i: (i, 0))],
        out_shape=[jax.ShapeDtypeStruct((n, d), F32),
                   jax.ShapeDtypeStruct(s0.shape, F32),
                   jax.ShapeDtypeStruct((n, d), F32)],
        scratch_shapes=[pltpu.VMEM((tt, RET_HEADS * RET_DK), F32),
                        pltpu.VMEM((tt, RET_HEADS * RET_DK), F32),
                        pltpu.VMEM((tt, d), F32), pltpu.VMEM((tt, d), F32)],
        compiler_params=_params(1),
        name="even_sample",
    )(x, s0, g, w_in, ln_g, ln_b, w_s_bd, b_s_bd, w_out, cosf, sinf)


def _pad_even_odd(blk, kv_in_block):
    lane = lax.broadcasted_iota(jnp.int32, blk.shape, 1)
    rolled = pltpu.roll(blk, C_HEAD_DIM, 1)
    if kv_in_block == 0:
        even = jnp.where(lane < C_HEAD_DIM, blk, 0.0)
        odd = jnp.where(lane >= C_HEAD_DIM, rolled, 0.0)
    else:
        odd = jnp.where(lane >= C_HEAD_DIM, blk, 0.0)
        even = jnp.where(lane < C_HEAD_DIM, rolled, 0.0)
    return even.astype(BF16), odd.astype(BF16)


def _sink_softmax(sc, sink):
    m = jnp.maximum(jnp.max(sc, axis=-1, keepdims=True), sink)
    p = jnp.exp2(sc - m)
    den = jnp.sum(p, axis=-1, keepdims=True) + jnp.exp2(sink - m)
    return p, jnp.broadcast_to(1.0 / den, (sc.shape[0], LANES))


def _pair_combine(o_even, o_odd, inv_even, inv_odd):
    lane = lax.broadcasted_iota(jnp.int32, o_even.shape, 1)
    return (o_even + o_odd) * jnp.where(lane < C_HEAD_DIM, inv_even, inv_odd)


def _pair_sinks(sink_ref, kh, first_half):
    h0 = kh * (C_HEADS // C_KV_HEADS)
    return (jnp.where(first_half, sink_ref[h0], sink_ref[h0 + 2]) * LOG2E,
            jnp.where(first_half, sink_ref[h0 + 1], sink_ref[h0 + 3]) * LOG2E)


def _odd_front(x, g_ref, wqkv_ref, bqkv_ref, cos_ref, lo_ref, hi_ref):
    h = _rms(x, g_ref[...]).astype(BF16)
    qkv = _dot(h, wqkv_ref[...]) + bqkv_ref[...]
    cos, lo, hi = cos_ref[...], lo_ref[...], hi_ref[...]
    nq = C_HEADS * C_HEAD_DIM
    nk = C_KV_HEADS * C_HEAD_DIM
    qb = [_win_rotate(qkv[:, j * LANES:(j + 1) * LANES], cos, lo, hi) * (C_HEAD_DIM ** -0.5 * LOG2E)
          for j in range(nq // LANES)]
    kb = [_win_rotate(qkv[:, nq + j * LANES:nq + (j + 1) * LANES], cos, lo, hi)
          for j in range(nk // LANES)]
    v = qkv[:, nq + nk:]
    return qb, kb, v


def _odd_prompt_kernel(sink_ref, x_ref, g_ref, wqkv_ref, bqkv_ref, wout_ref, bout_ref,
                       cos_ref, lo_ref, hi_ref, xo_ref, wk_ref, wv_ref,
                       q_s, kpad, kprev, vpad, att_s, pp_s, inv_s):
    t = pl.program_id(1)
    tm = x_ref.shape[1]
    w = CHUNK
    x = x_ref[0]
    qb, kb, v = _odd_front(x, g_ref, wqkv_ref, bqkv_ref, cos_ref, lo_ref, hi_ref)
    for j, blk in enumerate(qb):
        q_s[:, j * LANES:(j + 1) * LANES] = blk.astype(BF16)

    @pl.when(t == 0)
    def _():
        kprev[...] = jnp.zeros((2 * C_KV_HEADS, LANES, w), BF16)
        vpad[:, 0:w, :] = jnp.zeros((2 * C_KV_HEADS, w, LANES), BF16)

    kt = [blk.T for blk in kb]
    k_tails = []
    for kh in range(C_KV_HEADS):
        j, r = divmod(kh, LANES // C_HEAD_DIM)
        head = kt[j][r * C_HEAD_DIM:(r + 1) * C_HEAD_DIM, :]
        zeros = jnp.zeros_like(head)
        ke_t = jnp.concatenate([head, zeros], axis=0).astype(BF16)
        ko_t = jnp.concatenate([zeros, head], axis=0).astype(BF16)
        kpad[2 * kh] = ke_t
        kpad[2 * kh + 1] = ko_t
        k_tails += [ke_t[:, tm - w:], ko_t[:, tm - w:]]
        ve, vo = _pad_even_odd(v[:, j * LANES:(j + 1) * LANES], r)
        vpad[2 * kh, w:, :] = ve
        vpad[2 * kh + 1, w:, :] = vo

    @pl.when(t == pl.num_programs(1) - 1)
    def _():
        for j, blk in enumerate(kb):
            wk_ref[0, :, j * LANES:(j + 1) * LANES] = blk[tm - w:, :]
        wv_ref[0] = v[tm - w:, :]

    qi = lax.broadcasted_iota(jnp.int32, (2 * w, w), 0) % w
    kj = lax.broadcasted_iota(jnp.int32, (2 * w, w), 1)
    upper = kj > qi
    first_half = lax.broadcasted_iota(jnp.int32, (2 * w, 1), 0) < w

    zero = jnp.zeros((), F32)
    for c in range(tm // w):
        rows = pl.ds(c * w, w)
        kkeys = pl.ds((c - 1) * w, 2 * w)
        prev_bias = jnp.where(t == 0, NEG, 0.0) if c == 0 else None
        for kh in range(C_KV_HEADS):
            if c == 0:
                ke = jnp.concatenate([kprev[2 * kh], kpad[2 * kh, :, 0:w]], axis=1)
                ko = jnp.concatenate([kprev[2 * kh + 1], kpad[2 * kh + 1, :, 0:w]], axis=1)
            else:
                ke, ko = kpad[2 * kh, :, kkeys], kpad[2 * kh + 1, :, kkeys]
            c0 = (2 * kh) * LANES
            q2 = jnp.concatenate([q_s[rows, c0:c0 + LANES], q_s[rows, c0 + LANES:c0 + 2 * LANES]], 0)
            for half, (kk, sink) in enumerate(zip((ke, ko), _pair_sinks(sink_ref, kh, first_half))):
                s2 = _dot(q2, kk)
                s_prev = s2[:, :w] if prev_bias is None else s2[:, :w] + prev_bias
                p, inv = _sink_softmax(jnp.where(upper, s_prev, s2[:, w:]), sink)
                u = (c * C_KV_HEADS + kh) * 2 + half
                pp_s[u] = jnp.concatenate([jnp.where(upper, p, zero), jnp.where(upper, zero, p)],
                                          axis=1).astype(BF16)
                inv_s[u] = inv
    for c in range(tm // w):
        rows = pl.ds(c * w, w)
        keys = pl.ds(c * w, 2 * w)
        for kh in range(C_KV_HEADS):
            c0 = (2 * kh) * LANES
            u = (c * C_KV_HEADS + kh) * 2
            o = _pair_combine(_dot(pp_s[u], vpad[2 * kh, keys, :]), _dot(pp_s[u + 1], vpad[2 * kh + 1, keys, :]),
                              inv_s[u], inv_s[u + 1])
            att_s[rows, c0:c0 + LANES] = o[:w].astype(BF16)
            att_s[rows, c0 + LANES:c0 + 2 * LANES] = o[w:].astype(BF16)
    xo_ref[0] = x + _dot(att_s[...], wout_ref[...]) + bout_ref[...]
    for i, tail in enumerate(k_tails):
        kprev[i] = tail
    vpad[:, 0:w, :] = vpad[:, tm:tm + w, :]


def _odd_prompt(x, j, sinks, g, w_qkv, b_qkv, w_out, b_out, cos, lo, hi):
    b, l, d = x.shape
    tm = TM_PROMPT_WIDE
    kvw = C_KV_HEADS * C_HEAD_DIM
    win_spec = pl.BlockSpec((1, CHUNK, kvw), lambda i, t: (i, 0, 0))
    tab = pl.BlockSpec((tm, LANES), lambda i, t: (t, 0))
    return pl.pallas_call(
        _odd_prompt_kernel,
        grid=(b, l // tm),
        in_specs=[pl.BlockSpec(memory_space=pltpu.SMEM),
                  pl.BlockSpec((1, tm, d), lambda i, t: (i, t, 0)),
                  _const((1, d)), _layer(j, (d, ODD_IN)), _const((1, ODD_IN)),
                  _layer(j, (d, d)), _const((1, d)), tab, tab, tab],
        out_specs=[pl.BlockSpec((1, tm, d), lambda i, t: (i, t, 0)), win_spec, win_spec],
        out_shape=[jax.ShapeDtypeStruct((b, l, d), F32),
                   jax.ShapeDtypeStruct((b, CHUNK, kvw), F32),
                   jax.ShapeDtypeStruct((b, CHUNK, kvw), F32)],
        scratch_shapes=[pltpu.VMEM((tm, d), BF16),
                        pltpu.VMEM((2 * C_KV_HEADS, LANES, tm), BF16),
                        pltpu.VMEM((2 * C_KV_HEADS, LANES, CHUNK), BF16),
                        pltpu.VMEM((2 * C_KV_HEADS, tm + CHUNK, LANES), BF16),
                        pltpu.VMEM((tm, d), BF16),
                        pltpu.VMEM((tm // CHUNK * C_KV_HEADS * 2, 2 * CHUNK, 2 * CHUNK), BF16),
                        pltpu.VMEM((tm // CHUNK * C_KV_HEADS * 2, 2 * CHUNK, LANES), F32)],
        compiler_params=_params(2),
        name="odd_prompt",
    )(sinks, x, g, w_qkv, b_qkv, w_out, b_out, cos, lo, hi)


def _odd_sample_kernel(sink_ref, x_ref, ck_ref, cv_ref, g_ref, wqkv_ref, bqkv_ref, wout_ref, bout_ref,
                       cos_ref, lo_ref, hi_ref, xo_ref, wk_ref, wv_ref, q_s, k_s, v_s, att_s, p_s, inv_s,
                       *, ls):
    i = pl.program_id(0)
    bt, w, kvw = ck_ref.shape
    tt = bt * ls

    @pl.when(i == 0)
    def _():
        qb, kb, v = _odd_front(x_ref[...], g_ref, wqkv_ref, bqkv_ref, cos_ref, lo_ref, hi_ref)
        for j, blk in enumerate(qb):
            q_s[:, j * LANES:(j + 1) * LANES] = blk.astype(BF16)
        for j, blk in enumerate(kb):
            k_s[:, j * LANES:(j + 1) * LANES] = blk
        v_s[...] = v

    rows = pl.ds(pl.multiple_of(i * tt, tt), tt)
    knew = k_s[rows, :]
    vnew = v_s[rows, :]
    ck = ck_ref[...]
    cv = cv_ref[...]
    for bi in range(bt):
        wk_ref[bi, 0:w - ls, :] = ck[bi, ls:, :]
        wk_ref[bi, w - ls:, :] = knew[bi * ls:(bi + 1) * ls, :]
        wv_ref[bi, 0:w - ls, :] = cv[bi, ls:, :]
        wv_ref[bi, w - ls:, :] = vnew[bi * ls:(bi + 1) * ls, :]
    kall = jnp.concatenate([ck.reshape(bt * w, kvw), knew], axis=0)
    vall = jnp.concatenate([cv.reshape(bt * w, kvw), vnew], axis=0)
    ns = bt * w + tt

    tok = lax.broadcasted_iota(jnp.int32, (2 * tt, ns), 0) % tt
    key = lax.broadcasted_iota(jnp.int32, (2 * tt, ns), 1)
    tb, tl = tok // ls, tok % ls
    cached = key < bt * w
    nkey = key - bt * w
    valid = ((cached & ((key // w) == tb) & ((key % w) > tl))
             | ((key >= bt * w) & ((nkey // ls) == tb) & ((nkey % ls) <= tl)))
    first_half = lax.broadcasted_iota(jnp.int32, (2 * tt, 1), 0) < tt

    for kh in range(C_KV_HEADS):
        j, r = divmod(kh, LANES // C_HEAD_DIM)
        c0 = (2 * kh) * LANES
        q2 = jnp.concatenate([q_s[rows, c0:c0 + LANES], q_s[rows, c0 + LANES:c0 + 2 * LANES]], axis=0)
        keys = _pad_even_odd(kall[:, j * LANES:(j + 1) * LANES], r)
        for half, (kk, sink) in enumerate(zip(keys, _pair_sinks(sink_ref, kh, first_half))):
            p, inv = _sink_softmax(jnp.where(valid, _dot_nt(q2, kk), NEG), sink)
            p_s[2 * kh + half] = p.astype(BF16)
            inv_s[2 * kh + half] = inv
    for kh in range(C_KV_HEADS):
        j, r = divmod(kh, LANES // C_HEAD_DIM)
        c0 = (2 * kh) * LANES
        ve, vo = _pad_even_odd(vall[:, j * LANES:(j + 1) * LANES], r)
        o = _pair_combine(_dot(p_s[2 * kh], ve), _dot(p_s[2 * kh + 1], vo), inv_s[2 * kh], inv_s[2 * kh + 1])
        att_s[rows, c0:c0 + LANES] = o[:tt].astype(BF16)
        att_s[rows, c0 + LANES:c0 + 2 * LANES] = o[tt:].astype(BF16)

    @pl.when(i == pl.num_programs(0) - 1)
    def _():
        xo_ref[...] = x_ref[...] + _dot(att_s[...], wout_ref[...]) + bout_ref[...]


def _odd_sample(x, ck, cv, j, sinks, g, w_qkv, b_qkv, w_out, b_out, cos, lo, hi):
    n, d = x.shape
    nb, w, kvw = ck.shape
    bt = BT_SAMPLE
    tt = bt * (n // nb)
    cache_spec = pl.BlockSpec((bt, w, kvw), lambda i: (i, 0, 0))
    return pl.pallas_call(
        functools.partial(_odd_sample_kernel, ls=n // nb),
        grid=(nb // bt,),
        in_specs=[pl.BlockSpec(memory_space=pltpu.SMEM),
                  _const((n, d)), cache_spec, cache_spec,
                  _const((1, d)), _layer(j, (d, ODD_IN)), _const((1, ODD_IN)),
                  _layer(j, (d, d)), _const((1, d)),
                  _const((n, LANES)), _const((n, LANES)), _const((n, LANES))],
        out_specs=[pl.BlockSpec((n, d), lambda i: (0, 0)), cache_spec, cache_spec],
        out_shape=[jax.ShapeDtypeStruct((n, d), F32),
                   jax.ShapeDtypeStruct(ck.shape, F32), jax.ShapeDtypeStruct(cv.shape, F32)],
        scratch_shapes=[pltpu.VMEM((n, d), BF16), pltpu.VMEM((n, kvw), F32), pltpu.VMEM((n, kvw), F32),
                        pltpu.VMEM((n, d), BF16),
                        pltpu.VMEM((2 * C_KV_HEADS, 2 * tt, bt * w + tt), BF16),
                        pltpu.VMEM((2 * C_KV_HEADS, 2 * tt, LANES), F32)],
        compiler_params=_params(1),
        name="odd_sample",
    )(sinks, x, ck, cv, g, w_qkv, b_qkv, w_out, b_out, cos, lo, hi)


def _cross_heads(q, mk, mv):
    outs = []
    for hd in range(MEM_HEADS):
        cols = slice(hd * MEM_HEAD_DIM, (hd + 1) * MEM_HEAD_DIM)
        s = _dot_nt(q[:, cols], mk[:, cols])
        p = jnp.exp(s - jnp.max(s, axis=-1, keepdims=True))
        den = jnp.sum(p, axis=-1, keepdims=True)
        outs.append(_dot(p.astype(BF16), mv[:, cols]) * (1.0 / den))
    return outs


def _mlp(x1, gf_ref, wup_ref, wdown_ref):
    h2 = _rms(x1, gf_ref[...]).astype(BF16)
    acc = x1
    for j in range(D_FF // D_MODEL):
        sl = slice(j * D_MODEL, (j + 1) * D_MODEL)
        a = jnp.square(jnp.maximum(_dot(h2, wup_ref[:, sl]), 0.0)).astype(BF16)
        acc = acc + _dot(a, wdown_ref[sl, :])
    return acc


def _cross_mlp_prompt_kernel(x_ref, mk_ref, mv_ref, gc_ref, wmq_ref, wmo_ref, gf_ref, wup_ref,
                             wdown_ref, gfin_ref, xo_ref, o_s, *, final):
    x = x_ref[0]
    h = _rms(x, gc_ref[...]).astype(BF16)
    q = (_dot(h, wmq_ref[...]) * (MEM_HEAD_DIM ** -0.5)).astype(BF16)
    outs = _cross_heads(q, mk_ref[0, 0], mv_ref[0, 0])
    for hd, o in enumerate(outs):
        o_s[:, hd * MEM_HEAD_DIM:(hd + 1) * MEM_HEAD_DIM] = o.astype(BF16)
    x1 = x + _dot(o_s[...], wmo_ref[...])
    y = _mlp(x1, gf_ref, wup_ref, wdown_ref)
    if final:
        y = _rms(y, gfin_ref[...])
    xo_ref[0] = y


def _cross_mlp_prompt(x, mk, mv, layer, g_cross, w_mq, w_mo, g_ffn, w_up, w_down, g_final, final):
    b, l, d = x.shape
    tm = TM_PROMPT_WIDE
    mem_spec = pl.BlockSpec((1, 1, N_MEM, d), lambda i, t: (layer, i, 0, 0))
    return pl.pallas_call(
        functools.partial(_cross_mlp_prompt_kernel, final=final),
        grid=(b, l // tm),
        in_specs=[pl.BlockSpec((1, tm, d), lambda i, t: (i, t, 0)), mem_spec, mem_spec,
                  _const((1, d)), _layer(layer, (d, d)), _layer(layer, (d, d)), _const((1, d)),
                  _const((d, D_FF)), _const((D_FF, d)), _const((1, d))],
        out_specs=pl.BlockSpec((1, tm, d), lambda i, t: (i, t, 0)),
        out_shape=jax.ShapeDtypeStruct((b, l, d), F32),
        scratch_shapes=[pltpu.VMEM((tm, d), BF16)],
        compiler_params=_params(2),
        name=f"cross_mlp_prompt_{layer}",
    )(x, mk, mv, g_cross, w_mq, w_mo, g_ffn, w_up, w_down, g_final)


def _cross_sample_kernel(x_ref, mk_ref, mv_ref, gc_ref, wmq_ref, wmo_ref, xo_ref, q_s, o_s, p_s, inv_s):
    tt = x_ref.shape[0]
    bt = mk_ref.shape[1]
    ls = tt // bt
    nrow = N_MEM * MEM_HEADS
    x = x_ref[...]
    h = _rms(x, gc_ref[...]).astype(BF16)
    q_s[...] = _dot(h, wmq_ref[...]) * (MEM_HEAD_DIM ** -0.5)
    qhead = lax.broadcasted_iota(jnp.int32, (MEM_HEADS * ls, nrow), 0) // ls
    khead = lax.broadcasted_iota(jnp.int32, (MEM_HEADS * ls, nrow), 1) % MEM_HEADS
    own = qhead == khead

    for bi in range(bt):
        qb = q_s[bi * ls:(bi + 1) * ls, :]
        q4 = jnp.concatenate([qb[:, hd * MEM_HEAD_DIM:(hd + 1) * MEM_HEAD_DIM]
                              for hd in range(MEM_HEADS)], axis=0).astype(BF16)
        k2 = mk_ref[0, bi].reshape(nrow, MEM_HEAD_DIM).astype(BF16)
        s = jnp.where(own, _dot_nt(q4, k2), NEG)
        p = jnp.exp(s - jnp.max(s, axis=-1, keepdims=True))
        p_s[bi] = p.astype(BF16)
        inv_s[bi] = jnp.broadcast_to(1.0 / jnp.sum(p, axis=-1, keepdims=True), (MEM_HEADS * ls, LANES))
    for bi in range(bt):
        v2 = mv_ref[0, bi].reshape(nrow, MEM_HEAD_DIM).astype(BF16)
        inv = inv_s[bi]
        o4 = _dot(p_s[bi], v2) * jnp.concatenate([inv] * (MEM_HEAD_DIM // LANES), axis=1)
        for hd in range(MEM_HEADS):
            o_s[bi * ls:(bi + 1) * ls, hd * MEM_HEAD_DIM:(hd + 1) * MEM_HEAD_DIM] = o4[hd * ls:(hd + 1) * ls]
    xo_ref[...] = x + _dot(o_s[...].astype(BF16), wmo_ref[...])


def _cross_sample(x, mk, mv, layer, g_cross, w_mq, w_mo):
    n, d = x.shape
    nb = mk.shape[1]
    ls = n // nb
    bt = BT_SAMPLE
    tt = bt * ls
    mem_spec = pl.BlockSpec((1, bt, N_MEM, MEM_HEADS, MEM_HEAD_DIM), lambda i: (layer, i, 0, 0, 0))
    return pl.pallas_call(
        _cross_sample_kernel,
        grid=(nb // bt,),
        in_specs=[pl.BlockSpec((tt, d), lambda i: (i, 0)), mem_spec, mem_spec,
                  _const((1, d)), _layer(layer, (d, d)), _layer(layer, (d, d))],
        out_specs=pl.BlockSpec((tt, d), lambda i: (i, 0)),
        out_shape=jax.ShapeDtypeStruct((n, d), F32),
        scratch_shapes=[pltpu.VMEM((tt, d), F32), pltpu.VMEM((tt, d), F32),
                        pltpu.VMEM((bt, MEM_HEADS * ls, N_MEM * MEM_HEADS), BF16),
                        pltpu.VMEM((bt, MEM_HEADS * ls, LANES), F32)],
        compiler_params=_params(1),
        name=f"cross_sample_{layer}",
    )(x, mk, mv, g_cross, w_mq, w_mo)


def _mlp_stream_kernel(x_ref, gf_ref, wup_ref, wdown_ref, gfin_ref, xo_ref, wup_b_ref, wdown_b_ref,
                       h_s, acc_s, *, final):
    j = pl.program_id(0)

    @pl.when(j == 0)
    def _():
        x = x_ref[...]
        h_s[...] = _rms(x, gf_ref[...]).astype(BF16)
        acc_s[...] = x

    wu = wup_ref[...].astype(BF16)
    wd = wdown_ref[...].astype(BF16)
    wup_b_ref[...] = wu
    wdown_b_ref[...] = wd
    a = jnp.square(jnp.maximum(_dot(h_s[...], wu), 0.0)).astype(BF16)
    acc_s[...] += _dot(a, wd)

    @pl.when(j == pl.num_programs(0) - 1)
    def _():
        y = acc_s[...]
        if final:
            y = _rms(y, gfin_ref[...])
        xo_ref[...] = y


def _mlp_sample(x, layer, g_ffn, w_up, w_down, g_final, final):
    n, d = x.shape
    fc = FF_CHUNK
    return pl.pallas_call(
        functools.partial(_mlp_stream_kernel, final=final),
        grid=(D_FF // fc,),
        in_specs=[_const((n, d)), _const((1, d)),
                  pl.BlockSpec((None, d, fc), lambda c: (layer, 0, c)),
                  pl.BlockSpec((None, fc, d), lambda c: (layer, c, 0)),
                  _const((1, d))],
        out_specs=[pl.BlockSpec((n, d), lambda c: (0, 0)),
                   pl.BlockSpec((d, fc), lambda c: (0, c)),
                   pl.BlockSpec((fc, d), lambda c: (c, 0))],
        out_shape=[jax.ShapeDtypeStruct((n, d), F32),
                   jax.ShapeDtypeStruct((d, D_FF), BF16),
                   jax.ShapeDtypeStruct((D_FF, d), BF16)],
        scratch_shapes=[pltpu.VMEM((n, d), BF16), pltpu.VMEM((n, d), F32)],
        compiler_params=_params(1),
        name=f"mlp_sample_{layer}",
    )(x, g_ffn, w_up, w_down, g_final)


def kernel(x_prompt, x_sample, state_ret, cache_win_k, cache_win_v, cache_mem_k, cache_mem_v, mem_prompt, g_mix, w_in_e, sgu_ln_g, sgu_ln_b, w_spatial, b_spatial, w_out_e, w_qkv_o, b_qkv_o, sinks, w_out_o, b_out_o, g_cross, g_mem, w_mq, w_mk, w_mv, w_mo, g_ffn, w_up, w_down, g_final):
    bp, lp, d = x_prompt.shape
    bs, ls, _ = x_sample.shape
    assert TM_PROMPT % RET_CHUNK_PROMPT == 0 and RET_CHUNK_PROMPT % CHUNK == 0
    assert d == D_MODEL and lp % TM_PROMPT == 0 and lp % TM_PROMPT_WIDE == 0 and bs % BT_SAMPLE == 0 and bs % BT_EVEN_SAMPLE == 0
    assert ls < CHUNK and CHUNK % ls == 0 and D_FF % FF_CHUNK == 0 and bp % MEMKV_BT == 0
    ns = bs * ls
    kvw = C_KV_HEADS * C_HEAD_DIM

    def row(v):
        return v.reshape(1, -1).astype(F32)

    def bf(w):
        return w.astype(BF16)

    pos_p = np.arange(lp)
    ret_tab_p = _ret_rope_tables(pos_p)
    tt_e = BT_EVEN_SAMPLE * ls
    ret_tab_s = _ret_rope_tables(PAST_LEN + (np.arange(tt_e) % ls))
    win_tab_p = _win_rope_tables(pos_p)
    win_tab_s = _win_rope_tables(PAST_LEN + (np.arange(ns) % ls))

    w_in_b, w_out_e_b, w_qkv_b, w_out_o_b = bf(w_in_e), bf(w_out_e), bf(w_qkv_o), bf(w_out_o)
    w_mq_b, w_mo_b = bf(w_mq), bf(w_mo)

    mk_p, mv_p, mk_pb, mv_pb = _memkv(mem_prompt, g_mem, w_mk, w_mv)
    mk_s, mv_s = cache_mem_k, cache_mem_v

    xp = x_prompt
    xs = x_sample.reshape(ns, d)
    final_g = row(g_final)
    outs = {}
    for layer in range(DEPTH):
        j = layer // 2
        g = row(g_mix[layer])
        if layer % 2 == 0:
            ln_g, ln_b = row(sgu_ln_g[j]), row(sgu_ln_b[j])
            tril = jnp.tril(jnp.ones((CHUNK, CHUNK), bool))
            w_s = bf(jnp.where(tril[None], w_spatial[j], 0.0))
            b_s = b_spatial[j].reshape(SGU_GROUPS, CHUNK, 1)
            xp, ret_p = _even_prompt(xp, j, g, w_in_b, ln_g, ln_b, w_s, b_s, w_out_e_b, *ret_tab_p)
            w_small = w_s[:, :ls, :ls]
            eye = jnp.eye(BT_EVEN_SAMPLE, dtype=BF16)
            w_bd = jnp.einsum("ab,gts->gatbs", eye, w_small).reshape(SGU_GROUPS, tt_e, tt_e)
            b_bd = jnp.tile(b_spatial[j][:, :ls], (1, BT_EVEN_SAMPLE)).reshape(SGU_GROUPS, tt_e, 1)
            xs, ret_s, v_rows = _even_sample(xs, state_ret[j], j, g, w_in_b, ln_g, ln_b, w_bd, b_bd,
                                             w_out_e_b, *ret_tab_s)
            outs["ret_p"], outs["ret_s"], outs["v_rows"] = ret_p, ret_s, v_rows
        else:
            b_qkv, b_out = row(b_qkv_o[j]), row(b_out_o[j])
            xp, wk_p, wv_p = _odd_prompt(xp, j, sinks[j], g, w_qkv_b, b_qkv, w_out_o_b, b_out,
                                         *win_tab_p)
            xs, wk_s, wv_s = _odd_sample(xs, cache_win_k[j].reshape(bs, CHUNK, kvw),
                                         cache_win_v[j].reshape(bs, CHUNK, kvw), j, sinks[j],
                                         g, w_qkv_b, b_qkv, w_out_o_b, b_out, *win_tab_s)
            outs["wk_p"], outs["wv_p"], outs["wk_s"], outs["wv_s"] = wk_p, wv_p, wk_s, wv_s
        final = layer == DEPTH - 1
        gc, gf = row(g_cross[layer]), row(g_ffn[layer])
        xs = _cross_sample(xs, mk_s, mv_s, layer, gc, w_mq_b, w_mo_b)
        xs, w_up_l, w_down_l = _mlp_sample(xs, layer, gf, w_up, w_down, final_g, final)
        xp = _cross_mlp_prompt(xp, mk_pb, mv_pb, layer, gc, w_mq_b, w_mo_b, gf, w_up_l, w_down_l,
                               final_g, final)

    n_even, n_odd = (DEPTH + 1) // 2, DEPTH // 2
    return (xp,
            xs.reshape(bs, ls, d),
            mk_p,
            mv_p,
            outs["ret_p"].reshape(n_even, bp, RET_HEADS, RET_DK, RET_DV),
            outs["ret_s"].reshape(n_even, bs, RET_HEADS, RET_DK, RET_DV),
            outs["v_rows"].reshape(n_even, bs, ls, SGU_GROUPS, SGU_GROUP_DIM),
            outs["wk_p"].reshape(n_odd, bp, CHUNK, C_KV_HEADS, C_HEAD_DIM),
            outs["wv_p"].reshape(n_odd, bp, CHUNK, C_KV_HEADS, C_HEAD_DIM),
            outs["wk_s"].reshape(n_odd, bs, CHUNK, C_KV_HEADS, C_HEAD_DIM),
            outs["wv_s"].reshape(n_odd, bs, CHUNK, C_KV_HEADS, C_HEAD_DIM))
```

```python
import functools
import math

import numpy as np
import jax
import jax.numpy as jnp
from jax import lax
from jax.experimental import pallas as pl
from jax.experimental.pallas import tpu as pltpu

F32 = jnp.float32
BF16 = jnp.bfloat16

D_MODEL = 1024
DEPTH = 2
PAST_LEN = 16384
EPS = 1e-6

SGU_GROUPS = 4
SGU_GROUP_DIM = 256
RET_HEADS = 4
RET_DK = 128
RET_DV = 256
RET_ROPE_BASE = 10000.0
CHUNK = 128
RET_CHUNK_PROMPT = 256
C_HEADS = 16
C_KV_HEADS = 4
C_HEAD_DIM = 64
C_ROPE_BASE = 150000.0
N_MEM = 256
MEM_HEADS = 4
MEM_HEAD_DIM = 256
D_FF = 4096
EVEN_IN = 5120
EVEN_OUT = 2048
ODD_IN = 1536

LANES = 128
VMEM_LIMIT = 56 * 2 ** 20
NEG = -1e30
LOG2E = math.log2(math.e)

TM_PROMPT = 1024
BT_SAMPLE = 8
BT_EVEN_SAMPLE = 16
FF_CHUNK = 512
MEMKV_BT = 2


def _params(n_axes):
    return pltpu.CompilerParams(dimension_semantics=("arbitrary",) * n_axes,
                                vmem_limit_bytes=VMEM_LIMIT)


def _const(shape):
    nd = len(shape)
    return pl.BlockSpec(shape, lambda *_: (0,) * nd, pipeline_mode=pl.Buffered(1))


def _layer(layer, shape):
    nd = len(shape)
    return pl.BlockSpec((None,) + tuple(shape), lambda *_: (layer,) + (0,) * nd,
                        pipeline_mode=pl.Buffered(1))


def _rms(x, g):
    return x * lax.rsqrt(jnp.mean(x * x, axis=-1, keepdims=True) + EPS) * g


def _dot(a, b):
    return jnp.dot(a, b, preferred_element_type=F32)


def _dot_nt(a, b):
    return lax.dot_general(a, b, (((1,), (1,)), ((), ())), preferred_element_type=F32)


def _dot_tn(a, b):
    return lax.dot_general(a, b, (((0,), (0,)), ((), ())), preferred_element_type=F32)


def _ret_rope_tables(pos):
    half = RET_DK // 2
    inv = RET_ROPE_BASE ** (-np.arange(half, dtype=np.float64) / half)
    ang = pos.astype(np.float64)[:, None] * inv[None, :]
    cos, sin = np.cos(ang), np.sin(ang)
    return (jnp.asarray(np.concatenate([cos, cos], -1), F32),
            jnp.asarray(np.concatenate([-sin, sin], -1), F32))


def _win_rope_tables(pos):
    half = C_HEAD_DIM // 2
    inv = C_ROPE_BASE ** (-np.arange(half, dtype=np.float64) / half)
    ang = pos.astype(np.float64)[:, None] * inv[None, :]
    lane = np.arange(LANES)
    cos = np.cos(ang)[:, lane % half]
    sin = np.sin(ang)[:, lane % half]
    low = (lane % C_HEAD_DIM) < half
    return (jnp.asarray(cos, F32),
            jnp.asarray(np.where(low[None], -sin, 0.0), F32),
            jnp.asarray(np.where(low[None], 0.0, sin), F32))


def _ret_rotate(x, cosf, sinf):
    return x * cosf + pltpu.roll(x, RET_DK // 2, 1) * sinf


def _win_rotate(x, cos, sin_lo, sin_hi):
    return (x * cos + pltpu.roll(x, LANES - C_HEAD_DIM // 2, 1) * sin_lo
            + pltpu.roll(x, C_HEAD_DIM // 2, 1) * sin_hi)


def _log_gamma(h):
    return math.log1p(-2.0 ** (-5.0 - h))


def _memkv_kernel(mem_ref, g_ref, wk_ref, wv_ref, k_ref, v_ref, kb_ref, vb_ref, wk_s, wv_s):
    @pl.when(pl.program_id(1) == 0)
    def _():
        wk_s[...] = wk_ref[0].astype(BF16)
        wv_s[...] = wv_ref[0].astype(BF16)

    nb = mem_ref.shape[0]
    m = _rms(mem_ref[...].reshape(nb * N_MEM, D_MODEL), g_ref[0]).astype(BF16)
    for w_s, out_ref, bf_ref in ((wk_s, k_ref, kb_ref), (wv_s, v_ref, vb_ref)):
        kv = _dot(m, w_s[...])
        bf_ref[0] = kv.astype(BF16).reshape(nb, N_MEM, D_MODEL)
        for bi in range(nb):
            for hd in range(MEM_HEADS):
                out_ref[0, bi, :, hd, :] = kv[bi * N_MEM:(bi + 1) * N_MEM,
                                              hd * MEM_HEAD_DIM:(hd + 1) * MEM_HEAD_DIM]


def _memkv(mem, g_mem, w_mk, w_mv):
    b = mem.shape[0]
    nb = MEMKV_BT
    out = jax.ShapeDtypeStruct((DEPTH, b, N_MEM, MEM_HEADS, MEM_HEAD_DIM), F32)
    out_b = jax.ShapeDtypeStruct((DEPTH, b, N_MEM, D_MODEL), BF16)
    spec5 = pl.BlockSpec((1, nb, N_MEM, MEM_HEADS, MEM_HEAD_DIM), lambda l, i: (l, i, 0, 0, 0))
    spec_b = pl.BlockSpec((1, nb, N_MEM, D_MODEL), lambda l, i: (l, i, 0, 0))
    return pl.pallas_call(
        _memkv_kernel,
        grid=(DEPTH, b // nb),
        in_specs=[pl.BlockSpec((nb, N_MEM, D_MODEL), lambda l, i: (i, 0, 0)),
                  pl.BlockSpec((1, 1, D_MODEL), lambda l, i: (l, 0, 0)),
                  pl.BlockSpec((1, D_MODEL, D_MODEL), lambda l, i: (l, 0, 0)),
                  pl.BlockSpec((1, D_MODEL, D_MODEL), lambda l, i: (l, 0, 0))],
        out_specs=[spec5, spec5, spec_b, spec_b],
        out_shape=[out, out, out_b, out_b],
        scratch_shapes=[pltpu.VMEM((D_MODEL, D_MODEL), BF16), pltpu.VMEM((D_MODEL, D_MODEL), BF16)],
        compiler_params=_params(2),
        name="memkv",
    )(mem, g_mem.reshape(DEPTH, 1, D_MODEL), w_mk, w_mv)


def _even_front(x, g_ref, win_ref, lng_ref, lnb_ref, cos_ref, sin_ref):
    h = _rms(x, g_ref[...]).astype(BF16)

    def proj(lo, hi):
        return _dot(h, win_ref[:, lo:hi])

    u = jax.nn.gelu(proj(0, 1024))
    v = jax.nn.gelu(proj(1024, 2048))
    vc = v - jnp.mean(v, axis=-1, keepdims=True)
    vn = vc * lax.rsqrt(jnp.mean(vc * vc, axis=-1, keepdims=True) + EPS) * lng_ref[...] + lnb_ref[...]
    q = proj(2048, 2560)
    k = proj(2560, 3072)
    vr = proj(3072, 4096)
    gate = proj(4096, 5120)
    cosf, sinf = cos_ref[...], sin_ref[...]
    qs, ks = [], []
    for hd in range(RET_HEADS):
        cols = slice(hd * RET_DK, (hd + 1) * RET_DK)
        qs.append(_ret_rotate(q[:, cols], cosf, sinf))
        ks.append(_ret_rotate(k[:, cols], cosf, sinf) * (RET_DK ** -0.5))
    return u, vn, qs, ks, vr, gate


def _head_norm_gate(o, gate):
    on = o * lax.rsqrt(jnp.mean(o * o, axis=-1, keepdims=True) + EPS)
    return on * jax.nn.silu(gate)


def _even_prompt_kernel(x_ref, g_ref, win_ref, lng_ref, lnb_ref, ws_ref, bs_ref, wout_ref,
                        cos_ref, sin_ref, xo_ref, sout_ref,
                        state, u_s, vn_s, q_s, k_s, vr_s, gate_s, cat_s, oi_s):
    t = pl.program_id(1)
    tm = x_ref.shape[1]

    @pl.when(t == 0)
    def _():
        state[...] = jnp.zeros_like(state)

    x = x_ref[0]
    u, vn, qs, ks, vr, gate = _even_front(x, g_ref, win_ref, lng_ref, lnb_ref, cos_ref, sin_ref)
    u_s[...] = u
    vn_s[...] = vn.astype(BF16)
    for hd in range(RET_HEADS):
        cols = slice(hd * RET_DK, (hd + 1) * RET_DK)
        q_s[:, cols] = qs[hd].astype(BF16)
        k_s[cols, :] = ks[hd].T
    vr_s[...] = vr.astype(BF16)
    gate_s[...] = gate

    rc = RET_CHUNK_PROMPT
    row = lax.broadcasted_iota(jnp.int32, (rc, rc), 0)
    col = lax.broadcasted_iota(jnp.int32, (rc, rc), 1)
    diff = (row - col).astype(F32)
    ridx = lax.broadcasted_iota(jnp.int32, (rc, 1), 0).astype(F32)
    cidx = lax.broadcasted_iota(jnp.int32, (1, rc), 1).astype(F32)
    decay, xi, zeta, cdec = [], [], [], []
    for hd in range(RET_HEADS):
        lg = _log_gamma(hd)
        decay.append(jnp.where(diff >= 0, jnp.exp(jnp.maximum(diff, 0.0) * lg), 0.0))
        xi.append(jnp.exp((ridx + 1.0) * lg))
        zeta.append(jnp.exp((rc - 1.0 - cidx) * lg))
        cdec.append(math.exp(rc * lg))

    for c in range(tm // CHUNK):
        rows = pl.ds(c * CHUNK, CHUNK)
        for g in range(SGU_GROUPS):
            cols = slice(g * SGU_GROUP_DIM, (g + 1) * SGU_GROUP_DIM)
            mixed = _dot(ws_ref[g], vn_s[rows, cols]) + bs_ref[g]
            cat_s[rows, cols] = (u_s[rows, cols] * mixed).astype(BF16)
    for c in range(tm // rc):
        rows = pl.ds(c * rc, rc)
        for hd in range(RET_HEADS):
            kcols = slice(hd * RET_DK, (hd + 1) * RET_DK)
            vcols = slice(hd * RET_DV, (hd + 1) * RET_DV)
            sc = _dot(q_s[rows, kcols], k_s[kcols, rows].astype(BF16)) * decay[hd]
            oi_s[rows, vcols] = _dot(sc.astype(BF16), vr_s[rows, vcols])
    for c in range(tm // rc):
        rows = pl.ds(c * rc, rc)
        for hd in range(RET_HEADS):
            kcols = slice(hd * RET_DK, (hd + 1) * RET_DK)
            vcols = slice(hd * RET_DV, (hd + 1) * RET_DV)
            qc = q_s[rows, kcols]
            kt = k_s[kcols, rows]
            vc = vr_s[rows, vcols]
            s_prev = state[hd]
            o = oi_s[rows, vcols] + _dot(qc, s_prev.astype(BF16)) * xi[hd]
            state[hd] = s_prev * cdec[hd] + _dot((kt * zeta[hd]).astype(BF16), vc)
            ocols = slice(D_MODEL + hd * RET_DV, D_MODEL + (hd + 1) * RET_DV)
            cat_s[rows, ocols] = _head_norm_gate(o, gate_s[rows, vcols]).astype(BF16)
    xo_ref[0] = x + _dot(cat_s[...], wout_ref[...])

    @pl.when(t == pl.num_programs(1) - 1)
    def _():
        sout_ref[0] = state[...]


def _even_prompt(x, j, g, w_in, ln_g, ln_b, w_s, b_s, w_out, cosf, sinf):
    b, l, d = x.shape
    tm = TM_PROMPT
    return pl.pallas_call(
        _even_prompt_kernel,
        grid=(b, l // tm),
        in_specs=[pl.BlockSpec((1, tm, d), lambda i, t: (i, t, 0)),
                  _const((1, d)), _layer(j, (d, EVEN_IN)), _const((1, d)), _const((1, d)),
                  _const((SGU_GROUPS, CHUNK, CHUNK)), _const((SGU_GROUPS, CHUNK, 1)),
                  _layer(j, (EVEN_OUT, d)),
                  pl.BlockSpec((tm, RET_DK), lambda i, t: (t, 0)),
                  pl.BlockSpec((tm, RET_DK), lambda i, t: (t, 0))],
        out_specs=[pl.BlockSpec((1, tm, d), lambda i, t: (i, t, 0)),
                   pl.BlockSpec((1, RET_HEADS, RET_DK, RET_DV), lambda i, t: (i, 0, 0, 0))],
        out_shape=[jax.ShapeDtypeStruct((b, l, d), F32),
                   jax.ShapeDtypeStruct((b, RET_HEADS, RET_DK, RET_DV), F32)],
        scratch_shapes=[pltpu.VMEM((RET_HEADS, RET_DK, RET_DV), F32),
                        pltpu.VMEM((tm, d), F32), pltpu.VMEM((tm, d), BF16),
                        pltpu.VMEM((tm, RET_HEADS * RET_DK), BF16),
                        pltpu.VMEM((RET_HEADS * RET_DK, tm), F32),
                        pltpu.VMEM((tm, d), BF16), pltpu.VMEM((tm, d), F32),
                        pltpu.VMEM((tm, EVEN_OUT), BF16), pltpu.VMEM((tm, d), F32)],
        compiler_params=_params(2),
        name="even_prompt",
    )(x, g, w_in, ln_g, ln_b, w_s, b_s, w_out, cosf, sinf)


def _even_sample_kernel(x_ref, s0_ref, g_ref, win_ref, lng_ref, lnb_ref, ws_ref, bs_ref, wout_ref,
                        cos_ref, sin_ref, xo_ref, sout_ref, vrow_ref,
                        q_s, kz_s, v_s, o_s):
    tt = x_ref.shape[0]
    ls = tt // s0_ref.shape[0]
    x = x_ref[...]
    u, vn, qs, ks, vr, gate = _even_front(x, g_ref, win_ref, lng_ref, lnb_ref, cos_ref, sin_ref)
    vrow_ref[...] = vn
    vn_b = vn.astype(BF16)
    vr_b = vr.astype(BF16)

    row = lax.broadcasted_iota(jnp.int32, (tt, tt), 0)
    col = lax.broadcasted_iota(jnp.int32, (tt, tt), 1)
    same = (row // ls) == (col // ls)
    diff = (row - col).astype(F32)
    rloc = (lax.broadcasted_iota(jnp.int32, (tt, 1), 0) % ls).astype(F32)

    cat = []
    for g in range(SGU_GROUPS):
        cols = slice(g * SGU_GROUP_DIM, (g + 1) * SGU_GROUP_DIM)
        mixed = _dot(ws_ref[g], vn_b[:, cols]) + bs_ref[g]
        cat.append((u[:, cols] * mixed).astype(BF16))

    xi, cdec = [], []
    for hd in range(RET_HEADS):
        lg = _log_gamma(hd)
        kcols = slice(hd * RET_DK, (hd + 1) * RET_DK)
        vcols = slice(hd * RET_DV, (hd + 1) * RET_DV)
        decay = jnp.where(same & (diff >= 0), jnp.exp(jnp.maximum(diff, 0.0) * lg), 0.0)
        xi.append(jnp.exp((rloc[:ls] + 1.0) * lg))
        cdec.append(math.exp(ls * lg))
        zeta = jnp.exp((ls - 1.0 - rloc) * lg)
        sc = _dot_nt(qs[hd].astype(BF16), ks[hd].astype(BF16)) * decay
        o_s[:, vcols] = _dot(sc.astype(BF16), vr_b[:, vcols])
        q_s[:, kcols] = qs[hd]
        kz_s[:, kcols] = ks[hd] * zeta
    v_s[...] = vr

    def per_batch(bi, carry):
        rows = pl.ds(pl.multiple_of(bi * ls, ls), ls)
        for hd in range(RET_HEADS):
            kcols = slice(hd * RET_DK, (hd + 1) * RET_DK)
            vcols = slice(hd * RET_DV, (hd + 1) * RET_DV)
            s_prev = s0_ref[bi, hd]
            qb = q_s[rows, kcols].astype(BF16)
            o_s[rows, vcols] += _dot(qb, s_prev.astype(BF16)) * xi[hd]
            sout_ref[bi, hd] = s_prev * cdec[hd] + _dot_tn(kz_s[rows, kcols].astype(BF16),
                                                          v_s[rows, vcols].astype(BF16))
        return carry

    lax.fori_loop(0, s0_ref.shape[0], per_batch, 0, unroll=True)
    for hd in range(RET_HEADS):
        vcols = slice(hd * RET_DV, (hd + 1) * RET_DV)
        cat.append(_head_norm_gate(o_s[:, vcols], gate[:, vcols]).astype(BF16))
    xo_ref[...] = x + _dot(jnp.concatenate(cat, axis=-1), wout_ref[...])


def _even_sample(x, s0, j, g, w_in, ln_g, ln_b, w_s_bd, b_s_bd, w_out, cosf, sinf):
    n, d = x.shape
    nb = s0.shape[0]
    ls = n // nb
    bt = BT_EVEN_SAMPLE
    tt = bt * ls
    state_spec = pl.BlockSpec((bt, RET_HEADS, RET_DK, RET_DV), lambda i: (i, 0, 0, 0))
    return pl.pallas_call(
        _even_sample_kernel,
        grid=(nb // bt,),
        in_specs=[pl.BlockSpec((tt, d), lambda i: (i, 0)), state_spec,
                  _const((1, d)), _layer(j, (d, EVEN_IN)), _const((1, d)), _const((1, d)),
                  _const((SGU_GROUPS, tt, tt)), _const((SGU_GROUPS, tt, 1)),
                  _layer(j, (EVEN_OUT, d)), _const((tt, RET_DK)), _const((tt, RET_DK))],
        out_specs=[pl.BlockSpec((tt, d), lambda i: (i, 0)), state_spec,
                   pl.BlockSpec((tt, d), lambda i: (i, 0))],
        out_shape=[jax.ShapeDtypeStruct((n, d), F32),
                   jax.ShapeDtypeStruct(s0.shape, F32),
                   jax.ShapeDtypeStruct((n, d), F32)],
        scratch_shapes=[pltpu.VMEM((tt, RET_HEADS * RET_DK), F32),
                        pltpu.VMEM((tt, RET_HEADS * RET_DK), F32),
                        pltpu.VMEM((tt, d), F32), pltpu.VMEM((tt, d), F32)],
        compiler_params=_params(1),
        name="even_sample",
    )(x, s0, g, w_in, ln_g, ln_b, w_s_bd, b_s_bd, w_out, cosf, sinf)


def _pad_even_odd(blk, kv_in_block):
    lane = lax.broadcasted_iota(jnp.int32, blk.shape, 1)
    rolled = pltpu.roll(blk, C_HEAD_DIM, 1)
    if kv_in_block == 0:
        even = jnp.where(lane < C_HEAD_DIM, blk, 0.0)
        odd = jnp.where(lane >= C_HEAD_DIM, rolled, 0.0)
    else:
        odd = jnp.where(lane >= C_HEAD_DIM, blk, 0.0)
        even = jnp.where(lane < C_HEAD_DIM, rolled, 0.0)
    return even.astype(BF16), odd.astype(BF16)


def _sink_softmax(sc, sink):
    m = jnp.maximum(jnp.max(sc, axis=-1, keepdims=True), sink)
    p = jnp.exp2(sc - m)
    den = jnp.sum(p, axis=-1, keepdims=True) + jnp.exp2(sink - m)
    return p, jnp.broadcast_to(1.0 / den, (sc.shape[0], LANES))


def _pair_combine(o_even, o_odd, inv_even, inv_odd):
    lane = lax.broadcasted_iota(jnp.int32, o_even.shape, 1)
    return (o_even + o_odd) * jnp.where(lane < C_HEAD_DIM, inv_even, inv_odd)


def _pair_sinks(sink_ref, kh, first_half):
    h0 = kh * (C_HEADS // C_KV_HEADS)
    return (jnp.where(first_half, sink_ref[h0], sink_ref[h0 + 2]) * LOG2E,
            jnp.where(first_half, sink_ref[h0 + 1], sink_ref[h0 + 3]) * LOG2E)


def _odd_front(x, g_ref, wqkv_ref, bqkv_ref, cos_ref, lo_ref, hi_ref):
    h = _rms(x, g_ref[...]).astype(BF16)
    qkv = _dot(h, wqkv_ref[...]) + bqkv_ref[...]
    cos, lo, hi = cos_ref[...], lo_ref[...], hi_ref[...]
    nq = C_HEADS * C_HEAD_DIM
    nk = C_KV_HEADS * C_HEAD_DIM
    qb = [_win_rotate(qkv[:, j * LANES:(j + 1) * LANES], cos, lo, hi) * (C_HEAD_DIM ** -0.5 * LOG2E)
          for j in range(nq // LANES)]
    kb = [_win_rotate(qkv[:, nq + j * LANES:nq + (j + 1) * LANES], cos, lo, hi)
          for j in range(nk // LANES)]
    v = qkv[:, nq + nk:]
    return qb, kb, v


def _odd_prompt_kernel(sink_ref, x_ref, g_ref, wqkv_ref, bqkv_ref, wout_ref, bout_ref,
                       cos_ref, lo_ref, hi_ref, xo_ref, wk_ref, wv_ref,
                       q_s, kpad, kprev, vpad, att_s, pp_s, inv_s):
    t = pl.program_id(1)
    tm = x_ref.shape[1]
    w = CHUNK
    x = x_ref[0]
    qb, kb, v = _odd_front(x, g_ref, wqkv_ref, bqkv_ref, cos_ref, lo_ref, hi_ref)
    for j, blk in enumerate(qb):
        q_s[:, j * LANES:(j + 1) * LANES] = blk.astype(BF16)

    @pl.when(t == 0)
    def _():
        kprev[...] = jnp.zeros((2 * C_KV_HEADS, LANES, w), BF16)
        vpad[:, 0:w, :] = jnp.zeros((2 * C_KV_HEADS, w, LANES), BF16)

    kt = [blk.T for blk in kb]
    k_tails = []
    for kh in range(C_KV_HEADS):
        j, r = divmod(kh, LANES // C_HEAD_DIM)
        head = kt[j][r * C_HEAD_DIM:(r + 1) * C_HEAD_DIM, :]
        zeros = jnp.zeros_like(head)
        ke_t = jnp.concatenate([head, zeros], axis=0).astype(BF16)
        ko_t = jnp.concatenate([zeros, head], axis=0).astype(BF16)
        kpad[2 * kh] = ke_t
        kpad[2 * kh + 1] = ko_t
        k_tails += [ke_t[:, tm - w:], ko_t[:, tm - w:]]
        ve, vo = _pad_even_odd(v[:, j * LANES:(j + 1) * LANES], r)
        vpad[2 * kh, w:, :] = ve
        vpad[2 * kh + 1, w:, :] = vo

    @pl.when(t == pl.num_programs(1) - 1)
    def _():
        for j, blk in enumerate(kb):
            wk_ref[0, :, j * LANES:(j + 1) * LANES] = blk[tm - w:, :]
        wv_ref[0] = v[tm - w:, :]

    qi = lax.broadcasted_iota(jnp.int32, (2 * w, w), 0) % w
    kj = lax.broadcasted_iota(jnp.int32, (2 * w, w), 1)
    upper = kj > qi
    first_half = lax.broadcasted_iota(jnp.int32, (2 * w, 1), 0) < w

    zero = jnp.zeros((), F32)
    for c in range(tm // w):
        rows = pl.ds(c * w, w)
        kkeys = pl.ds((c - 1) * w, 2 * w)
        prev_bias = jnp.where(t == 0, NEG, 0.0) if c == 0 else None
        for kh in range(C_KV_HEADS):
            if c == 0:
                ke = jnp.concatenate([kprev[2 * kh], kpad[2 * kh, :, 0:w]], axis=1)
                ko = jnp.concatenate([kprev[2 * kh + 1], kpad[2 * kh + 1, :, 0:w]], axis=1)
            else:
                ke, ko = kpad[2 * kh, :, kkeys], kpad[2 * kh + 1, :, kkeys]
            c0 = (2 * kh) * LANES
            q2 = jnp.concatenate([q_s[rows, c0:c0 + LANES], q_s[rows, c0 + LANES:c0 + 2 * LANES]], 0)
            for half, (kk, sink) in enumerate(zip((ke, ko), _pair_sinks(sink_ref, kh, first_half))):
                s2 = _dot(q2, kk)
                s_prev = s2[:, :w] if prev_bias is None else s2[:, :w] + prev_bias
                p, inv = _sink_softmax(jnp.where(upper, s_prev, s2[:, w:]), sink)
                u = (c * C_KV_HEADS + kh) * 2 + half
                pp_s[u] = jnp.concatenate([jnp.where(upper, p, zero), jnp.where(upper, zero, p)],
                                          axis=1).astype(BF16)
                inv_s[u] = inv
    for c in range(tm // w):
        rows = pl.ds(c * w, w)
        keys = pl.ds(c * w, 2 * w)
        for kh in range(C_KV_HEADS):
            c0 = (2 * kh) * LANES
            u = (c * C_KV_HEADS + kh) * 2
            o = _pair_combine(_dot(pp_s[u], vpad[2 * kh, keys, :]), _dot(pp_s[u + 1], vpad[2 * kh + 1, keys, :]),
                              inv_s[u], inv_s[u + 1])
            att_s[rows, c0:c0 + LANES] = o[:w].astype(BF16)
            att_s[rows, c0 + LANES:c0 + 2 * LANES] = o[w:].astype(BF16)
    xo_ref[0] = x + _dot(att_s[...], wout_ref[...]) + bout_ref[...]
    for i, tail in enumerate(k_tails):
        kprev[i] = tail
    vpad[:, 0:w, :] = vpad[:, tm:tm + w, :]


def _odd_prompt(x, j, sinks, g, w_qkv, b_qkv, w_out, b_out, cos, lo, hi):
    b, l, d = x.shape
    tm = TM_PROMPT
    kvw = C_KV_HEADS * C_HEAD_DIM
    win_spec = pl.BlockSpec((1, CHUNK, kvw), lambda i, t: (i, 0, 0))
    tab = pl.BlockSpec((tm, LANES), lambda i, t: (t, 0))
    return pl.pallas_call(
        _odd_prompt_kernel,
        grid=(b, l // tm),
        in_specs=[pl.BlockSpec(memory_space=pltpu.SMEM),
                  pl.BlockSpec((1, tm, d), lambda i, t: (i, t, 0)),
                  _const((1, d)), _layer(j, (d, ODD_IN)), _const((1, ODD_IN)),
                  _layer(j, (d, d)), _const((1, d)), tab, tab, tab],
        out_specs=[pl.BlockSpec((1, tm, d), lambda i, t: (i, t, 0)), win_spec, win_spec],
        out_shape=[jax.ShapeDtypeStruct((b, l, d), F32),
                   jax.ShapeDtypeStruct((b, CHUNK, kvw), F32),
                   jax.ShapeDtypeStruct((b, CHUNK, kvw), F32)],
        scratch_shapes=[pltpu.VMEM((tm, d), BF16),
                        pltpu.VMEM((2 * C_KV_HEADS, LANES, tm), BF16),
                        pltpu.VMEM((2 * C_KV_HEADS, LANES, CHUNK), BF16),
                        pltpu.VMEM((2 * C_KV_HEADS, tm + CHUNK, LANES), BF16),
                        pltpu.VMEM((tm, d), BF16),
                        pltpu.VMEM((tm // CHUNK * C_KV_HEADS * 2, 2 * CHUNK, 2 * CHUNK), BF16),
                        pltpu.VMEM((tm // CHUNK * C_KV_HEADS * 2, 2 * CHUNK, LANES), F32)],
        compiler_params=_params(2),
        name="odd_prompt",
    )(sinks, x, g, w_qkv, b_qkv, w_out, b_out, cos, lo, hi)


def _odd_sample_kernel(sink_ref, x_ref, ck_ref, cv_ref, g_ref, wqkv_ref, bqkv_ref, wout_ref, bout_ref,
                       cos_ref, lo_ref, hi_ref, xo_ref, wk_ref, wv_ref, q_s, k_s, v_s, att_s, p_s, inv_s,
                       *, ls):
    i = pl.program_id(0)
    bt, w, kvw = ck_ref.shape
    tt = bt * ls

    @pl.when(i == 0)
    def _():
        qb, kb, v = _odd_front(x_ref[...], g_ref, wqkv_ref, bqkv_ref, cos_ref, lo_ref, hi_ref)
        for j, blk in enumerate(qb):
            q_s[:, j * LANES:(j + 1) * LANES] = blk.astype(BF16)
        for j, blk in enumerate(kb):
            k_s[:, j * LANES:(j + 1) * LANES] = blk
        v_s[...] = v

    rows = pl.ds(pl.multiple_of(i * tt, tt), tt)
    knew = k_s[rows, :]
    vnew = v_s[rows, :]
    ck = ck_ref[...]
    cv = cv_ref[...]
    for bi in range(bt):
        wk_ref[bi, 0:w - ls, :] = ck[bi, ls:, :]
        wk_ref[bi, w - ls:, :] = knew[bi * ls:(bi + 1) * ls, :]
        wv_ref[bi, 0:w - ls, :] = cv[bi, ls:, :]
        wv_ref[bi, w - ls:, :] = vnew[bi * ls:(bi + 1) * ls, :]
    kall = jnp.concatenate([ck.reshape(bt * w, kvw), knew], axis=0)
    vall = jnp.concatenate([cv.reshape(bt * w, kvw), vnew], axis=0)
    ns = bt * w + tt

    tok = lax.broadcasted_iota(jnp.int32, (2 * tt, ns), 0) % tt
    key = lax.broadcasted_iota(jnp.int32, (2 * tt, ns), 1)
    tb, tl = tok // ls, tok % ls
    cached = key < bt * w
    nkey = key - bt * w
    valid = ((cached & ((key // w) == tb) & ((key % w) > tl))
             | ((key >= bt * w) & ((nkey // ls) == tb) & ((nkey % ls) <= tl)))
    first_half = lax.broadcasted_iota(jnp.int32, (2 * tt, 1), 0) < tt

    for kh in range(C_KV_HEADS):
        j, r = divmod(kh, LANES // C_HEAD_DIM)
        c0 = (2 * kh) * LANES
        q2 = jnp.concatenate([q_s[rows, c0:c0 + LANES], q_s[rows, c0 + LANES:c0 + 2 * LANES]], axis=0)
        keys = _pad_even_odd(kall[:, j * LANES:(j + 1) * LANES], r)
        for half, (kk, sink) in enumerate(zip(keys, _pair_sinks(sink_ref, kh, first_half))):
            p, inv = _sink_softmax(jnp.where(valid, _dot_nt(q2, kk), NEG), sink)
            p_s[2 * kh + half] = p.astype(BF16)
            inv_s[2 * kh + half] = inv
    for kh in range(C_KV_HEADS):
        j, r = divmod(kh, LANES // C_HEAD_DIM)
        c0 = (2 * kh) * LANES
        ve, vo = _pad_even_odd(vall[:, j * LANES:(j + 1) * LANES], r)
        o = _pair_combine(_dot(p_s[2 * kh], ve), _dot(p_s[2 * kh + 1], vo), inv_s[2 * kh], inv_s[2 * kh + 1])
        att_s[rows, c0:c0 + LANES] = o[:tt].astype(BF16)
        att_s[rows, c0 + LANES:c0 + 2 * LANES] = o[tt:].astype(BF16)

    @pl.when(i == pl.num_programs(0) - 1)
    def _():
        xo_ref[...] = x_ref[...] + _dot(att_s[...], wout_ref[...]) + bout_ref[...]


def _odd_sample(x, ck, cv, j, sinks, g, w_qkv, b_qkv, w_out, b_out, cos, lo, hi):
    n, d = x.shape
    nb, w, kvw = ck.shape
    bt = BT_SAMPLE
    tt = bt * (n // nb)
    cache_spec = pl.BlockSpec((bt, w, kvw), lambda i: (i, 0, 0))
    return pl.pallas_call(
        functools.partial(_odd_sample_kernel, ls=n // nb),
        grid=(nb // bt,),
        in_specs=[pl.BlockSpec(memory_space=pltpu.SMEM),
                  _const((n, d)), cache_spec, cache_spec,
                  _const((1, d)), _layer(j, (d, ODD_IN)), _const((1, ODD_IN)),
                  _layer(j, (d, d)), _const((1, d)),
                  _const((n, LANES)), _const((n, LANES)), _const((n, LANES))],
        out_specs=[pl.BlockSpec((n, d), lambda i: (0, 0)), cache_spec, cache_spec],
        out_shape=[jax.ShapeDtypeStruct((n, d), F32),
                   jax.ShapeDtypeStruct(ck.shape, F32), jax.ShapeDtypeStruct(cv.shape, F32)],
        scratch_shapes=[pltpu.VMEM((n, d), BF16), pltpu.VMEM((n, kvw), F32), pltpu.VMEM((n, kvw), F32),
                        pltpu.VMEM((n, d), BF16),
                        pltpu.VMEM((2 * C_KV_HEADS, 2 * tt, bt * w + tt), BF16),
                        pltpu.VMEM((2 * C_KV_HEADS, 2 * tt, LANES), F32)],
        compiler_params=_params(1),
        name="odd_sample",
    )(sinks, x, ck, cv, g, w_qkv, b_qkv, w_out, b_out, cos, lo, hi)


def _cross_heads(q, mk, mv):
    outs = []
    for hd in range(MEM_HEADS):
        cols = slice(hd * MEM_HEAD_DIM, (hd + 1) * MEM_HEAD_DIM)
        s = _dot_nt(q[:, cols], mk[:, cols])
        p = jnp.exp(s - jnp.max(s, axis=-1, keepdims=True))
        den = jnp.sum(p, axis=-1, keepdims=True)
        outs.append(_dot(p.astype(BF16), mv[:, cols]) * (1.0 / den))
    return outs


def _mlp(x1, gf_ref, wup_ref, wdown_ref):
    h2 = _rms(x1, gf_ref[...]).astype(BF16)
    acc = x1
    for j in range(D_FF // D_MODEL):
        sl = slice(j * D_MODEL, (j + 1) * D_MODEL)
        a = jnp.square(jnp.maximum(_dot(h2, wup_ref[:, sl]), 0.0)).astype(BF16)
        acc = acc + _dot(a, wdown_ref[sl, :])
    return acc


def _cross_mlp_prompt_kernel(x_ref, mk_ref, mv_ref, gc_ref, wmq_ref, wmo_ref, gf_ref, wup_ref,
                             wdown_ref, gfin_ref, xo_ref, o_s, *, final):
    x = x_ref[0]
    h = _rms(x, gc_ref[...]).astype(BF16)
    q = (_dot(h, wmq_ref[...]) * (MEM_HEAD_DIM ** -0.5)).astype(BF16)
    outs = _cross_heads(q, mk_ref[0, 0], mv_ref[0, 0])
    for hd, o in enumerate(outs):
        o_s[:, hd * MEM_HEAD_DIM:(hd + 1) * MEM_HEAD_DIM] = o.astype(BF16)
    x1 = x + _dot(o_s[...], wmo_ref[...])
    y = _mlp(x1, gf_ref, wup_ref, wdown_ref)
    if final:
        y = _rms(y, gfin_ref[...])
    xo_ref[0] = y


def _cross_mlp_prompt(x, mk, mv, layer, g_cross, w_mq, w_mo, g_ffn, w_up, w_down, g_final, final):
    b, l, d = x.shape
    tm = TM_PROMPT
    mem_spec = pl.BlockSpec((1, 1, N_MEM, d), lambda i, t: (layer, i, 0, 0))
    return pl.pallas_call(
        functools.partial(_cross_mlp_prompt_kernel, final=final),
        grid=(b, l // tm),
        in_specs=[pl.BlockSpec((1, tm, d), lambda i, t: (i, t, 0)), mem_spec, mem_spec,
                  _const((1, d)), _layer(layer, (d, d)), _layer(layer, (d, d)), _const((1, d)),
                  _const((d, D_FF)), _const((D_FF, d)), _const((1, d))],
        out_specs=pl.BlockSpec((1, tm, d), lambda i, t: (i, t, 0)),
        out_shape=jax.ShapeDtypeStruct((b, l, d), F32),
        scratch_shapes=[pltpu.VMEM((tm, d), BF16)],
        compiler_params=_params(2),
        name=f"cross_mlp_prompt_{layer}",
    )(x, mk, mv, g_cross, w_mq, w_mo, g_ffn, w_up, w_down, g_final)


def _cross_sample_kernel(x_ref, mk_ref, mv_ref, gc_ref, wmq_ref, wmo_ref, xo_ref, q_s, o_s, p_s, inv_s):
    tt = x_ref.shape[0]
    bt = mk_ref.shape[1]
    ls = tt // bt
    nrow = N_MEM * MEM_HEADS
    x = x_ref[...]
    h = _rms(x, gc_ref[...]).astype(BF16)
    q_s[...] = _dot(h, wmq_ref[...]) * (MEM_HEAD_DIM ** -0.5)
    qhead = lax.broadcasted_iota(jnp.int32, (MEM_HEADS * ls, nrow), 0) // ls
    khead = lax.broadcasted_iota(jnp.int32, (MEM_HEADS * ls, nrow), 1) % MEM_HEADS
    own = qhead == khead

    for bi in range(bt):
        qb = q_s[bi * ls:(bi + 1) * ls, :]
        q4 = jnp.concatenate([qb[:, hd * MEM_HEAD_DIM:(hd + 1) * MEM_HEAD_DIM]
                              for hd in range(MEM_HEADS)], axis=0).astype(BF16)
        k2 = mk_ref[0, bi].reshape(nrow, MEM_HEAD_DIM).astype(BF16)
        s = jnp.where(own, _dot_nt(q4, k2), NEG)
        p = jnp.exp(s - jnp.max(s, axis=-1, keepdims=True))
        p_s[bi] = p.astype(BF16)
        inv_s[bi] = jnp.broadcast_to(1.0 / jnp.sum(p, axis=-1, keepdims=True), (MEM_HEADS * ls, LANES))
    for bi in range(bt):
        v2 = mv_ref[0, bi].reshape(nrow, MEM_HEAD_DIM).astype(BF16)
        inv = inv_s[bi]
        o4 = _dot(p_s[bi], v2) * jnp.concatenate([inv] * (MEM_HEAD_DIM // LANES), axis=1)
        for hd in range(MEM_HEADS):
            o_s[bi * ls:(bi + 1) * ls, hd * MEM_HEAD_DIM:(hd + 1) * MEM_HEAD_DIM] = o4[hd * ls:(hd + 1) * ls]
    xo_ref[...] = x + _dot(o_s[...].astype(BF16), wmo_ref[...])


def _cross_sample(x, mk, mv, layer, g_cross, w_mq, w_mo):
    n, d = x.shape
    nb = mk.shape[1]
    ls = n // nb
    bt = BT_SAMPLE
    tt = bt * ls
    mem_spec = pl.BlockSpec((1, bt, N_MEM, MEM_HEADS, MEM_HEAD_DIM), lambda i: (layer, i, 0, 0, 0))
    return pl.pallas_call(
        _cross_sample_kernel,
        grid=(nb // bt,),
        in_specs=[pl.BlockSpec((tt, d), lambda i: (i, 0)), mem_spec, mem_spec,
                  _const((1, d)), _layer(layer, (d, d)), _layer(layer, (d, d))],
        out_specs=pl.BlockSpec((tt, d), lambda i: (i, 0)),
        out_shape=jax.ShapeDtypeStruct((n, d), F32),
        scratch_shapes=[pltpu.VMEM((tt, d), F32), pltpu.VMEM((tt, d), F32),
                        pltpu.VMEM((bt, MEM_HEADS * ls, N_MEM * MEM_HEADS), BF16),
                        pltpu.VMEM((bt, MEM_HEADS * ls, LANES), F32)],
        compiler_params=_params(1),
        name=f"cross_sample_{layer}",
    )(x, mk, mv, g_cross, w_mq, w_mo)


def _mlp_stream_kernel(x_ref, gf_ref, wup_ref, wdown_ref, gfin_ref, xo_ref, wup_b_ref, wdown_b_ref,
                       h_s, acc_s, *, final):
    j = pl.program_id(0)

    @pl.when(j == 0)
    def _():
        x = x_ref[...]
        h_s[...] = _rms(x, gf_ref[...]).astype(BF16)
        acc_s[...] = x

    wu = wup_ref[...].astype(BF16)
    wd = wdown_ref[...].astype(BF16)
    wup_b_ref[...] = wu
    wdown_b_ref[...] = wd
    a = jnp.square(jnp.maximum(_dot(h_s[...], wu), 0.0)).astype(BF16)
    acc_s[...] += _dot(a, wd)

    @pl.when(j == pl.num_programs(0) - 1)
    def _():
        y = acc_s[...]
        if final:
            y = _rms(y, gfin_ref[...])
        xo_ref[...] = y


def _mlp_sample(x, layer, g_ffn, w_up, w_down, g_final, final):
    n, d = x.shape
    fc = FF_CHUNK
    return pl.pallas_call(
        functools.partial(_mlp_stream_kernel, final=final),
        grid=(D_FF // fc,),
        in_specs=[_const((n, d)), _const((1, d)),
                  pl.BlockSpec((None, d, fc), lambda c: (layer, 0, c)),
                  pl.BlockSpec((None, fc, d), lambda c: (layer, c, 0)),
                  _const((1, d))],
        out_specs=[pl.BlockSpec((n, d), lambda c: (0, 0)),
                   pl.BlockSpec((d, fc), lambda c: (0, c)),
                   pl.BlockSpec((fc, d), lambda c: (c, 0))],
        out_shape=[jax.ShapeDtypeStruct((n, d), F32),
                   jax.ShapeDtypeStruct((d, D_FF), BF16),
                   jax.ShapeDtypeStruct((D_FF, d), BF16)],
        scratch_shapes=[pltpu.VMEM((n, d), BF16), pltpu.VMEM((n, d), F32)],
        compiler_params=_params(1),
        name=f"mlp_sample_{layer}",
    )(x, g_ffn, w_up, w_down, g_final)


def kernel(x_prompt, x_sample, state_ret, cache_win_k, cache_win_v, cache_mem_k, cache_mem_v, mem_prompt, g_mix, w_in_e, sgu_ln_g, sgu_ln_b, w_spatial, b_spatial, w_out_e, w_qkv_o, b_qkv_o, sinks, w_out_o, b_out_o, g_cross, g_mem, w_mq, w_mk, w_mv, w_mo, g_ffn, w_up, w_down, g_final):
    bp, lp, d = x_prompt.shape
    bs, ls, _ = x_sample.shape
    assert TM_PROMPT % RET_CHUNK_PROMPT == 0 and RET_CHUNK_PROMPT % CHUNK == 0
    assert d == D_MODEL and lp % TM_PROMPT == 0 and bs % BT_SAMPLE == 0 and bs % BT_EVEN_SAMPLE == 0
    assert ls < CHUNK and CHUNK % ls == 0 and D_FF % FF_CHUNK == 0 and bp % MEMKV_BT == 0
    ns = bs * ls
    kvw = C_KV_HEADS * C_HEAD_DIM

    def row(v):
        return v.reshape(1, -1).astype(F32)

    def bf(w):
        return w.astype(BF16)

    pos_p = np.arange(lp)
    ret_tab_p = _ret_rope_tables(pos_p)
    tt_e = BT_EVEN_SAMPLE * ls
    ret_tab_s = _ret_rope_tables(PAST_LEN + (np.arange(tt_e) % ls))
    win_tab_p = _win_rope_tables(pos_p)
    win_tab_s = _win_rope_tables(PAST_LEN + (np.arange(ns) % ls))

    w_in_b, w_out_e_b, w_qkv_b, w_out_o_b = bf(w_in_e), bf(w_out_e), bf(w_qkv_o), bf(w_out_o)
    w_mq_b, w_mo_b = bf(w_mq), bf(w_mo)

    mk_p, mv_p, mk_pb, mv_pb = _memkv(mem_prompt, g_mem, w_mk, w_mv)
    mk_s, mv_s = cache_mem_k, cache_mem_v

    xp = x_prompt
    xs = x_sample.reshape(ns, d)
    final_g = row(g_final)
    outs = {}
    for layer in range(DEPTH):
        j = layer // 2
        g = row(g_mix[layer])
        if layer % 2 == 0:
            ln_g, ln_b = row(sgu_ln_g[j]), row(sgu_ln_b[j])
            tril = jnp.tril(jnp.ones((CHUNK, CHUNK), bool))
            w_s = bf(jnp.where(tril[None], w_spatial[j], 0.0))
            b_s = b_spatial[j].reshape(SGU_GROUPS, CHUNK, 1)
            xp, ret_p = _even_prompt(xp, j, g, w_in_b, ln_g, ln_b, w_s, b_s, w_out_e_b, *ret_tab_p)
            w_small = w_s[:, :ls, :ls]
            eye = jnp.eye(BT_EVEN_SAMPLE, dtype=BF16)
            w_bd = jnp.einsum("ab,gts->gatbs", eye, w_small).reshape(SGU_GROUPS, tt_e, tt_e)
            b_bd = jnp.tile(b_spatial[j][:, :ls], (1, BT_EVEN_SAMPLE)).reshape(SGU_GROUPS, tt_e, 1)
            xs, ret_s, v_rows = _even_sample(xs, state_ret[j], j, g, w_in_b, ln_g, ln_b, w_bd, b_bd,
                                             w_out_e_b, *ret_tab_s)
            outs["ret_p"], outs["ret_s"], outs["v_rows"] = ret_p, ret_s, v_rows
        else:
            b_qkv, b_out = row(b_qkv_o[j]), row(b_out_o[j])
            xp, wk_p, wv_p = _odd_prompt(xp, j, sinks[j], g, w_qkv_b, b_qkv, w_out_o_b, b_out,
                                         *win_tab_p)
            xs, wk_s, wv_s = _odd_sample(xs, cache_win_k[j].reshape(bs, CHUNK, kvw),
                                         cache_win_v[j].reshape(bs, CHUNK, kvw), j, sinks[j],
                                         g, w_qkv_b, b_qkv, w_out_o_b, b_out, *win_tab_s)
            outs["wk_p"], outs["wv_p"], outs["wk_s"], outs["wv_s"] = wk_p, wv_p, wk_s, wv_s
        final = layer == DEPTH - 1
        gc, gf = row(g_cross[layer]), row(g_ffn[layer])
        xs = _cross_sample(xs, mk_s, mv_s, layer, gc, w_mq_b, w_mo_b)
        xs, w_up_l, w_down_l = _mlp_sample(xs, layer, gf, w_up, w_down, final_g, final)
        xp = _cross_mlp_prompt(xp, mk_pb, mv_pb, layer, gc, w_mq_b, w_mo_b, gf, w_up_l, w_down_l,
                               final_g, final)

    n_even, n_odd = (DEPTH + 1) // 2, DEPTH // 2
    return (xp,
            xs.reshape(bs, ls, d),
            mk_p,
            mv_p,
            outs["ret_p"].reshape(n_even, bp, RET_HEADS, RET_DK, RET_DV),
            outs["ret_s"].reshape(n_even, bs, RET_HEADS, RET_DK, RET_DV),
            outs["v_rows"].reshape(n_even, bs, ls, SGU_GROUPS, SGU_GROUP_DIM),
            outs["wk_p"].reshape(n_odd, bp, CHUNK, C_KV_HEADS, C_HEAD_DIM),
            outs["wv_p"].reshape(n_odd, bp, CHUNK, C_KV_HEADS, C_HEAD_DIM),
            outs["wk_s"].reshape(n_odd, bs, CHUNK, C_KV_HEADS, C_HEAD_DIM),
            outs["wv_s"].reshape(n_odd, bs, CHUNK, C_KV_HEADS, C_HEAD_DIM))
```

```python
import functools
import math

import numpy as np
import jax
import jax.numpy as jnp
from jax import lax
from jax.experimental import pallas as pl
from jax.experimental.pallas import tpu as pltpu

F32 = jnp.float32
BF16 = jnp.bfloat16

D_MODEL = 1024
DEPTH = 2
PAST_LEN = 16384
EPS = 1e-6

SGU_GROUPS = 4
SGU_GROUP_DIM = 256
RET_HEADS = 4
RET_DK = 128
RET_DV = 256
RET_ROPE_BASE = 10000.0
CHUNK = 128
RET_CHUNK_PROMPT = 256
C_HEADS = 16
C_KV_HEADS = 4
C_HEAD_DIM = 64
C_ROPE_BASE = 150000.0
N_MEM = 256
MEM_HEADS = 4
MEM_HEAD_DIM = 256
D_FF = 4096
EVEN_IN = 5120
EVEN_OUT = 2048
ODD_IN = 1536

LANES = 128
VMEM_LIMIT = 56 * 2 ** 20
NEG = -1e30
LOG2E = math.log2(math.e)

TM_PROMPT = 512
TM_PROMPT_WIDE = 1024
BT_SAMPLE = 8
BT_EVEN_SAMPLE = 16
FF_CHUNK = 512
MEMKV_BT = 2


def _params(n_axes):
    return pltpu.CompilerParams(dimension_semantics=("arbitrary",) * n_axes,
                                vmem_limit_bytes=VMEM_LIMIT)


def _const(shape):
    nd = len(shape)
    return pl.BlockSpec(shape, lambda *_: (0,) * nd, pipeline_mode=pl.Buffered(1))


def _layer(layer, shape):
    nd = len(shape)
    return pl.BlockSpec((None,) + tuple(shape), lambda *_: (layer,) + (0,) * nd,
                        pipeline_mode=pl.Buffered(1))


def _rms(x, g):
    return x * lax.rsqrt(jnp.mean(x * x, axis=-1, keepdims=True) + EPS) * g


def _dot(a, b):
    return jnp.dot(a, b, preferred_element_type=F32)


def _dot_nt(a, b):
    return lax.dot_general(a, b, (((1,), (1,)), ((), ())), preferred_element_type=F32)


def _dot_tn(a, b):
    return lax.dot_general(a, b, (((0,), (0,)), ((), ())), preferred_element_type=F32)


def _ret_rope_tables(pos):
    half = RET_DK // 2
    inv = RET_ROPE_BASE ** (-np.arange(half, dtype=np.float64) / half)
    ang = pos.astype(np.float64)[:, None] * inv[None, :]
    cos, sin = np.cos(ang), np.sin(ang)
    return (jnp.asarray(np.concatenate([cos, cos], -1), F32),
            jnp.asarray(np.concatenate([-sin, sin], -1), F32))


def _win_rope_tables(pos):
    half = C_HEAD_DIM // 2
    inv = C_ROPE_BASE ** (-np.arange(half, dtype=np.float64) / half)
    ang = pos.astype(np.float64)[:, None] * inv[None, :]
    lane = np.arange(LANES)
    cos = np.cos(ang)[:, lane % half]
    sin = np.sin(ang)[:, lane % half]
    low = (lane % C_HEAD_DIM) < half
    return (jnp.asarray(cos, F32),
            jnp.asarray(np.where(low[None], -sin, 0.0), F32),
            jnp.asarray(np.where(low[None], 0.0, sin), F32))


def _ret_rotate(x, cosf, sinf):
    return x * cosf + pltpu.roll(x, RET_DK // 2, 1) * sinf


def _win_rotate(x, cos, sin_lo, sin_hi):
    return (x * cos + pltpu.roll(x, LANES - C_HEAD_DIM // 2, 1) * sin_lo
            + pltpu.roll(x, C_HEAD_DIM // 2, 1) * sin_hi)


def _log_gamma(h):
    return math.log1p(-2.0 ** (-5.0 - h))


def _memkv_kernel(mem_ref, g_ref, wk_ref, wv_ref, k_ref, v_ref, kb_ref, vb_ref, wk_s, wv_s):
    @pl.when(pl.program_id(1) == 0)
    def _():
        wk_s[...] = wk_ref[0].astype(BF16)
        wv_s[...] = wv_ref[0].astype(BF16)

    nb = mem_ref.shape[0]
    m = _rms(mem_ref[...].reshape(nb * N_MEM, D_MODEL), g_ref[0]).astype(BF16)
    for w_s, out_ref, bf_ref in ((wk_s, k_ref, kb_ref), (wv_s, v_ref, vb_ref)):
        kv = _dot(m, w_s[...])
        bf_ref[0] = kv.astype(BF16).reshape(nb, N_MEM, D_MODEL)
        for bi in range(nb):
            for hd in range(MEM_HEADS):
                out_ref[0, bi, :, hd, :] = kv[bi * N_MEM:(bi + 1) * N_MEM,
                                              hd * MEM_HEAD_DIM:(hd + 1) * MEM_HEAD_DIM]


def _memkv(mem, g_mem, w_mk, w_mv):
    b = mem.shape[0]
    nb = MEMKV_BT
    out = jax.ShapeDtypeStruct((DEPTH, b, N_MEM, MEM_HEADS, MEM_HEAD_DIM), F32)
    out_b = jax.ShapeDtypeStruct((DEPTH, b, N_MEM, D_MODEL), BF16)
    spec5 = pl.BlockSpec((1, nb, N_MEM, MEM_HEADS, MEM_HEAD_DIM), lambda l, i: (l, i, 0, 0, 0))
    spec_b = pl.BlockSpec((1, nb, N_MEM, D_MODEL), lambda l, i: (l, i, 0, 0))
    return pl.pallas_call(
        _memkv_kernel,
        grid=(DEPTH, b // nb),
        in_specs=[pl.BlockSpec((nb, N_MEM, D_MODEL), lambda l, i: (i, 0, 0)),
                  pl.BlockSpec((1, 1, D_MODEL), lambda l, i: (l, 0, 0)),
                  pl.BlockSpec((1, D_MODEL, D_MODEL), lambda l, i: (l, 0, 0)),
                  pl.BlockSpec((1, D_MODEL, D_MODEL), lambda l, i: (l, 0, 0))],
        out_specs=[spec5, spec5, spec_b, spec_b],
        out_shape=[out, out, out_b, out_b],
        scratch_shapes=[pltpu.VMEM((D_MODEL, D_MODEL), BF16), pltpu.VMEM((D_MODEL, D_MODEL), BF16)],
        compiler_params=_params(2),
        name="memkv",
    )(mem, g_mem.reshape(DEPTH, 1, D_MODEL), w_mk, w_mv)


def _even_front(x, g_ref, win_ref, lng_ref, lnb_ref, cos_ref, sin_ref):
    h = _rms(x, g_ref[...]).astype(BF16)

    def proj(lo, hi):
        return _dot(h, win_ref[:, lo:hi])

    u = jax.nn.gelu(proj(0, 1024))
    v = jax.nn.gelu(proj(1024, 2048))
    vc = v - jnp.mean(v, axis=-1, keepdims=True)
    vn = vc * lax.rsqrt(jnp.mean(vc * vc, axis=-1, keepdims=True) + EPS) * lng_ref[...] + lnb_ref[...]
    q = proj(2048, 2560)
    k = proj(2560, 3072)
    vr = proj(3072, 4096)
    gate = proj(4096, 5120)
    cosf, sinf = cos_ref[...], sin_ref[...]
    qs, ks = [], []
    for hd in range(RET_HEADS):
        cols = slice(hd * RET_DK, (hd + 1) * RET_DK)
        qs.append(_ret_rotate(q[:, cols], cosf, sinf))
        ks.append(_ret_rotate(k[:, cols], cosf, sinf) * (RET_DK ** -0.5))
    return u, vn, qs, ks, vr, gate


def _head_norm_gate(o, gate):
    on = o * lax.rsqrt(jnp.mean(o * o, axis=-1, keepdims=True) + EPS)
    return on * jax.nn.silu(gate)


def _even_prompt_kernel(x_ref, g_ref, win_ref, lng_ref, lnb_ref, ws_ref, bs_ref, wout_ref,
                        cos_ref, sin_ref, xo_ref, sout_ref,
                        state, u_s, vn_s, q_s, k_s, vr_s, gate_s, cat_s, oi_s):
    t = pl.program_id(1)
    tm = x_ref.shape[1]

    @pl.when(t == 0)
    def _():
        state[...] = jnp.zeros_like(state)

    x = x_ref[0]
    u, vn, qs, ks, vr, gate = _even_front(x, g_ref, win_ref, lng_ref, lnb_ref, cos_ref, sin_ref)
    u_s[...] = u
    vn_s[...] = vn.astype(BF16)
    for hd in range(RET_HEADS):
        cols = slice(hd * RET_DK, (hd + 1) * RET_DK)
        q_s[:, cols] = qs[hd].astype(BF16)
        k_s[cols, :] = ks[hd].T
    vr_s[...] = vr.astype(BF16)
    gate_s[...] = gate

    rc = RET_CHUNK_PROMPT
    row = lax.broadcasted_iota(jnp.int32, (rc, rc), 0)
    col = lax.broadcasted_iota(jnp.int32, (rc, rc), 1)
    diff = (row - col).astype(F32)
    ridx = lax.broadcasted_iota(jnp.int32, (rc, 1), 0).astype(F32)
    cidx = lax.broadcasted_iota(jnp.int32, (1, rc), 1).astype(F32)
    decay, xi, zeta, cdec = [], [], [], []
    for hd in range(RET_HEADS):
        lg = _log_gamma(hd)
        decay.append(jnp.where(diff >= 0, jnp.exp(jnp.maximum(diff, 0.0) * lg), 0.0))
        xi.append(jnp.exp((ridx + 1.0) * lg))
        zeta.append(jnp.exp((rc - 1.0 - cidx) * lg))
        cdec.append(math.exp(rc * lg))

    for c in range(tm // CHUNK):
        rows = pl.ds(c * CHUNK, CHUNK)
        for g in range(SGU_GROUPS):
            cols = slice(g * SGU_GROUP_DIM, (g + 1) * SGU_GROUP_DIM)
            mixed = _dot(ws_ref[g], vn_s[rows, cols]) + bs_ref[g]
            cat_s[rows, cols] = (u_s[rows, cols] * mixed).astype(BF16)
    for c in range(tm // rc):
        rows = pl.ds(c * rc, rc)
        for hd in range(RET_HEADS):
            kcols = slice(hd * RET_DK, (hd + 1) * RET_DK)
            vcols = slice(hd * RET_DV, (hd + 1) * RET_DV)
            sc = _dot(q_s[rows, kcols], k_s[kcols, rows].astype(BF16)) * decay[hd]
            oi_s[rows, vcols] = _dot(sc.astype(BF16), vr_s[rows, vcols])
    for c in range(tm // rc):
        rows = pl.ds(c * rc, rc)
        for hd in range(RET_HEADS):
            kcols = slice(hd * RET_DK, (hd + 1) * RET_DK)
            vcols = slice(hd * RET_DV, (hd + 1) * RET_DV)
            qc = q_s[rows, kcols]
            kt = k_s[kcols, rows]
            vc = vr_s[rows, vcols]
            s_prev = state[hd]
            o = oi_s[rows, vcols] + _dot(qc, s_prev.astype(BF16)) * xi[hd]
            state[hd] = s_prev * cdec[hd] + _dot((kt * zeta[hd]).astype(BF16), vc)
            ocols = slice(D_MODEL + hd * RET_DV, D_MODEL + (hd + 1) * RET_DV)
            cat_s[rows, ocols] = _head_norm_gate(o, gate_s[rows, vcols]).astype(BF16)
    xo_ref[0] = x + _dot(cat_s[...], wout_ref[...])

    @pl.when(t == pl.num_programs(1) - 1)
    def _():
        sout_ref[0] = state[...]


def _even_prompt(x, j, g, w_in, ln_g, ln_b, w_s, b_s, w_out, cosf, sinf):
    b, l, d = x.shape
    tm = TM_PROMPT
    return pl.pallas_call(
        _even_prompt_kernel,
        grid=(b, l // tm),
        in_specs=[pl.BlockSpec((1, tm, d), lambda i, t: (i, t, 0)),
                  _const((1, d)), _layer(j, (d, EVEN_IN)), _const((1, d)), _const((1, d)),
                  _const((SGU_GROUPS, CHUNK, CHUNK)), _const((SGU_GROUPS, CHUNK, 1)),
                  _layer(j, (EVEN_OUT, d)),
                  pl.BlockSpec((tm, RET_DK), lambda i, t: (t, 0)),
                  pl.BlockSpec((tm, RET_DK), lambda i, t: (t, 0))],
        out_specs=[pl.BlockSpec((1, tm, d), lambda i, t: (i, t, 0)),
                   pl.BlockSpec((1, RET_HEADS, RET_DK, RET_DV), lambda i, t: (i, 0, 0, 0))],
        out_shape=[jax.ShapeDtypeStruct((b, l, d), F32),
                   jax.ShapeDtypeStruct((b, RET_HEADS, RET_DK, RET_DV), F32)],
        scratch_shapes=[pltpu.VMEM((RET_HEADS, RET_DK, RET_DV), F32),
                        pltpu.VMEM((tm, d), F32), pltpu.VMEM((tm, d), BF16),
                        pltpu.VMEM((tm, RET_HEADS * RET_DK), BF16),
                        pltpu.VMEM((RET_HEADS * RET_DK, tm), F32),
                        pltpu.VMEM((tm, d), BF16), pltpu.VMEM((tm, d), F32),
                        pltpu.VMEM((tm, EVEN_OUT), BF16), pltpu.VMEM((tm, d), F32)],
        compiler_params=_params(2),
        name="even_prompt",
    )(x, g, w_in, ln_g, ln_b, w_s, b_s, w_out, cosf, sinf)


def _even_sample_kernel(x_ref, s0_ref, g_ref, win_ref, lng_ref, lnb_ref, ws_ref, bs_ref, wout_ref,
                        cos_ref, sin_ref, xo_ref, sout_ref, vrow_ref,
                        q_s, kz_s, v_s, o_s):
    tt = x_ref.shape[0]
    ls = tt // s0_ref.shape[0]
    x = x_ref[...]
    u, vn, qs, ks, vr, gate = _even_front(x, g_ref, win_ref, lng_ref, lnb_ref, cos_ref, sin_ref)
    vrow_ref[...] = vn
    vn_b = vn.astype(BF16)
    vr_b = vr.astype(BF16)

    row = lax.broadcasted_iota(jnp.int32, (tt, tt), 0)
    col = lax.broadcasted_iota(jnp.int32, (tt, tt), 1)
    same = (row // ls) == (col // ls)
    diff = (row - col).astype(F32)
    rloc = (lax.broadcasted_iota(jnp.int32, (tt, 1), 0) % ls).astype(F32)

    cat = []
    for g in range(SGU_GROUPS):
        cols = slice(g * SGU_GROUP_DIM, (g + 1) * SGU_GROUP_DIM)
        mixed = _dot(ws_ref[g], vn_b[:, cols]) + bs_ref[g]
        cat.append((u[:, cols] * mixed).astype(BF16))

    xi, cdec = [], []
    for hd in range(RET_HEADS):
        lg = _log_gamma(hd)
        kcols = slice(hd * RET_DK, (hd + 1) * RET_DK)
        vcols = slice(hd * RET_DV, (hd + 1) * RET_DV)
        decay = jnp.where(same & (diff >= 0), jnp.exp(jnp.maximum(diff, 0.0) * lg), 0.0)
        xi.append(jnp.exp((rloc[:ls] + 1.0) * lg))
        cdec.append(math.exp(ls * lg))
        zeta = jnp.exp((ls - 1.0 - rloc) * lg)
        sc = _dot_nt(qs[hd].astype(BF16), ks[hd].astype(BF16)) * decay
        o_s[:, vcols] = _dot(sc.astype(BF16), vr_b[:, vcols])
        q_s[:, kcols] = qs[hd]
        kz_s[:, kcols] = ks[hd] * zeta
    v_s[...] = vr

    def per_batch(bi, carry):
        rows = pl.ds(pl.multiple_of(bi * ls, ls), ls)
        for hd in range(RET_HEADS):
            kcols = slice(hd * RET_DK, (hd + 1) * RET_DK)
            vcols = slice(hd * RET_DV, (hd + 1) * RET_DV)
            s_prev = s0_ref[bi, hd]
            qb = q_s[rows, kcols].astype(BF16)
            o_s[rows, vcols] += _dot(qb, s_prev.astype(BF16)) * xi[hd]
            sout_ref[bi, hd] = s_prev * cdec[hd] + _dot_tn(kz_s[rows, kcols].astype(BF16),
                                                          v_s[rows, vcols].astype(BF16))
        return carry

    lax.fori_loop(0, s0_ref.shape[0], per_batch, 0, unroll=True)
    for hd in range(RET_HEADS):
        vcols = slice(hd * RET_DV, (hd + 1) * RET_DV)
        cat.append(_head_norm_gate(o_s[:, vcols], gate[:, vcols]).astype(BF16))
    xo_ref[...] = x + _dot(jnp.concatenate(cat, axis=-1), wout_ref[...])


def _even_sample(x, s0, j, g, w_in, ln_g, ln_b, w_s_bd, b_s_bd, w_out, cosf, sinf):
    n, d = x.shape
    nb = s0.shape[0]
    ls = n // nb
    bt = BT_EVEN_SAMPLE
    tt = bt * ls
    state_spec = pl.BlockSpec((bt, RET_HEADS, RET_DK, RET_DV), lambda i: (i, 0, 0, 0))
    return pl.pallas_call(
        _even_sample_kernel,
        grid=(nb // bt,),
        in_specs=[pl.BlockSpec((tt, d), lambda i: (i, 0)), state_spec,
                  _const((1, d)), _layer(j, (d, EVEN_IN)), _const((1, d)), _const((1, d)),
                  _const((SGU_GROUPS, tt, tt)), _const((SGU_GROUPS, tt, 1)),
                  _layer(j, (EVEN_OUT, d)), _const((tt, RET_DK)), _const((tt, RET_DK))],
        out_specs=[pl.BlockSpec((tt, d), lambda i: (i, 0)), state_spec,
                   pl.BlockSpec((tt, d), lambda i: (i, 0))],
        out_shape=[jax.ShapeDtypeStruct((n, d), F32),
                   jax.ShapeDtypeStruct(s0.shape, F32),
                   jax.ShapeDtypeStruct((n, d), F32)],
        scratch_shapes=[pltpu.VMEM((tt, RET_HEADS * RET_DK), F32),
                        pltpu.VMEM((tt, RET_HEADS * RET_DK), F32),
                        pltpu.VMEM((tt, d), F32), pltpu.VMEM((tt, d), F32)],
        compiler_params=_params(1),
        name="even_sample",
    )(x, s0, g, w_in, ln_g, ln_b, w_s_bd, b_s_bd, w_out, cosf, sinf)


def _pad_even_odd(blk, kv_in_block):
    lane = lax.broadcasted_iota(jnp.int32, blk.shape, 1)
    rolled = pltpu.roll(blk, C_HEAD_DIM, 1)
    if kv_in_block == 0:
        even = jnp.where(lane < C_HEAD_DIM, blk, 0.0)
        odd = jnp.where(lane >= C_HEAD_DIM, rolled, 0.0)
    else:
        odd = jnp.where(lane >= C_HEAD_DIM, blk, 0.0)
        even = jnp.where(lane < C_HEAD_DIM, rolled, 0.0)
    return even.astype(BF16), odd.astype(BF16)


def _sink_softmax(sc, sink):
    m = jnp.maximum(jnp.max(sc, axis=-1, keepdims=True), sink)
    p = jnp.exp2(sc - m)
    den = jnp.sum(p, axis=-1, keepdims=True) + jnp.exp2(sink - m)
    return p, jnp.broadcast_to(1.0 / den, (sc.shape[0], LANES))


def _pair_combine(o_even, o_odd, inv_even, inv_odd):
    lane = lax.broadcasted_iota(jnp.int32, o_even.shape, 1)
    return (o_even + o_odd) * jnp.where(lane < C_HEAD_DIM, inv_even, inv_odd)


def _pair_sinks(sink_ref, kh, first_half):
    h0 = kh * (C_HEADS // C_KV_HEADS)
    return (jnp.where(first_half, sink_ref[h0], sink_ref[h0 + 2]) * LOG2E,
            jnp.where(first_half, sink_ref[h0 + 1], sink_ref[h0 + 3]) * LOG2E)


def _odd_front(x, g_ref, wqkv_ref, bqkv_ref, cos_ref, lo_ref, hi_ref):
    h = _rms(x, g_ref[...]).astype(BF16)
    qkv = _dot(h, wqkv_ref[...]) + bqkv_ref[...]
    cos, lo, hi = cos_ref[...], lo_ref[...], hi_ref[...]
    nq = C_HEADS * C_HEAD_DIM
    nk = C_KV_HEADS * C_HEAD_DIM
    qb = [_win_rotate(qkv[:, j * LANES:(j + 1) * LANES], cos, lo, hi) * (C_HEAD_DIM ** -0.5 * LOG2E)
          for j in range(nq // LANES)]
    kb = [_win_rotate(qkv[:, nq + j * LANES:nq + (j + 1) * LANES], cos, lo, hi)
          for j in range(nk // LANES)]
    v = qkv[:, nq + nk:]
    return qb, kb, v


def _odd_prompt_kernel(sink_ref, x_ref, g_ref, wqkv_ref, bqkv_ref, wout_ref, bout_ref,
                       cos_ref, lo_ref, hi_ref, xo_ref, wk_ref, wv_ref,
                       q_s, kpad, kprev, vpad, att_s, pp_s, inv_s):
    t = pl.program_id(1)
    tm = x_ref.shape[1]
    w = CHUNK
    x = x_ref[0]
    qb, kb, v = _odd_front(x, g_ref, wqkv_ref, bqkv_ref, cos_ref, lo_ref, hi_ref)
    for j, blk in enumerate(qb):
        q_s[:, j * LANES:(j + 1) * LANES] = blk.astype(BF16)

    @pl.when(t == 0)
    def _():
        kprev[...] = jnp.zeros((2 * C_KV_HEADS, LANES, w), BF16)
        vpad[:, 0:w, :] = jnp.zeros((2 * C_KV_HEADS, w, LANES), BF16)

    kt = [blk.T for blk in kb]
    k_tails = []
    for kh in range(C_KV_HEADS):
        j, r = divmod(kh, LANES // C_HEAD_DIM)
        head = kt[j][r * C_HEAD_DIM:(r + 1) * C_HEAD_DIM, :]
        zeros = jnp.zeros_like(head)
        ke_t = jnp.concatenate([head, zeros], axis=0).astype(BF16)
        ko_t = jnp.concatenate([zeros, head], axis=0).astype(BF16)
        kpad[2 * kh] = ke_t
        kpad[2 * kh + 1] = ko_t
        k_tails += [ke_t[:, tm - w:], ko_t[:, tm - w:]]
        ve, vo = _pad_even_odd(v[:, j * LANES:(j + 1) * LANES], r)
        vpad[2 * kh, w:, :] = ve
        vpad[2 * kh + 1, w:, :] = vo

    @pl.when(t == pl.num_programs(1) - 1)
    def _():
        for j, blk in enumerate(kb):
            wk_ref[0, :, j * LANES:(j + 1) * LANES] = blk[tm - w:, :]
        wv_ref[0] = v[tm - w:, :]

    qi = lax.broadcasted_iota(jnp.int32, (2 * w, w), 0) % w
    kj = lax.broadcasted_iota(jnp.int32, (2 * w, w), 1)
    upper = kj > qi
    first_half = lax.broadcasted_iota(jnp.int32, (2 * w, 1), 0) < w

    zero = jnp.zeros((), F32)
    for c in range(tm // w):
        rows = pl.ds(c * w, w)
        kkeys = pl.ds((c - 1) * w, 2 * w)
        prev_bias = jnp.where(t == 0, NEG, 0.0) if c == 0 else None
        for kh in range(C_KV_HEADS):
            if c == 0:
                ke = jnp.concatenate([kprev[2 * kh], kpad[2 * kh, :, 0:w]], axis=1)
                ko = jnp.concatenate([kprev[2 * kh + 1], kpad[2 * kh + 1, :, 0:w]], axis=1)
            else:
                ke, ko = kpad[2 * kh, :, kkeys], kpad[2 * kh + 1, :, kkeys]
            c0 = (2 * kh) * LANES
            q2 = jnp.concatenate([q_s[rows, c0:c0 + LANES], q_s[rows, c0 + LANES:c0 + 2 * LANES]], 0)
            for half, (kk, sink) in enumerate(zip((ke, ko), _pair_sinks(sink_ref, kh, first_half))):
                s2 = _dot(q2, kk)
                s_prev = s2[:, :w] if prev_bias is None else s2[:, :w] + prev_bias
                p, inv = _sink_softmax(jnp.where(upper, s_prev, s2[:, w:]), sink)
                u = (c * C_KV_HEADS + kh) * 2 + half
                pp_s[u] = jnp.concatenate([jnp.where(upper, p, zero), jnp.where(upper, zero, p)],
                                          axis=1).astype(BF16)
                inv_s[u] = inv
    for c in range(tm // w):
        rows = pl.ds(c * w, w)
        keys = pl.ds(c * w, 2 * w)
        for kh in range(C_KV_HEADS):
            c0 = (2 * kh) * LANES
            u = (c * C_KV_HEADS + kh) * 2
            o = _pair_combine(_dot(pp_s[u], vpad[2 * kh, keys, :]), _dot(pp_s[u + 1], vpad[2 * kh + 1, keys, :]),
                              inv_s[u], inv_s[u + 1])
            att_s[rows, c0:c0 + LANES] = o[:w].astype(BF16)
            att_s[rows, c0 + LANES:c0 + 2 * LANES] = o[w:].astype(BF16)
    xo_ref[0] = x + _dot(att_s[...], wout_ref[...]) + bout_ref[...]
    for i, tail in enumerate(k_tails):
        kprev[i] = tail
    vpad[:, 0:w, :] = vpad[:, tm:tm + w, :]


def _odd_prompt(x, j, sinks, g, w_qkv, b_qkv, w_out, b_out, cos, lo, hi):
    b, l, d = x.shape
    tm = TM_PROMPT_WIDE
    kvw = C_KV_HEADS * C_HEAD_DIM
    win_spec = pl.BlockSpec((1, CHUNK, kvw), lambda i, t: (i, 0, 0))
    tab = pl.BlockSpec((tm, LANES), lambda i, t: (t, 0))
    return pl.pallas_call(
        _odd_prompt_kernel,
        grid=(b, l // tm),
        in_specs=[pl.BlockSpec(memory_space=pltpu.SMEM),
                  pl.BlockSpec((1, tm, d), lambda i, t: (i, t, 0)),
                  _const((1, d)), _layer(j, (d, ODD_IN)), _const((1, ODD_IN)),
                  _layer(j, (d, d)), _const((1, d)), tab, tab, tab],
        out_specs=[pl.BlockSpec((1, tm, d), lambda i, t: (i, t, 0)), win_spec, win_spec],
        out_shape=[jax.ShapeDtypeStruct((b, l, d), F32),
                   jax.ShapeDtypeStruct((b, CHUNK, kvw), F32),
                   jax.ShapeDtypeStruct((b, CHUNK, kvw), F32)],
        scratch_shapes=[pltpu.VMEM((tm, d), BF16),
                        pltpu.VMEM((2 * C_KV_HEADS, LANES, tm), BF16),
                        pltpu.VMEM((2 * C_KV_HEADS, LANES, CHUNK), BF16),
                        pltpu.VMEM((2 * C_KV_HEADS, tm + CHUNK, LANES), BF16),
                        pltpu.VMEM((tm, d), BF16),
                        pltpu.VMEM((tm // CHUNK * C_KV_HEADS * 2, 2 * CHUNK, 2 * CHUNK), BF16),
                        pltpu.VMEM((tm // CHUNK * C_KV_HEADS * 2, 2 * CHUNK, LANES), F32)],
        compiler_params=_params(2),
        name="odd_prompt",
    )(sinks, x, g, w_qkv, b_qkv, w_out, b_out, cos, lo, hi)


def _odd_sample_kernel(sink_ref, x_ref, ck_ref, cv_ref, g_ref, wqkv_ref, bqkv_ref, wout_ref, bout_ref,
                       cos_ref, lo_ref, hi_ref, xo_ref, wk_ref, wv_ref, q_s, k_s, v_s, att_s, p_s, inv_s,
                       *, ls):
    i = pl.program_id(0)
    bt, kvw, w = ck_ref.shape
    tt = bt * ls

    @pl.when(i == 0)
    def _():
        qb, kb, v = _odd_front(x_ref[...], g_ref, wqkv_ref, bqkv_ref, cos_ref, lo_ref, hi_ref)
        for j, blk in enumerate(qb):
            q_s[:, j * LANES:(j + 1) * LANES] = blk.astype(BF16)
        for j, blk in enumerate(kb):
            k_s[:, j * LANES:(j + 1) * LANES] = blk
        v_s[...] = v

    rows = pl.ds(pl.multiple_of(i * tt, tt), tt)
    knew = k_s[rows, :]
    vnew = v_s[rows, :]
    ck = [ck_ref[bi].T for bi in range(bt)]
    cv = [cv_ref[bi].T for bi in range(bt)]
    for bi in range(bt):
        wk_ref[bi, 0:w - ls, :] = ck[bi][ls:, :]
        wk_ref[bi, w - ls:, :] = knew[bi * ls:(bi + 1) * ls, :]
        wv_ref[bi, 0:w - ls, :] = cv[bi][ls:, :]
        wv_ref[bi, w - ls:, :] = vnew[bi * ls:(bi + 1) * ls, :]
    kall = jnp.concatenate(ck + [knew], axis=0)
    vall = jnp.concatenate(cv + [vnew], axis=0)
    ns = bt * w + tt

    tok = lax.broadcasted_iota(jnp.int32, (2 * tt, ns), 0) % tt
    key = lax.broadcasted_iota(jnp.int32, (2 * tt, ns), 1)
    tb, tl = tok // ls, tok % ls
    cached = key < bt * w
    nkey = key - bt * w
    valid = ((cached & ((key // w) == tb) & ((key % w) > tl))
             | ((key >= bt * w) & ((nkey // ls) == tb) & ((nkey % ls) <= tl)))
    first_half = lax.broadcasted_iota(jnp.int32, (2 * tt, 1), 0) < tt

    for kh in range(C_KV_HEADS):
        j, r = divmod(kh, LANES // C_HEAD_DIM)
        c0 = (2 * kh) * LANES
        q2 = jnp.concatenate([q_s[rows, c0:c0 + LANES], q_s[rows, c0 + LANES:c0 + 2 * LANES]], axis=0)
        keys = _pad_even_odd(kall[:, j * LANES:(j + 1) * LANES], r)
        for half, (kk, sink) in enumerate(zip(keys, _pair_sinks(sink_ref, kh, first_half))):
            p, inv = _sink_softmax(jnp.where(valid, _dot_nt(q2, kk), NEG), sink)
            p_s[2 * kh + half] = p.astype(BF16)
            inv_s[2 * kh + half] = inv
    for kh in range(C_KV_HEADS):
        j, r = divmod(kh, LANES // C_HEAD_DIM)
        c0 = (2 * kh) * LANES
        ve, vo = _pad_even_odd(vall[:, j * LANES:(j + 1) * LANES], r)
        o = _pair_combine(_dot(p_s[2 * kh], ve), _dot(p_s[2 * kh + 1], vo), inv_s[2 * kh], inv_s[2 * kh + 1])
        att_s[rows, c0:c0 + LANES] = o[:tt].astype(BF16)
        att_s[rows, c0 + LANES:c0 + 2 * LANES] = o[tt:].astype(BF16)

    @pl.when(i == pl.num_programs(0) - 1)
    def _():
        xo_ref[...] = x_ref[...] + _dot(att_s[...], wout_ref[...]) + bout_ref[...]


def _odd_sample(x, ck, cv, j, sinks, g, w_qkv, b_qkv, w_out, b_out, cos, lo, hi):
    n, d = x.shape
    nb, kvw, w = ck.shape
    bt = BT_SAMPLE
    tt = bt * (n // nb)
    cache_in = pl.BlockSpec((bt, kvw, w), lambda i: (i, 0, 0))
    cache_spec = pl.BlockSpec((bt, w, kvw), lambda i: (i, 0, 0))
    return pl.pallas_call(
        functools.partial(_odd_sample_kernel, ls=n // nb),
        grid=(nb // bt,),
        in_specs=[pl.BlockSpec(memory_space=pltpu.SMEM),
                  _const((n, d)), cache_in, cache_in,
                  _const((1, d)), _layer(j, (d, ODD_IN)), _const((1, ODD_IN)),
                  _layer(j, (d, d)), _const((1, d)),
                  _const((n, LANES)), _const((n, LANES)), _const((n, LANES))],
        out_specs=[pl.BlockSpec((n, d), lambda i: (0, 0)), cache_spec, cache_spec],
        out_shape=[jax.ShapeDtypeStruct((n, d), F32),
                   jax.ShapeDtypeStruct((nb, w, kvw), F32), jax.ShapeDtypeStruct((nb, w, kvw), F32)],
        scratch_shapes=[pltpu.VMEM((n, d), BF16), pltpu.VMEM((n, kvw), F32), pltpu.VMEM((n, kvw), F32),
                        pltpu.VMEM((n, d), BF16),
                        pltpu.VMEM((2 * C_KV_HEADS, 2 * tt, bt * w + tt), BF16),
                        pltpu.VMEM((2 * C_KV_HEADS, 2 * tt, LANES), F32)],
        compiler_params=_params(1),
        name="odd_sample",
    )(sinks, x, ck, cv, g, w_qkv, b_qkv, w_out, b_out, cos, lo, hi)


def _cross_heads(q, mk, mv):
    outs = []
    for hd in range(MEM_HEADS):
        cols = slice(hd * MEM_HEAD_DIM, (hd + 1) * MEM_HEAD_DIM)
        s = _dot_nt(q[:, cols], mk[:, cols])
        p = jnp.exp(s - jnp.max(s, axis=-1, keepdims=True))
        den = jnp.sum(p, axis=-1, keepdims=True)
        outs.append(_dot(p.astype(BF16), mv[:, cols]) * (1.0 / den))
    return outs


def _mlp(x1, gf_ref, wup_ref, wdown_ref):
    h2 = _rms(x1, gf_ref[...]).astype(BF16)
    acc = x1
    for j in range(D_FF // D_MODEL):
        sl = slice(j * D_MODEL, (j + 1) * D_MODEL)
        a = jnp.square(jnp.maximum(_dot(h2, wup_ref[:, sl]), 0.0)).astype(BF16)
        acc = acc + _dot(a, wdown_ref[sl, :])
    return acc


def _cross_mlp_prompt_kernel(x_ref, mk_ref, mv_ref, gc_ref, wmq_ref, wmo_ref, gf_ref, wup_ref,
                             wdown_ref, gfin_ref, xo_ref, o_s, *, final):
    x = x_ref[0]
    h = _rms(x, gc_ref[...]).astype(BF16)
    q = (_dot(h, wmq_ref[...]) * (MEM_HEAD_DIM ** -0.5)).astype(BF16)
    outs = _cross_heads(q, mk_ref[0, 0], mv_ref[0, 0])
    for hd, o in enumerate(outs):
        o_s[:, hd * MEM_HEAD_DIM:(hd + 1) * MEM_HEAD_DIM] = o.astype(BF16)
    x1 = x + _dot(o_s[...], wmo_ref[...])
    y = _mlp(x1, gf_ref, wup_ref, wdown_ref)
    if final:
        y = _rms(y, gfin_ref[...])
    xo_ref[0] = y


def _cross_mlp_prompt(x, mk, mv, layer, g_cross, w_mq, w_mo, g_ffn, w_up, w_down, g_final, final):
    b, l, d = x.shape
    tm = TM_PROMPT_WIDE
    mem_spec = pl.BlockSpec((1, 1, N_MEM, d), lambda i, t: (layer, i, 0, 0))
    return pl.pallas_call(
        functools.partial(_cross_mlp_prompt_kernel, final=final),
        grid=(b, l // tm),
        in_specs=[pl.BlockSpec((1, tm, d), lambda i, t: (i, t, 0)), mem_spec, mem_spec,
                  _const((1, d)), _layer(layer, (d, d)), _layer(layer, (d, d)), _const((1, d)),
                  _const((d, D_FF)), _const((D_FF, d)), _const((1, d))],
        out_specs=pl.BlockSpec((1, tm, d), lambda i, t: (i, t, 0)),
        out_shape=jax.ShapeDtypeStruct((b, l, d), F32),
        scratch_shapes=[pltpu.VMEM((tm, d), BF16)],
        compiler_params=_params(2),
        name=f"cross_mlp_prompt_{layer}",
    )(x, mk, mv, g_cross, w_mq, w_mo, g_ffn, w_up, w_down, g_final)


def _cross_sample_kernel(x_ref, mk_ref, mv_ref, gc_ref, wmq_ref, wmo_ref, xo_ref, q_s, o_s, p_s, inv_s):
    tt = x_ref.shape[0]
    bt = mk_ref.shape[1]
    ls = tt // bt
    nrow = N_MEM * MEM_HEADS
    x = x_ref[...]
    h = _rms(x, gc_ref[...]).astype(BF16)
    q_s[...] = _dot(h, wmq_ref[...]) * (MEM_HEAD_DIM ** -0.5)
    qhead = lax.broadcasted_iota(jnp.int32, (MEM_HEADS * ls, nrow), 0) // ls
    khead = lax.broadcasted_iota(jnp.int32, (MEM_HEADS * ls, nrow), 1) % MEM_HEADS
    own = qhead == khead

    for bi in range(bt):
        qb = q_s[bi * ls:(bi + 1) * ls, :]
        q4 = jnp.concatenate([qb[:, hd * MEM_HEAD_DIM:(hd + 1) * MEM_HEAD_DIM]
                              for hd in range(MEM_HEADS)], axis=0).astype(BF16)
        k2 = mk_ref[0, bi].reshape(nrow, MEM_HEAD_DIM).astype(BF16)
        s = jnp.where(own, _dot_nt(q4, k2), NEG)
        p = jnp.exp(s - jnp.max(s, axis=-1, keepdims=True))
        p_s[bi] = p.astype(BF16)
        inv_s[bi] = jnp.broadcast_to(1.0 / jnp.sum(p, axis=-1, keepdims=True), (MEM_HEADS * ls, LANES))
    for bi in range(bt):
        v2 = mv_ref[0, bi].reshape(nrow, MEM_HEAD_DIM).astype(BF16)
        inv = inv_s[bi]
        o4 = _dot(p_s[bi], v2) * jnp.concatenate([inv] * (MEM_HEAD_DIM // LANES), axis=1)
        for hd in range(MEM_HEADS):
            o_s[bi * ls:(bi + 1) * ls, hd * MEM_HEAD_DIM:(hd + 1) * MEM_HEAD_DIM] = o4[hd * ls:(hd + 1) * ls]
    xo_ref[...] = x + _dot(o_s[...].astype(BF16), wmo_ref[...])


def _cross_sample(x, mk, mv, layer, g_cross, w_mq, w_mo):
    n, d = x.shape
    nb = mk.shape[1]
    ls = n // nb
    bt = BT_SAMPLE
    tt = bt * ls
    mem_spec = pl.BlockSpec((1, bt, N_MEM, MEM_HEADS, MEM_HEAD_DIM), lambda i: (layer, i, 0, 0, 0))
    return pl.pallas_call(
        _cross_sample_kernel,
        grid=(nb // bt,),
        in_specs=[pl.BlockSpec((tt, d), lambda i: (i, 0)), mem_spec, mem_spec,
                  _const((1, d)), _layer(layer, (d, d)), _layer(layer, (d, d))],
        out_specs=pl.BlockSpec((tt, d), lambda i: (i, 0)),
        out_shape=jax.ShapeDtypeStruct((n, d), F32),
        scratch_shapes=[pltpu.VMEM((tt, d), F32), pltpu.VMEM((tt, d), F32),
                        pltpu.VMEM((bt, MEM_HEADS * ls, N_MEM * MEM_HEADS), BF16),
                        pltpu.VMEM((bt, MEM_HEADS * ls, LANES), F32)],
        compiler_params=_params(1),
        name=f"cross_sample_{layer}",
    )(x, mk, mv, g_cross, w_mq, w_mo)


def _mlp_stream_kernel(x_ref, gf_ref, wup_ref, wdown_ref, gfin_ref, xo_ref, wup_b_ref, wdown_b_ref,
                       h_s, acc_s, *, final):
    j = pl.program_id(0)

    @pl.when(j == 0)
    def _():
        x = x_ref[...]
        h_s[...] = _rms(x, gf_ref[...]).astype(BF16)
        acc_s[...] = x

    wu = wup_ref[...].astype(BF16)
    wd = wdown_ref[...].astype(BF16)
    wup_b_ref[...] = wu
    wdown_b_ref[...] = wd
    a = jnp.square(jnp.maximum(_dot(h_s[...], wu), 0.0)).astype(BF16)
    acc_s[...] += _dot(a, wd)

    @pl.when(j == pl.num_programs(0) - 1)
    def _():
        y = acc_s[...]
        if final:
            y = _rms(y, gfin_ref[...])
        xo_ref[...] = y


def _mlp_sample(x, layer, g_ffn, w_up, w_down, g_final, final):
    n, d = x.shape
    fc = FF_CHUNK
    return pl.pallas_call(
        functools.partial(_mlp_stream_kernel, final=final),
        grid=(D_FF // fc,),
        in_specs=[_const((n, d)), _const((1, d)),
                  pl.BlockSpec((None, d, fc), lambda c: (layer, 0, c)),
                  pl.BlockSpec((None, fc, d), lambda c: (layer, c, 0)),
                  _const((1, d))],
        out_specs=[pl.BlockSpec((n, d), lambda c: (0, 0)),
                   pl.BlockSpec((d, fc), lambda c: (0, c)),
                   pl.BlockSpec((fc, d), lambda c: (c, 0))],
        out_shape=[jax.ShapeDtypeStruct((n, d), F32),
                   jax.ShapeDtypeStruct((d, D_FF), BF16),
                   jax.ShapeDtypeStruct((D_FF, d), BF16)],
        scratch_shapes=[pltpu.VMEM((n, d), BF16), pltpu.VMEM((n, d), F32)],
        compiler_params=_params(1),
        name=f"mlp_sample_{layer}",
    )(x, g_ffn, w_up, w_down, g_final)


def kernel(x_prompt, x_sample, state_ret, cache_win_k, cache_win_v, cache_mem_k, cache_mem_v, mem_prompt, g_mix, w_in_e, sgu_ln_g, sgu_ln_b, w_spatial, b_spatial, w_out_e, w_qkv_o, b_qkv_o, sinks, w_out_o, b_out_o, g_cross, g_mem, w_mq, w_mk, w_mv, w_mo, g_ffn, w_up, w_down, g_final):
    bp, lp, d = x_prompt.shape
    bs, ls, _ = x_sample.shape
    assert TM_PROMPT % RET_CHUNK_PROMPT == 0 and RET_CHUNK_PROMPT % CHUNK == 0
    assert d == D_MODEL and lp % TM_PROMPT == 0 and lp % TM_PROMPT_WIDE == 0 and bs % BT_SAMPLE == 0 and bs % BT_EVEN_SAMPLE == 0
    assert ls < CHUNK and CHUNK % ls == 0 and D_FF % FF_CHUNK == 0 and bp % MEMKV_BT == 0
    ns = bs * ls
    kvw = C_KV_HEADS * C_HEAD_DIM

    def row(v):
        return v.reshape(1, -1).astype(F32)

    def bf(w):
        return w.astype(BF16)

    pos_p = np.arange(lp)
    ret_tab_p = _ret_rope_tables(pos_p)
    tt_e = BT_EVEN_SAMPLE * ls
    ret_tab_s = _ret_rope_tables(PAST_LEN + (np.arange(tt_e) % ls))
    win_tab_p = _win_rope_tables(pos_p)
    win_tab_s = _win_rope_tables(PAST_LEN + (np.arange(ns) % ls))

    w_in_b, w_out_e_b, w_qkv_b, w_out_o_b = bf(w_in_e), bf(w_out_e), bf(w_qkv_o), bf(w_out_o)
    w_mq_b, w_mo_b = bf(w_mq), bf(w_mo)

    mk_p, mv_p, mk_pb, mv_pb = _memkv(mem_prompt, g_mem, w_mk, w_mv)
    mk_s, mv_s = cache_mem_k, cache_mem_v

    xp = x_prompt
    xs = x_sample.reshape(ns, d)
    final_g = row(g_final)
    outs = {}
    for layer in range(DEPTH):
        j = layer // 2
        g = row(g_mix[layer])
        if layer % 2 == 0:
            ln_g, ln_b = row(sgu_ln_g[j]), row(sgu_ln_b[j])
            tril = jnp.tril(jnp.ones((CHUNK, CHUNK), bool))
            w_s = bf(jnp.where(tril[None], w_spatial[j], 0.0))
            b_s = b_spatial[j].reshape(SGU_GROUPS, CHUNK, 1)
            xp, ret_p = _even_prompt(xp, j, g, w_in_b, ln_g, ln_b, w_s, b_s, w_out_e_b, *ret_tab_p)
            w_small = w_s[:, :ls, :ls]
            eye = jnp.eye(BT_EVEN_SAMPLE, dtype=BF16)
            w_bd = jnp.einsum("ab,gts->gatbs", eye, w_small).reshape(SGU_GROUPS, tt_e, tt_e)
            b_bd = jnp.tile(b_spatial[j][:, :ls], (1, BT_EVEN_SAMPLE)).reshape(SGU_GROUPS, tt_e, 1)
            xs, ret_s, v_rows = _even_sample(xs, state_ret[j], j, g, w_in_b, ln_g, ln_b, w_bd, b_bd,
                                             w_out_e_b, *ret_tab_s)
            outs["ret_p"], outs["ret_s"], outs["v_rows"] = ret_p, ret_s, v_rows
        else:
            b_qkv, b_out = row(b_qkv_o[j]), row(b_out_o[j])
            xp, wk_p, wv_p = _odd_prompt(xp, j, sinks[j], g, w_qkv_b, b_qkv, w_out_o_b, b_out,
                                         *win_tab_p)
            xs, wk_s, wv_s = _odd_sample(xs, cache_win_k[j].transpose(0, 2, 3, 1).reshape(bs, kvw, CHUNK),
                                         cache_win_v[j].transpose(0, 2, 3, 1).reshape(bs, kvw, CHUNK), j, sinks[j],
                                         g, w_qkv_b, b_qkv, w_out_o_b, b_out, *win_tab_s)
            outs["wk_p"], outs["wv_p"], outs["wk_s"], outs["wv_s"] = wk_p, wv_p, wk_s, wv_s
        final = layer == DEPTH - 1
        gc, gf = row(g_cross[layer]), row(g_ffn[layer])
        xs = _cross_sample(xs, mk_s, mv_s, layer, gc, w_mq_b, w_mo_b)
        xs, w_up_l, w_down_l = _mlp_sample(xs, layer, gf, w_up, w_down, final_g, final)
        xp = _cross_mlp_prompt(xp, mk_pb, mv_pb, layer, gc, w_mq_b, w_mo_b, gf, w_up_l, w_down_l,
                               final_g, final)

    n_even, n_odd = (DEPTH + 1) // 2, DEPTH // 2
    return (xp,
            xs.reshape(bs, ls, d),
            mk_p,
            mv_p,
            outs["ret_p"].reshape(n_even, bp, RET_HEADS, RET_DK, RET_DV),
            outs["ret_s"].reshape(n_even, bs, RET_HEADS, RET_DK, RET_DV),
            outs["v_rows"].reshape(n_even, bs, ls, SGU_GROUPS, SGU_GROUP_DIM),
            outs["wk_p"].reshape(n_odd, bp, CHUNK, C_KV_HEADS, C_HEAD_DIM),
            outs["wv_p"].reshape(n_odd, bp, CHUNK, C_KV_HEADS, C_HEAD_DIM),
            outs["wk_s"].reshape(n_odd, bs, CHUNK, C_KV_HEADS, C_HEAD_DIM),
            outs["wv_s"].reshape(n_odd, bs, CHUNK, C_KV_HEADS, C_HEAD_DIM))
```

```python
import functools
import math

import numpy as np
import jax
import jax.numpy as jnp
from jax import lax
from jax.experimental import pallas as pl
from jax.experimental.pallas import tpu as pltpu

F32 = jnp.float32
BF16 = jnp.bfloat16

D_MODEL = 1024
DEPTH = 2
PAST_LEN = 16384
EPS = 1e-6

SGU_GROUPS = 4
SGU_GROUP_DIM = 256
RET_HEADS = 4
RET_DK = 128
RET_DV = 256
RET_ROPE_BASE = 10000.0
CHUNK = 128
RET_CHUNK_PROMPT = 256
C_HEADS = 16
C_KV_HEADS = 4
C_HEAD_DIM = 64
C_ROPE_BASE = 150000.0
N_MEM = 256
MEM_HEADS = 4
MEM_HEAD_DIM = 256
D_FF = 4096
EVEN_IN = 5120
EVEN_OUT = 2048
ODD_IN = 1536

LANES = 128
VMEM_LIMIT = 56 * 2 ** 20
NEG = -1e30
LOG2E = math.log2(math.e)

TM_PROMPT = 512
TM_PROMPT_WIDE = 1024
BT_SAMPLE = 8
BT_EVEN_SAMPLE = 16
FF_CHUNK = 512
MEMKV_BT = 2


def _params(n_axes):
    return pltpu.CompilerParams(dimension_semantics=("arbitrary",) * n_axes,
                                vmem_limit_bytes=VMEM_LIMIT)


def _const(shape):
    nd = len(shape)
    return pl.BlockSpec(shape, lambda *_: (0,) * nd, pipeline_mode=pl.Buffered(1))


def _layer(layer, shape):
    nd = len(shape)
    return pl.BlockSpec((None,) + tuple(shape), lambda *_: (layer,) + (0,) * nd,
                        pipeline_mode=pl.Buffered(1))


def _rms(x, g):
    return x * lax.rsqrt(jnp.mean(x * x, axis=-1, keepdims=True) + EPS) * g


def _dot(a, b):
    return jnp.dot(a, b, preferred_element_type=F32)


def _dot_nt(a, b):
    return lax.dot_general(a, b, (((1,), (1,)), ((), ())), preferred_element_type=F32)


def _dot_tn(a, b):
    return lax.dot_general(a, b, (((0,), (0,)), ((), ())), preferred_element_type=F32)


def _ret_rope_tables(pos):
    half = RET_DK // 2
    inv = RET_ROPE_BASE ** (-np.arange(half, dtype=np.float64) / half)
    ang = pos.astype(np.float64)[:, None] * inv[None, :]
    cos, sin = np.cos(ang), np.sin(ang)
    return (jnp.asarray(np.concatenate([cos, cos], -1), F32),
            jnp.asarray(np.concatenate([-sin, sin], -1), F32))


def _win_rope_tables(pos):
    half = C_HEAD_DIM // 2
    inv = C_ROPE_BASE ** (-np.arange(half, dtype=np.float64) / half)
    ang = pos.astype(np.float64)[:, None] * inv[None, :]
    lane = np.arange(LANES)
    cos = np.cos(ang)[:, lane % half]
    sin = np.sin(ang)[:, lane % half]
    low = (lane % C_HEAD_DIM) < half
    return (jnp.asarray(cos, F32),
            jnp.asarray(np.where(low[None], -sin, 0.0), F32),
            jnp.asarray(np.where(low[None], 0.0, sin), F32))


def _ret_rotate(x, cosf, sinf):
    return x * cosf + pltpu.roll(x, RET_DK // 2, 1) * sinf


def _win_rotate(x, cos, sin_lo, sin_hi):
    return (x * cos + pltpu.roll(x, LANES - C_HEAD_DIM // 2, 1) * sin_lo
            + pltpu.roll(x, C_HEAD_DIM // 2, 1) * sin_hi)


def _log_gamma(h):
    return math.log1p(-2.0 ** (-5.0 - h))


def _memkv_kernel(mem_ref, g_ref, wk_ref, wv_ref, k_ref, v_ref, kb_ref, vb_ref, wk_s, wv_s):
    @pl.when(pl.program_id(1) == 0)
    def _():
        wk_s[...] = wk_ref[0].astype(BF16)
        wv_s[...] = wv_ref[0].astype(BF16)

    nb = mem_ref.shape[0]
    m = _rms(mem_ref[...].reshape(nb * N_MEM, D_MODEL), g_ref[0]).astype(BF16)
    for w_s, out_ref, bf_ref in ((wk_s, k_ref, kb_ref), (wv_s, v_ref, vb_ref)):
        kv = _dot(m, w_s[...])
        bf_ref[0] = kv.astype(BF16).reshape(nb, N_MEM, D_MODEL)
        for bi in range(nb):
            for hd in range(MEM_HEADS):
                out_ref[0, bi, :, hd, :] = kv[bi * N_MEM:(bi + 1) * N_MEM,
                                              hd * MEM_HEAD_DIM:(hd + 1) * MEM_HEAD_DIM]


def _memkv(mem, g_mem, w_mk, w_mv):
    b = mem.shape[0]
    nb = MEMKV_BT
    out = jax.ShapeDtypeStruct((DEPTH, b, N_MEM, MEM_HEADS, MEM_HEAD_DIM), F32)
    out_b = jax.ShapeDtypeStruct((DEPTH, b, N_MEM, D_MODEL), BF16)
    spec5 = pl.BlockSpec((1, nb, N_MEM, MEM_HEADS, MEM_HEAD_DIM), lambda l, i: (l, i, 0, 0, 0))
    spec_b = pl.BlockSpec((1, nb, N_MEM, D_MODEL), lambda l, i: (l, i, 0, 0))
    return pl.pallas_call(
        _memkv_kernel,
        grid=(DEPTH, b // nb),
        in_specs=[pl.BlockSpec((nb, N_MEM, D_MODEL), lambda l, i: (i, 0, 0)),
                  pl.BlockSpec((1, 1, D_MODEL), lambda l, i: (l, 0, 0)),
                  pl.BlockSpec((1, D_MODEL, D_MODEL), lambda l, i: (l, 0, 0)),
                  pl.BlockSpec((1, D_MODEL, D_MODEL), lambda l, i: (l, 0, 0))],
        out_specs=[spec5, spec5, spec_b, spec_b],
        out_shape=[out, out, out_b, out_b],
        scratch_shapes=[pltpu.VMEM((D_MODEL, D_MODEL), BF16), pltpu.VMEM((D_MODEL, D_MODEL), BF16)],
        compiler_params=_params(2),
        name="memkv",
    )(mem, g_mem.reshape(DEPTH, 1, D_MODEL), w_mk, w_mv)


def _even_front(x, g_ref, win_ref, lng_ref, lnb_ref, cos_ref, sin_ref):
    h = _rms(x, g_ref[...]).astype(BF16)

    def proj(lo, hi):
        return _dot(h, win_ref[:, lo:hi])

    u = jax.nn.gelu(proj(0, 1024))
    v = jax.nn.gelu(proj(1024, 2048))
    vc = v - jnp.mean(v, axis=-1, keepdims=True)
    vn = vc * lax.rsqrt(jnp.mean(vc * vc, axis=-1, keepdims=True) + EPS) * lng_ref[...] + lnb_ref[...]
    q = proj(2048, 2560)
    k = proj(2560, 3072)
    vr = proj(3072, 4096)
    gate = proj(4096, 5120)
    cosf, sinf = cos_ref[...], sin_ref[...]
    qs, ks = [], []
    for hd in range(RET_HEADS):
        cols = slice(hd * RET_DK, (hd + 1) * RET_DK)
        qs.append(_ret_rotate(q[:, cols], cosf, sinf))
        ks.append(_ret_rotate(k[:, cols], cosf, sinf) * (RET_DK ** -0.5))
    return u, vn, qs, ks, vr, gate


def _head_norm_gate(o, gate):
    on = o * lax.rsqrt(jnp.mean(o * o, axis=-1, keepdims=True) + EPS)
    return on * jax.nn.silu(gate)


def _even_prompt_kernel(x_ref, g_ref, win_ref, lng_ref, lnb_ref, ws_ref, bs_ref, wout_ref,
                        cos_ref, sin_ref, xo_ref, sout_ref,
                        state, u_s, vn_s, q_s, k_s, vr_s, gate_s, cat_s, oi_s):
    t = pl.program_id(1)
    tm = x_ref.shape[1]

    @pl.when(t == 0)
    def _():
        state[...] = jnp.zeros_like(state)

    x = x_ref[0]
    u, vn, qs, ks, vr, gate = _even_front(x, g_ref, win_ref, lng_ref, lnb_ref, cos_ref, sin_ref)
    u_s[...] = u
    vn_s[...] = vn.astype(BF16)
    for hd in range(RET_HEADS):
        cols = slice(hd * RET_DK, (hd + 1) * RET_DK)
        q_s[:, cols] = qs[hd].astype(BF16)
        k_s[cols, :] = ks[hd].T
    vr_s[...] = vr.astype(BF16)
    gate_s[...] = gate

    rc = RET_CHUNK_PROMPT
    row = lax.broadcasted_iota(jnp.int32, (rc, rc), 0)
    col = lax.broadcasted_iota(jnp.int32, (rc, rc), 1)
    diff = (row - col).astype(F32)
    ridx = lax.broadcasted_iota(jnp.int32, (rc, 1), 0).astype(F32)
    cidx = lax.broadcasted_iota(jnp.int32, (1, rc), 1).astype(F32)
    decay, xi, zeta, cdec = [], [], [], []
    for hd in range(RET_HEADS):
        lg = _log_gamma(hd)
        decay.append(jnp.where(diff >= 0, jnp.exp(jnp.maximum(diff, 0.0) * lg), 0.0))
        xi.append(jnp.exp((ridx + 1.0) * lg))
        zeta.append(jnp.exp((rc - 1.0 - cidx) * lg))
        cdec.append(math.exp(rc * lg))

    for c in range(tm // rc):
        rows = pl.ds(c * rc, rc)
        for hd in range(RET_HEADS):
            kcols = slice(hd * RET_DK, (hd + 1) * RET_DK)
            vcols = slice(hd * RET_DV, (hd + 1) * RET_DV)
            sc = _dot(q_s[rows, kcols], k_s[kcols, rows].astype(BF16)) * decay[hd]
            oi_s[rows, vcols] = _dot(sc.astype(BF16), vr_s[rows, vcols])
    for c in range(tm // CHUNK):
        rows = pl.ds(c * CHUNK, CHUNK)
        for g in range(SGU_GROUPS):
            cols = slice(g * SGU_GROUP_DIM, (g + 1) * SGU_GROUP_DIM)
            mixed = _dot(ws_ref[g], vn_s[rows, cols]) + bs_ref[g]
            cat_s[rows, cols] = (u_s[rows, cols] * mixed).astype(BF16)
    for c in range(tm // rc):
        rows = pl.ds(c * rc, rc)
        for hd in range(RET_HEADS):
            kcols = slice(hd * RET_DK, (hd + 1) * RET_DK)
            vcols = slice(hd * RET_DV, (hd + 1) * RET_DV)
            qc = q_s[rows, kcols]
            kt = k_s[kcols, rows]
            vc = vr_s[rows, vcols]
            s_prev = state[hd]
            o = oi_s[rows, vcols] + _dot(qc, s_prev.astype(BF16)) * xi[hd]
            state[hd] = s_prev * cdec[hd] + _dot((kt * zeta[hd]).astype(BF16), vc)
            ocols = slice(D_MODEL + hd * RET_DV, D_MODEL + (hd + 1) * RET_DV)
            cat_s[rows, ocols] = _head_norm_gate(o, gate_s[rows, vcols]).astype(BF16)
    xo_ref[0] = x + _dot(cat_s[...], wout_ref[...])

    @pl.when(t == pl.num_programs(1) - 1)
    def _():
        sout_ref[0] = state[...]


def _even_prompt(x, j, g, w_in, ln_g, ln_b, w_s, b_s, w_out, cosf, sinf):
    b, l, d = x.shape
    tm = TM_PROMPT
    return pl.pallas_call(
        _even_prompt_kernel,
        grid=(b, l // tm),
        in_specs=[pl.BlockSpec((1, tm, d), lambda i, t: (i, t, 0)),
                  _const((1, d)), _layer(j, (d, EVEN_IN)), _const((1, d)), _const((1, d)),
                  _const((SGU_GROUPS, CHUNK, CHUNK)), _const((SGU_GROUPS, CHUNK, 1)),
                  _layer(j, (EVEN_OUT, d)),
                  pl.BlockSpec((tm, RET_DK), lambda i, t: (t, 0)),
                  pl.BlockSpec((tm, RET_DK), lambda i, t: (t, 0))],
        out_specs=[pl.BlockSpec((1, tm, d), lambda i, t: (i, t, 0)),
                   pl.BlockSpec((1, RET_HEADS, RET_DK, RET_DV), lambda i, t: (i, 0, 0, 0))],
        out_shape=[jax.ShapeDtypeStruct((b, l, d), F32),
                   jax.ShapeDtypeStruct((b, RET_HEADS, RET_DK, RET_DV), F32)],
        scratch_shapes=[pltpu.VMEM((RET_HEADS, RET_DK, RET_DV), F32),
                        pltpu.VMEM((tm, d), F32), pltpu.VMEM((tm, d), BF16),
                        pltpu.VMEM((tm, RET_HEADS * RET_DK), BF16),
                        pltpu.VMEM((RET_HEADS * RET_DK, tm), F32),
                        pltpu.VMEM((tm, d), BF16), pltpu.VMEM((tm, d), F32),
                        pltpu.VMEM((tm, EVEN_OUT), BF16), pltpu.VMEM((tm, d), F32)],
        compiler_params=_params(2),
        name="even_prompt",
    )(x, g, w_in, ln_g, ln_b, w_s, b_s, w_out, cosf, sinf)


def _even_sample_kernel(x_ref, s0_ref, g_ref, win_ref, lng_ref, lnb_ref, ws_ref, bs_ref, wout_ref,
                        cos_ref, sin_ref, xo_ref, sout_ref, vrow_ref,
                        q_s, kz_s, v_s, o_s):
    tt = x_ref.shape[0]
    ls = tt // s0_ref.shape[0]
    x = x_ref[...]
    u, vn, qs, ks, vr, gate = _even_front(x, g_ref, win_ref, lng_ref, lnb_ref, cos_ref, sin_ref)
    vrow_ref[...] = vn
    vn_b = vn.astype(BF16)
    vr_b = vr.astype(BF16)

    row = lax.broadcasted_iota(jnp.int32, (tt, tt), 0)
    col = lax.broadcasted_iota(jnp.int32, (tt, tt), 1)
    same = (row // ls) == (col // ls)
    diff = (row - col).astype(F32)
    rloc = (lax.broadcasted_iota(jnp.int32, (tt, 1), 0) % ls).astype(F32)

    cat = []
    for g in range(SGU_GROUPS):
        cols = slice(g * SGU_GROUP_DIM, (g + 1) * SGU_GROUP_DIM)
        mixed = _dot(ws_ref[g], vn_b[:, cols]) + bs_ref[g]
        cat.append((u[:, cols] * mixed).astype(BF16))

    xi, cdec = [], []
    for hd in range(RET_HEADS):
        lg = _log_gamma(hd)
        kcols = slice(hd * RET_DK, (hd + 1) * RET_DK)
        vcols = slice(hd * RET_DV, (hd + 1) * RET_DV)
        decay = jnp.where(same & (diff >= 0), jnp.exp(jnp.maximum(diff, 0.0) * lg), 0.0)
        xi.append(jnp.exp((rloc[:ls] + 1.0) * lg))
        cdec.append(math.exp(ls * lg))
        zeta = jnp.exp((ls - 1.0 - rloc) * lg)
        sc = _dot_nt(qs[hd].astype(BF16), ks[hd].astype(BF16)) * decay
        o_s[:, vcols] = _dot(sc.astype(BF16), vr_b[:, vcols])
        q_s[:, kcols] = qs[hd]
        kz_s[:, kcols] = ks[hd] * zeta
    v_s[...] = vr

    def per_batch(bi, carry):
        rows = pl.ds(pl.multiple_of(bi * ls, ls), ls)
        for hd in range(RET_HEADS):
            kcols = slice(hd * RET_DK, (hd + 1) * RET_DK)
            vcols = slice(hd * RET_DV, (hd + 1) * RET_DV)
            s_prev = s0_ref[bi, hd]
            qb = q_s[rows, kcols].astype(BF16)
            o_s[rows, vcols] += _dot(qb, s_prev.astype(BF16)) * xi[hd]
            sout_ref[bi, hd] = s_prev * cdec[hd] + _dot_tn(kz_s[rows, kcols].astype(BF16),
                                                          v_s[rows, vcols].astype(BF16))
        return carry

    lax.fori_loop(0, s0_ref.shape[0], per_batch, 0, unroll=True)
    for hd in range(RET_HEADS):
        vcols = slice(hd * RET_DV, (hd + 1) * RET_DV)
        cat.append(_head_norm_gate(o_s[:, vcols], gate[:, vcols]).astype(BF16))
    xo_ref[...] = x + _dot(jnp.concatenate(cat, axis=-1), wout_ref[...])


def _even_sample(x, s0, j, g, w_in, ln_g, ln_b, w_s_bd, b_s_bd, w_out, cosf, sinf):
    n, d = x.shape
    nb = s0.shape[0]
    ls = n // nb
    bt = BT_EVEN_SAMPLE
    tt = bt * ls
    state_spec = pl.BlockSpec((bt, RET_HEADS, RET_DK, RET_DV), lambda i: (i, 0, 0, 0))
    return pl.pallas_call(
        _even_sample_kernel,
        grid=(nb // bt,),
        in_specs=[pl.BlockSpec((tt, d), lambda i: (i, 0)), state_spec,
                  _const((1, d)), _layer(j, (d, EVEN_IN)), _const((1, d)), _const((1, d)),
                  _const((SGU_GROUPS, tt, tt)), _const((SGU_GROUPS, tt, 1)),
                  _layer(j, (EVEN_OUT, d)), _const((tt, RET_DK)), _const((tt, RET_DK))],
        out_specs=[pl.BlockSpec((tt, d), lambda i: (i, 0)), state_spec,
                   pl.BlockSpec((tt, d), lambda i: (i, 0))],
        out_shape=[jax.ShapeDtypeStruct((n, d), F32),
                   jax.ShapeDtypeStruct(s0.shape, F32),
                   jax.ShapeDtypeStruct((n, d), F32)],
        scratch_shapes=[pltpu.VMEM((tt, RET_HEADS * RET_DK), F32),
                        pltpu.VMEM((tt, RET_HEADS * RET_DK), F32),
                        pltpu.VMEM((tt, d), F32), pltpu.VMEM((tt, d), F32)],
        compiler_params=_params(1),
        name="even_sample",
    )(x, s0, g, w_in, ln_g, ln_b, w_s_bd, b_s_bd, w_out, cosf, sinf)


def _pad_even_odd(blk, kv_in_block):
    lane = lax.broadcasted_iota(jnp.int32, blk.shape, 1)
    rolled = pltpu.roll(blk, C_HEAD_DIM, 1)
    if kv_in_block == 0:
        even = jnp.where(lane < C_HEAD_DIM, blk, 0.0)
        odd = jnp.where(lane >= C_HEAD_DIM, rolled, 0.0)
    else:
        odd = jnp.where(lane >= C_HEAD_DIM, blk, 0.0)
        even = jnp.where(lane < C_HEAD_DIM, rolled, 0.0)
    return even.astype(BF16), odd.astype(BF16)


def _sink_softmax(sc, sink):
    m = jnp.maximum(jnp.max(sc, axis=-1, keepdims=True), sink)
    p = jnp.exp2(sc - m)
    den = jnp.sum(p, axis=-1, keepdims=True) + jnp.exp2(sink - m)
    return p, jnp.broadcast_to(1.0 / den, (sc.shape[0], LANES))


def _pair_combine(o_even, o_odd, inv_even, inv_odd):
    lane = lax.broadcasted_iota(jnp.int32, o_even.shape, 1)
    return (o_even + o_odd) * jnp.where(lane < C_HEAD_DIM, inv_even, inv_odd)


def _pair_sinks(sink_ref, kh, first_half):
    h0 = kh * (C_HEADS // C_KV_HEADS)
    return (jnp.where(first_half, sink_ref[h0], sink_ref[h0 + 2]) * LOG2E,
            jnp.where(first_half, sink_ref[h0 + 1], sink_ref[h0 + 3]) * LOG2E)


def _odd_front(x, g_ref, wqkv_ref, bqkv_ref, cos_ref, lo_ref, hi_ref):
    h = _rms(x, g_ref[...]).astype(BF16)
    qkv = _dot(h, wqkv_ref[...]) + bqkv_ref[...]
    cos, lo, hi = cos_ref[...], lo_ref[...], hi_ref[...]
    nq = C_HEADS * C_HEAD_DIM
    nk = C_KV_HEADS * C_HEAD_DIM
    qb = [_win_rotate(qkv[:, j * LANES:(j + 1) * LANES], cos, lo, hi) * (C_HEAD_DIM ** -0.5 * LOG2E)
          for j in range(nq // LANES)]
    kb = [_win_rotate(qkv[:, nq + j * LANES:nq + (j + 1) * LANES], cos, lo, hi)
          for j in range(nk // LANES)]
    v = qkv[:, nq + nk:]
    return qb, kb, v


def _odd_prompt_kernel(sink_ref, x_ref, g_ref, wqkv_ref, bqkv_ref, wout_ref, bout_ref,
                       cos_ref, lo_ref, hi_ref, xo_ref, wk_ref, wv_ref,
                       q_s, kpad, kprev, vpad, att_s, pp_s, inv_s):
    t = pl.program_id(1)
    tm = x_ref.shape[1]
    w = CHUNK
    x = x_ref[0]
    qb, kb, v = _odd_front(x, g_ref, wqkv_ref, bqkv_ref, cos_ref, lo_ref, hi_ref)
    for j, blk in enumerate(qb):
        q_s[:, j * LANES:(j + 1) * LANES] = blk.astype(BF16)

    @pl.when(t == 0)
    def _():
        kprev[...] = jnp.zeros((2 * C_KV_HEADS, LANES, w), BF16)
        vpad[:, 0:w, :] = jnp.zeros((2 * C_KV_HEADS, w, LANES), BF16)

    kt = [blk.T for blk in kb]
    k_tails = []
    for kh in range(C_KV_HEADS):
        j, r = divmod(kh, LANES // C_HEAD_DIM)
        head = kt[j][r * C_HEAD_DIM:(r + 1) * C_HEAD_DIM, :]
        zeros = jnp.zeros_like(head)
        ke_t = jnp.concatenate([head, zeros], axis=0).astype(BF16)
        ko_t = jnp.concatenate([zeros, head], axis=0).astype(BF16)
        kpad[2 * kh] = ke_t
        kpad[2 * kh + 1] = ko_t
        k_tails += [ke_t[:, tm - w:], ko_t[:, tm - w:]]
        ve, vo = _pad_even_odd(v[:, j * LANES:(j + 1) * LANES], r)
        vpad[2 * kh, w:, :] = ve
        vpad[2 * kh + 1, w:, :] = vo

    @pl.when(t == pl.num_programs(1) - 1)
    def _():
        for j, blk in enumerate(kb):
            wk_ref[0, :, j * LANES:(j + 1) * LANES] = blk[tm - w:, :]
        wv_ref[0] = v[tm - w:, :]

    qi = lax.broadcasted_iota(jnp.int32, (2 * w, w), 0) % w
    kj = lax.broadcasted_iota(jnp.int32, (2 * w, w), 1)
    upper = kj > qi
    first_half = lax.broadcasted_iota(jnp.int32, (2 * w, 1), 0) < w

    zero = jnp.zeros((), F32)
    for c in range(tm // w):
        rows = pl.ds(c * w, w)
        kkeys = pl.ds((c - 1) * w, 2 * w)
        prev_bias = jnp.where(t == 0, NEG, 0.0) if c == 0 else None
        for kh in range(C_KV_HEADS):
            if c == 0:
                ke = jnp.concatenate([kprev[2 * kh], kpad[2 * kh, :, 0:w]], axis=1)
                ko = jnp.concatenate([kprev[2 * kh + 1], kpad[2 * kh + 1, :, 0:w]], axis=1)
            else:
                ke, ko = kpad[2 * kh, :, kkeys], kpad[2 * kh + 1, :, kkeys]
            c0 = (2 * kh) * LANES
            q2 = jnp.concatenate([q_s[rows, c0:c0 + LANES], q_s[rows, c0 + LANES:c0 + 2 * LANES]], 0)
            for half, (kk, sink) in enumerate(zip((ke, ko), _pair_sinks(sink_ref, kh, first_half))):
                s2 = _dot(q2, kk)
                s_prev = s2[:, :w] if prev_bias is None else s2[:, :w] + prev_bias
                p, inv = _sink_softmax(jnp.where(upper, s_prev, s2[:, w:]), sink)
                u = (c * C_KV_HEADS + kh) * 2 + half
                pp_s[u] = jnp.concatenate([jnp.where(upper, p, zero), jnp.where(upper, zero, p)],
                                          axis=1).astype(BF16)
                inv_s[u] = inv
    for c in range(tm // w):
        rows = pl.ds(c * w, w)
        keys = pl.ds(c * w, 2 * w)
        for kh in range(C_KV_HEADS):
            c0 = (2 * kh) * LANES
            u = (c * C_KV_HEADS + kh) * 2
            o = _pair_combine(_dot(pp_s[u], vpad[2 * kh, keys, :]), _dot(pp_s[u + 1], vpad[2 * kh + 1, keys, :]),
                              inv_s[u], inv_s[u + 1])
            att_s[rows, c0:c0 + LANES] = o[:w].astype(BF16)
            att_s[rows, c0 + LANES:c0 + 2 * LANES] = o[w:].astype(BF16)
    xo_ref[0] = x + _dot(att_s[...], wout_ref[...]) + bout_ref[...]
    for i, tail in enumerate(k_tails):
        kprev[i] = tail
    vpad[:, 0:w, :] = vpad[:, tm:tm + w, :]


def _odd_prompt(x, j, sinks, g, w_qkv, b_qkv, w_out, b_out, cos, lo, hi):
    b, l, d = x.shape
    tm = TM_PROMPT_WIDE
    kvw = C_KV_HEADS * C_HEAD_DIM
    win_spec = pl.BlockSpec((1, CHUNK, kvw), lambda i, t: (i, 0, 0))
    tab = pl.BlockSpec((tm, LANES), lambda i, t: (t, 0))
    return pl.pallas_call(
        _odd_prompt_kernel,
        grid=(b, l // tm),
        in_specs=[pl.BlockSpec(memory_space=pltpu.SMEM),
                  pl.BlockSpec((1, tm, d), lambda i, t: (i, t, 0)),
                  _const((1, d)), _layer(j, (d, ODD_IN)), _const((1, ODD_IN)),
                  _layer(j, (d, d)), _const((1, d)), tab, tab, tab],
        out_specs=[pl.BlockSpec((1, tm, d), lambda i, t: (i, t, 0)), win_spec, win_spec],
        out_shape=[jax.ShapeDtypeStruct((b, l, d), F32),
                   jax.ShapeDtypeStruct((b, CHUNK, kvw), F32),
                   jax.ShapeDtypeStruct((b, CHUNK, kvw), F32)],
        scratch_shapes=[pltpu.VMEM((tm, d), BF16),
                        pltpu.VMEM((2 * C_KV_HEADS, LANES, tm), BF16),
                        pltpu.VMEM((2 * C_KV_HEADS, LANES, CHUNK), BF16),
                        pltpu.VMEM((2 * C_KV_HEADS, tm + CHUNK, LANES), BF16),
                        pltpu.VMEM((tm, d), BF16),
                        pltpu.VMEM((tm // CHUNK * C_KV_HEADS * 2, 2 * CHUNK, 2 * CHUNK), BF16),
                        pltpu.VMEM((tm // CHUNK * C_KV_HEADS * 2, 2 * CHUNK, LANES), F32)],
        compiler_params=_params(2),
        name="odd_prompt",
    )(sinks, x, g, w_qkv, b_qkv, w_out, b_out, cos, lo, hi)


def _odd_sample_kernel(sink_ref, x_ref, ck_ref, cv_ref, g_ref, wqkv_ref, bqkv_ref, wout_ref, bout_ref,
                       cos_ref, lo_ref, hi_ref, xo_ref, wk_ref, wv_ref, q_s, k_s, v_s, att_s, p_s, inv_s,
                       *, ls):
    i = pl.program_id(0)
    bt, kvw, w = ck_ref.shape
    tt = bt * ls

    @pl.when(i == 0)
    def _():
        qb, kb, v = _odd_front(x_ref[...], g_ref, wqkv_ref, bqkv_ref, cos_ref, lo_ref, hi_ref)
        for j, blk in enumerate(qb):
            q_s[:, j * LANES:(j + 1) * LANES] = blk.astype(BF16)
        for j, blk in enumerate(kb):
            k_s[:, j * LANES:(j + 1) * LANES] = blk
        v_s[...] = v

    rows = pl.ds(pl.multiple_of(i * tt, tt), tt)
    knew = k_s[rows, :]
    vnew = v_s[rows, :]
    ck = [ck_ref[bi].T for bi in range(bt)]
    cv = [cv_ref[bi].T for bi in range(bt)]
    for bi in range(bt):
        wk_ref[bi, 0:w - ls, :] = ck[bi][ls:, :]
        wk_ref[bi, w - ls:, :] = knew[bi * ls:(bi + 1) * ls, :]
        wv_ref[bi, 0:w - ls, :] = cv[bi][ls:, :]
        wv_ref[bi, w - ls:, :] = vnew[bi * ls:(bi + 1) * ls, :]
    kall = jnp.concatenate(ck + [knew], axis=0)
    vall = jnp.concatenate(cv + [vnew], axis=0)
    ns = bt * w + tt

    tok = lax.broadcasted_iota(jnp.int32, (2 * tt, ns), 0) % tt
    key = lax.broadcasted_iota(jnp.int32, (2 * tt, ns), 1)
    tb, tl = tok // ls, tok % ls
    cached = key < bt * w
    nkey = key - bt * w
    valid = ((cached & ((key // w) == tb) & ((key % w) > tl))
             | ((key >= bt * w) & ((nkey // ls) == tb) & ((nkey % ls) <= tl)))
    first_half = lax.broadcasted_iota(jnp.int32, (2 * tt, 1), 0) < tt

    for kh in range(C_KV_HEADS):
        j, r = divmod(kh, LANES // C_HEAD_DIM)
        c0 = (2 * kh) * LANES
        q2 = jnp.concatenate([q_s[rows, c0:c0 + LANES], q_s[rows, c0 + LANES:c0 + 2 * LANES]], axis=0)
        keys = _pad_even_odd(kall[:, j * LANES:(j + 1) * LANES], r)
        for half, (kk, sink) in enumerate(zip(keys, _pair_sinks(sink_ref, kh, first_half))):
            p, inv = _sink_softmax(jnp.where(valid, _dot_nt(q2, kk), NEG), sink)
            p_s[2 * kh + half] = p.astype(BF16)
            inv_s[2 * kh + half] = inv
    for kh in range(C_KV_HEADS):
        j, r = divmod(kh, LANES // C_HEAD_DIM)
        c0 = (2 * kh) * LANES
        ve, vo = _pad_even_odd(vall[:, j * LANES:(j + 1) * LANES], r)
        o = _pair_combine(_dot(p_s[2 * kh], ve), _dot(p_s[2 * kh + 1], vo), inv_s[2 * kh], inv_s[2 * kh + 1])
        att_s[rows, c0:c0 + LANES] = o[:tt].astype(BF16)
        att_s[rows, c0 + LANES:c0 + 2 * LANES] = o[tt:].astype(BF16)

    @pl.when(i == pl.num_programs(0) - 1)
    def _():
        xo_ref[...] = x_ref[...] + _dot(att_s[...], wout_ref[...]) + bout_ref[...]


def _odd_sample(x, ck, cv, j, sinks, g, w_qkv, b_qkv, w_out, b_out, cos, lo, hi):
    n, d = x.shape
    nb, kvw, w = ck.shape
    bt = BT_SAMPLE
    tt = bt * (n // nb)
    cache_in = pl.BlockSpec((bt, kvw, w), lambda i: (i, 0, 0))
    cache_spec = pl.BlockSpec((bt, w, kvw), lambda i: (i, 0, 0))
    return pl.pallas_call(
        functools.partial(_odd_sample_kernel, ls=n // nb),
        grid=(nb // bt,),
        in_specs=[pl.BlockSpec(memory_space=pltpu.SMEM),
                  _const((n, d)), cache_in, cache_in,
                  _const((1, d)), _layer(j, (d, ODD_IN)), _const((1, ODD_IN)),
                  _layer(j, (d, d)), _const((1, d)),
                  _const((n, LANES)), _const((n, LANES)), _const((n, LANES))],
        out_specs=[pl.BlockSpec((n, d), lambda i: (0, 0)), cache_spec, cache_spec],
        out_shape=[jax.ShapeDtypeStruct((n, d), F32),
                   jax.ShapeDtypeStruct((nb, w, kvw), F32), jax.ShapeDtypeStruct((nb, w, kvw), F32)],
        scratch_shapes=[pltpu.VMEM((n, d), BF16), pltpu.VMEM((n, kvw), F32), pltpu.VMEM((n, kvw), F32),
                        pltpu.VMEM((n, d), BF16),
                        pltpu.VMEM((2 * C_KV_HEADS, 2 * tt, bt * w + tt), BF16),
                        pltpu.VMEM((2 * C_KV_HEADS, 2 * tt, LANES), F32)],
        compiler_params=_params(1),
        name="odd_sample",
    )(sinks, x, ck, cv, g, w_qkv, b_qkv, w_out, b_out, cos, lo, hi)


def _cross_heads(q, mk, mv):
    outs = []
    for hd in range(MEM_HEADS):
        cols = slice(hd * MEM_HEAD_DIM, (hd + 1) * MEM_HEAD_DIM)
        s = _dot_nt(q[:, cols], mk[:, cols])
        p = jnp.exp(s - jnp.max(s, axis=-1, keepdims=True))
        den = jnp.sum(p, axis=-1, keepdims=True)
        outs.append(_dot(p.astype(BF16), mv[:, cols]) * (1.0 / den))
    return outs


def _mlp(x1, gf_ref, wup_ref, wdown_ref):
    h2 = _rms(x1, gf_ref[...]).astype(BF16)
    acc = x1
    for j in range(D_FF // D_MODEL):
        sl = slice(j * D_MODEL, (j + 1) * D_MODEL)
        a = jnp.square(jnp.maximum(_dot(h2, wup_ref[:, sl]), 0.0)).astype(BF16)
        acc = acc + _dot(a, wdown_ref[sl, :])
    return acc


def _cross_mlp_prompt_kernel(x_ref, mk_ref, mv_ref, gc_ref, wmq_ref, wmo_ref, gf_ref, wup_ref,
                             wdown_ref, gfin_ref, xo_ref, o_s, *, final):
    x = x_ref[0]
    h = _rms(x, gc_ref[...]).astype(BF16)
    q = (_dot(h, wmq_ref[...]) * (MEM_HEAD_DIM ** -0.5)).astype(BF16)
    outs = _cross_heads(q, mk_ref[0, 0], mv_ref[0, 0])
    for hd, o in enumerate(outs):
        o_s[:, hd * MEM_HEAD_DIM:(hd + 1) * MEM_HEAD_DIM] = o.astype(BF16)
    x1 = x + _dot(o_s[...], wmo_ref[...])
    y = _mlp(x1, gf_ref, wup_ref, wdown_ref)
    if final:
        y = _rms(y, gfin_ref[...])
    xo_ref[0] = y


def _cross_mlp_prompt(x, mk, mv, layer, g_cross, w_mq, w_mo, g_ffn, w_up, w_down, g_final, final):
    b, l, d = x.shape
    tm = TM_PROMPT_WIDE
    mem_spec = pl.BlockSpec((1, 1, N_MEM, d), lambda i, t: (layer, i, 0, 0))
    return pl.pallas_call(
        functools.partial(_cross_mlp_prompt_kernel, final=final),
        grid=(b, l // tm),
        in_specs=[pl.BlockSpec((1, tm, d), lambda i, t: (i, t, 0)), mem_spec, mem_spec,
                  _const((1, d)), _layer(layer, (d, d)), _layer(layer, (d, d)), _const((1, d)),
                  _const((d, D_FF)), _const((D_FF, d)), _const((1, d))],
        out_specs=pl.BlockSpec((1, tm, d), lambda i, t: (i, t, 0)),
        out_shape=jax.ShapeDtypeStruct((b, l, d), F32),
        scratch_shapes=[pltpu.VMEM((tm, d), BF16)],
        compiler_params=_params(2),
        name=f"cross_mlp_prompt_{layer}",
    )(x, mk, mv, g_cross, w_mq, w_mo, g_ffn, w_up, w_down, g_final)


def _cross_sample_kernel(x_ref, mk_ref, mv_ref, gc_ref, wmq_ref, wmo_ref, xo_ref, q_s, o_s, p_s, inv_s):
    tt = x_ref.shape[0]
    bt = mk_ref.shape[1]
    ls = tt // bt
    nrow = N_MEM * MEM_HEADS
    x = x_ref[...]
    h = _rms(x, gc_ref[...]).astype(BF16)
    q_s[...] = _dot(h, wmq_ref[...]) * (MEM_HEAD_DIM ** -0.5)
    qhead = lax.broadcasted_iota(jnp.int32, (MEM_HEADS * ls, nrow), 0) // ls
    khead = lax.broadcasted_iota(jnp.int32, (MEM_HEADS * ls, nrow), 1) % MEM_HEADS
    own = qhead == khead

    for bi in range(bt):
        qb = q_s[bi * ls:(bi + 1) * ls, :]
        q4 = jnp.concatenate([qb[:, hd * MEM_HEAD_DIM:(hd + 1) * MEM_HEAD_DIM]
                              for hd in range(MEM_HEADS)], axis=0).astype(BF16)
        k2 = mk_ref[0, bi].reshape(nrow, MEM_HEAD_DIM).astype(BF16)
        s = jnp.where(own, _dot_nt(q4, k2), NEG)
        p = jnp.exp(s - jnp.max(s, axis=-1, keepdims=True))
        p_s[bi] = p.astype(BF16)
        inv_s[bi] = jnp.broadcast_to(1.0 / jnp.sum(p, axis=-1, keepdims=True), (MEM_HEADS * ls, LANES))
    for bi in range(bt):
        v2 = mv_ref[0, bi].reshape(nrow, MEM_HEAD_DIM).astype(BF16)
        inv = inv_s[bi]
        o4 = _dot(p_s[bi], v2) * jnp.concatenate([inv] * (MEM_HEAD_DIM // LANES), axis=1)
        for hd in range(MEM_HEADS):
            o_s[bi * ls:(bi + 1) * ls, hd * MEM_HEAD_DIM:(hd + 1) * MEM_HEAD_DIM] = o4[hd * ls:(hd + 1) * ls]
    xo_ref[...] = x + _dot(o_s[...].astype(BF16), wmo_ref[...])


def _cross_sample(x, mk, mv, layer, g_cross, w_mq, w_mo):
    n, d = x.shape
    nb = mk.shape[1]
    ls = n // nb
    bt = BT_SAMPLE
    tt = bt * ls
    mem_spec = pl.BlockSpec((1, bt, N_MEM, MEM_HEADS, MEM_HEAD_DIM), lambda i: (layer, i, 0, 0, 0))
    return pl.pallas_call(
        _cross_sample_kernel,
        grid=(nb // bt,),
        in_specs=[pl.BlockSpec((tt, d), lambda i: (i, 0)), mem_spec, mem_spec,
                  _const((1, d)), _layer(layer, (d, d)), _layer(layer, (d, d))],
        out_specs=pl.BlockSpec((tt, d), lambda i: (i, 0)),
        out_shape=jax.ShapeDtypeStruct((n, d), F32),
        scratch_shapes=[pltpu.VMEM((tt, d), F32), pltpu.VMEM((tt, d), F32),
                        pltpu.VMEM((bt, MEM_HEADS * ls, N_MEM * MEM_HEADS), BF16),
                        pltpu.VMEM((bt, MEM_HEADS * ls, LANES), F32)],
        compiler_params=_params(1),
        name=f"cross_sample_{layer}",
    )(x, mk, mv, g_cross, w_mq, w_mo)


def _mlp_stream_kernel(x_ref, gf_ref, wup_ref, wdown_ref, gfin_ref, xo_ref, wup_b_ref, wdown_b_ref,
                       h_s, acc_s, *, final):
    j = pl.program_id(0)

    @pl.when(j == 0)
    def _():
        x = x_ref[...]
        h_s[...] = _rms(x, gf_ref[...]).astype(BF16)
        acc_s[...] = x

    wu = wup_ref[...].astype(BF16)
    wd = wdown_ref[...].astype(BF16)
    wup_b_ref[...] = wu
    wdown_b_ref[...] = wd
    a = jnp.square(jnp.maximum(_dot(h_s[...], wu), 0.0)).astype(BF16)
    acc_s[...] += _dot(a, wd)

    @pl.when(j == pl.num_programs(0) - 1)
    def _():
        y = acc_s[...]
        if final:
            y = _rms(y, gfin_ref[...])
        xo_ref[...] = y


def _mlp_sample(x, layer, g_ffn, w_up, w_down, g_final, final):
    n, d = x.shape
    fc = FF_CHUNK
    return pl.pallas_call(
        functools.partial(_mlp_stream_kernel, final=final),
        grid=(D_FF // fc,),
        in_specs=[_const((n, d)), _const((1, d)),
                  pl.BlockSpec((None, d, fc), lambda c: (layer, 0, c)),
                  pl.BlockSpec((None, fc, d), lambda c: (layer, c, 0)),
                  _const((1, d))],
        out_specs=[pl.BlockSpec((n, d), lambda c: (0, 0)),
                   pl.BlockSpec((d, fc), lambda c: (0, c)),
                   pl.BlockSpec((fc, d), lambda c: (c, 0))],
        out_shape=[jax.ShapeDtypeStruct((n, d), F32),
                   jax.ShapeDtypeStruct((d, D_FF), BF16),
                   jax.ShapeDtypeStruct((D_FF, d), BF16)],
        scratch_shapes=[pltpu.VMEM((n, d), BF16), pltpu.VMEM((n, d), F32)],
        compiler_params=_params(1),
        name=f"mlp_sample_{layer}",
    )(x, g_ffn, w_up, w_down, g_final)


def kernel(x_prompt, x_sample, state_ret, cache_win_k, cache_win_v, cache_mem_k, cache_mem_v, mem_prompt, g_mix, w_in_e, sgu_ln_g, sgu_ln_b, w_spatial, b_spatial, w_out_e, w_qkv_o, b_qkv_o, sinks, w_out_o, b_out_o, g_cross, g_mem, w_mq, w_mk, w_mv, w_mo, g_ffn, w_up, w_down, g_final):
    bp, lp, d = x_prompt.shape
    bs, ls, _ = x_sample.shape
    assert TM_PROMPT % RET_CHUNK_PROMPT == 0 and RET_CHUNK_PROMPT % CHUNK == 0
    assert d == D_MODEL and lp % TM_PROMPT == 0 and lp % TM_PROMPT_WIDE == 0 and bs % BT_SAMPLE == 0 and bs % BT_EVEN_SAMPLE == 0
    assert ls < CHUNK and CHUNK % ls == 0 and D_FF % FF_CHUNK == 0 and bp % MEMKV_BT == 0
    ns = bs * ls
    kvw = C_KV_HEADS * C_HEAD_DIM

    def row(v):
        return v.reshape(1, -1).astype(F32)

    def bf(w):
        return w.astype(BF16)

    pos_p = np.arange(lp)
    ret_tab_p = _ret_rope_tables(pos_p)
    tt_e = BT_EVEN_SAMPLE * ls
    ret_tab_s = _ret_rope_tables(PAST_LEN + (np.arange(tt_e) % ls))
    win_tab_p = _win_rope_tables(pos_p)
    win_tab_s = _win_rope_tables(PAST_LEN + (np.arange(ns) % ls))

    w_in_b, w_out_e_b, w_qkv_b, w_out_o_b = bf(w_in_e), bf(w_out_e), bf(w_qkv_o), bf(w_out_o)
    w_mq_b, w_mo_b = bf(w_mq), bf(w_mo)

    mk_p, mv_p, mk_pb, mv_pb = _memkv(mem_prompt, g_mem, w_mk, w_mv)
    mk_s, mv_s = cache_mem_k, cache_mem_v

    xp = x_prompt
    xs = x_sample.reshape(ns, d)
    final_g = row(g_final)
    outs = {}
    for layer in range(DEPTH):
        j = layer // 2
        g = row(g_mix[layer])
        if layer % 2 == 0:
            ln_g, ln_b = row(sgu_ln_g[j]), row(sgu_ln_b[j])
            tril = jnp.tril(jnp.ones((CHUNK, CHUNK), bool))
            w_s = bf(jnp.where(tril[None], w_spatial[j], 0.0))
            b_s = b_spatial[j].reshape(SGU_GROUPS, CHUNK, 1)
            xp, ret_p = _even_prompt(xp, j, g, w_in_b, ln_g, ln_b, w_s, b_s, w_out_e_b, *ret_tab_p)
            w_small = w_s[:, :ls, :ls]
            eye = jnp.eye(BT_EVEN_SAMPLE, dtype=BF16)
            w_bd = jnp.einsum("ab,gts->gatbs", eye, w_small).reshape(SGU_GROUPS, tt_e, tt_e)
            b_bd = jnp.tile(b_spatial[j][:, :ls], (1, BT_EVEN_SAMPLE)).reshape(SGU_GROUPS, tt_e, 1)
            xs, ret_s, v_rows = _even_sample(xs, state_ret[j], j, g, w_in_b, ln_g, ln_b, w_bd, b_bd,
                                             w_out_e_b, *ret_tab_s)
            outs["ret_p"], outs["ret_s"], outs["v_rows"] = ret_p, ret_s, v_rows
        else:
            b_qkv, b_out = row(b_qkv_o[j]), row(b_out_o[j])
            xp, wk_p, wv_p = _odd_prompt(xp, j, sinks[j], g, w_qkv_b, b_qkv, w_out_o_b, b_out,
                                         *win_tab_p)
            xs, wk_s, wv_s = _odd_sample(xs, cache_win_k[j].transpose(0, 2, 3, 1).reshape(bs, kvw, CHUNK),
                                         cache_win_v[j].transpose(0, 2, 3, 1).reshape(bs, kvw, CHUNK), j, sinks[j],
                                         g, w_qkv_b, b_qkv, w_out_o_b, b_out, *win_tab_s)
            outs["wk_p"], outs["wv_p"], outs["wk_s"], outs["wv_s"] = wk_p, wv_p, wk_s, wv_s
        final = layer == DEPTH - 1
        gc, gf = row(g_cross[layer]), row(g_ffn[layer])
        xs = _cross_sample(xs, mk_s, mv_s, layer, gc, w_mq_b, w_mo_b)
        xs, w_up_l, w_down_l = _mlp_sample(xs, layer, gf, w_up, w_down, final_g, final)
        xp = _cross_mlp_prompt(xp, mk_pb, mv_pb, layer, gc, w_mq_b, w_mo_b, gf, w_up_l, w_down_l,
                               final_g, final)

    n_even, n_odd = (DEPTH + 1) // 2, DEPTH // 2
    return (xp,
            xs.reshape(bs, ls, d),
            mk_p,
            mv_p,
            outs["ret_p"].reshape(n_even, bp, RET_HEADS, RET_DK, RET_DV),
            outs["ret_s"].reshape(n_even, bs, RET_HEADS, RET_DK, RET_DV),
            outs["v_rows"].reshape(n_even, bs, ls, SGU_GROUPS, SGU_GROUP_DIM),
            outs["wk_p"].reshape(n_odd, bp, CHUNK, C_KV_HEADS, C_HEAD_DIM),
            outs["wv_p"].reshape(n_odd, bp, CHUNK, C_KV_HEADS, C_HEAD_DIM),
            outs["wk_s"].reshape(n_odd, bs, CHUNK, C_KV_HEADS, C_HEAD_DIM),
            outs["wv_s"].reshape(n_odd, bs, CHUNK, C_KV_HEADS, C_HEAD_DIM))
```

```python
import functools
import math

import numpy as np
import jax
import jax.numpy as jnp
from jax import lax
from jax.experimental import pallas as pl
from jax.experimental.pallas import tpu as pltpu

F32 = jnp.float32
BF16 = jnp.bfloat16

D_MODEL = 1024
DEPTH = 2
PAST_LEN = 16384
EPS = 1e-6

SGU_GROUPS = 4
SGU_GROUP_DIM = 256
RET_HEADS = 4
RET_DK = 128
RET_DV = 256
RET_ROPE_BASE = 10000.0
CHUNK = 128
RET_CHUNK_PROMPT = 256
C_HEADS = 16
C_KV_HEADS = 4
C_HEAD_DIM = 64
C_ROPE_BASE = 150000.0
N_MEM = 256
MEM_HEADS = 4
MEM_HEAD_DIM = 256
D_FF = 4096
EVEN_IN = 5120
EVEN_OUT = 2048
ODD_IN = 1536

LANES = 128
VMEM_LIMIT = 56 * 2 ** 20
NEG = -1e30
LOG2E = math.log2(math.e)

TM_PROMPT = 512
TM_PROMPT_WIDE = 1024
BT_SAMPLE = 8
BT_EVEN_SAMPLE = 16
FF_CHUNK = 512
CROSS_BT = 4
CROSS_RING = 3
MEMKV_BT = 2


def _params(n_axes):
    return pltpu.CompilerParams(dimension_semantics=("arbitrary",) * n_axes,
                                vmem_limit_bytes=VMEM_LIMIT)


def _const(shape):
    nd = len(shape)
    return pl.BlockSpec(shape, lambda *_: (0,) * nd, pipeline_mode=pl.Buffered(1))


def _layer(layer, shape):
    nd = len(shape)
    return pl.BlockSpec((None,) + tuple(shape), lambda *_: (layer,) + (0,) * nd,
                        pipeline_mode=pl.Buffered(1))


def _rms(x, g):
    return x * lax.rsqrt(jnp.mean(x * x, axis=-1, keepdims=True) + EPS) * g


def _dot(a, b):
    return jnp.dot(a, b, preferred_element_type=F32)


def _dot_nt(a, b):
    return lax.dot_general(a, b, (((1,), (1,)), ((), ())), preferred_element_type=F32)


def _dot_tn(a, b):
    return lax.dot_general(a, b, (((0,), (0,)), ((), ())), preferred_element_type=F32)


def _ret_rope_tables(pos):
    half = RET_DK // 2
    inv = RET_ROPE_BASE ** (-np.arange(half, dtype=np.float64) / half)
    ang = pos.astype(np.float64)[:, None] * inv[None, :]
    cos, sin = np.cos(ang), np.sin(ang)
    return (jnp.asarray(np.concatenate([cos, cos], -1), F32),
            jnp.asarray(np.concatenate([-sin, sin], -1), F32))


def _win_rope_tables(pos):
    half = C_HEAD_DIM // 2
    inv = C_ROPE_BASE ** (-np.arange(half, dtype=np.float64) / half)
    ang = pos.astype(np.float64)[:, None] * inv[None, :]
    lane = np.arange(LANES)
    cos = np.cos(ang)[:, lane % half]
    sin = np.sin(ang)[:, lane % half]
    low = (lane % C_HEAD_DIM) < half
    return (jnp.asarray(cos, F32),
            jnp.asarray(np.where(low[None], -sin, 0.0), F32),
            jnp.asarray(np.where(low[None], 0.0, sin), F32))


def _ret_rotate(x, cosf, sinf):
    return x * cosf + pltpu.roll(x, RET_DK // 2, 1) * sinf


def _win_rotate(x, cos, sin_lo, sin_hi):
    return (x * cos + pltpu.roll(x, LANES - C_HEAD_DIM // 2, 1) * sin_lo
            + pltpu.roll(x, C_HEAD_DIM // 2, 1) * sin_hi)


def _log_gamma(h):
    return math.log1p(-2.0 ** (-5.0 - h))


def _memkv_kernel(mem_ref, g_ref, wk_ref, wv_ref, k_ref, v_ref, kb_ref, vb_ref, wk_s, wv_s):
    @pl.when(pl.program_id(1) == 0)
    def _():
        wk_s[...] = wk_ref[0].astype(BF16)
        wv_s[...] = wv_ref[0].astype(BF16)

    nb = mem_ref.shape[0]
    m = _rms(mem_ref[...].reshape(nb * N_MEM, D_MODEL), g_ref[0]).astype(BF16)
    for w_s, out_ref, bf_ref in ((wk_s, k_ref, kb_ref), (wv_s, v_ref, vb_ref)):
        kv = _dot(m, w_s[...])
        bf_ref[0] = kv.astype(BF16).reshape(nb, N_MEM, D_MODEL)
        for bi in range(nb):
            for hd in range(MEM_HEADS):
                out_ref[0, bi, :, hd, :] = kv[bi * N_MEM:(bi + 1) * N_MEM,
                                              hd * MEM_HEAD_DIM:(hd + 1) * MEM_HEAD_DIM]


def _memkv(mem, g_mem, w_mk, w_mv):
    b = mem.shape[0]
    nb = MEMKV_BT
    out = jax.ShapeDtypeStruct((DEPTH, b, N_MEM, MEM_HEADS, MEM_HEAD_DIM), F32)
    out_b = jax.ShapeDtypeStruct((DEPTH, b, N_MEM, D_MODEL), BF16)
    spec5 = pl.BlockSpec((1, nb, N_MEM, MEM_HEADS, MEM_HEAD_DIM), lambda l, i: (l, i, 0, 0, 0))
    spec_b = pl.BlockSpec((1, nb, N_MEM, D_MODEL), lambda l, i: (l, i, 0, 0))
    return pl.pallas_call(
        _memkv_kernel,
        grid=(DEPTH, b // nb),
        in_specs=[pl.BlockSpec((nb, N_MEM, D_MODEL), lambda l, i: (i, 0, 0)),
                  pl.BlockSpec((1, 1, D_MODEL), lambda l, i: (l, 0, 0)),
                  pl.BlockSpec((1, D_MODEL, D_MODEL), lambda l, i: (l, 0, 0)),
                  pl.BlockSpec((1, D_MODEL, D_MODEL), lambda l, i: (l, 0, 0))],
        out_specs=[spec5, spec5, spec_b, spec_b],
        out_shape=[out, out, out_b, out_b],
        scratch_shapes=[pltpu.VMEM((D_MODEL, D_MODEL), BF16), pltpu.VMEM((D_MODEL, D_MODEL), BF16)],
        compiler_params=_params(2),
        name="memkv",
    )(mem, g_mem.reshape(DEPTH, 1, D_MODEL), w_mk, w_mv)


def _even_front(x, g_ref, win_ref, lng_ref, lnb_ref, cos_ref, sin_ref):
    h = _rms(x, g_ref[...]).astype(BF16)

    def proj(lo, hi):
        return _dot(h, win_ref[:, lo:hi])

    u = jax.nn.gelu(proj(0, 1024))
    v = jax.nn.gelu(proj(1024, 2048))
    vc = v - jnp.mean(v, axis=-1, keepdims=True)
    vn = vc * lax.rsqrt(jnp.mean(vc * vc, axis=-1, keepdims=True) + EPS) * lng_ref[...] + lnb_ref[...]
    q = proj(2048, 2560)
    k = proj(2560, 3072)
    vr = proj(3072, 4096)
    gate = proj(4096, 5120)
    cosf, sinf = cos_ref[...], sin_ref[...]
    qs, ks = [], []
    for hd in range(RET_HEADS):
        cols = slice(hd * RET_DK, (hd + 1) * RET_DK)
        qs.append(_ret_rotate(q[:, cols], cosf, sinf))
        ks.append(_ret_rotate(k[:, cols], cosf, sinf) * (RET_DK ** -0.5))
    return u, vn, qs, ks, vr, gate


def _head_norm_gate(o, gate):
    on = o * lax.rsqrt(jnp.mean(o * o, axis=-1, keepdims=True) + EPS)
    return on * jax.nn.silu(gate)


def _even_prompt_kernel(x_ref, g_ref, win_ref, lng_ref, lnb_ref, ws_ref, bs_ref, wout_ref,
                        cos_ref, sin_ref, xo_ref, sout_ref,
                        state, u_s, vn_s, q_s, k_s, vr_s, gate_s, cat_s, oi_s):
    t = pl.program_id(1)
    tm = x_ref.shape[1]

    @pl.when(t == 0)
    def _():
        state[...] = jnp.zeros_like(state)

    x = x_ref[0]
    u, vn, qs, ks, vr, gate = _even_front(x, g_ref, win_ref, lng_ref, lnb_ref, cos_ref, sin_ref)
    u_s[...] = u
    vn_s[...] = vn.astype(BF16)
    for hd in range(RET_HEADS):
        cols = slice(hd * RET_DK, (hd + 1) * RET_DK)
        q_s[:, cols] = qs[hd].astype(BF16)
        k_s[cols, :] = ks[hd].T
    vr_s[...] = vr.astype(BF16)
    gate_s[...] = gate

    rc = RET_CHUNK_PROMPT
    row = lax.broadcasted_iota(jnp.int32, (rc, rc), 0)
    col = lax.broadcasted_iota(jnp.int32, (rc, rc), 1)
    diff = (row - col).astype(F32)
    ridx = lax.broadcasted_iota(jnp.int32, (rc, 1), 0).astype(F32)
    cidx = lax.broadcasted_iota(jnp.int32, (1, rc), 1).astype(F32)
    decay, xi, zeta, cdec = [], [], [], []
    for hd in range(RET_HEADS):
        lg = _log_gamma(hd)
        decay.append(jnp.where(diff >= 0, jnp.exp(jnp.maximum(diff, 0.0) * lg), 0.0))
        xi.append(jnp.exp((ridx + 1.0) * lg))
        zeta.append(jnp.exp((rc - 1.0 - cidx) * lg))
        cdec.append(math.exp(rc * lg))

    for c in range(tm // rc):
        rows = pl.ds(c * rc, rc)
        for hd in range(RET_HEADS):
            kcols = slice(hd * RET_DK, (hd + 1) * RET_DK)
            vcols = slice(hd * RET_DV, (hd + 1) * RET_DV)
            sc = _dot(q_s[rows, kcols], k_s[kcols, rows].astype(BF16)) * decay[hd]
            oi_s[rows, vcols] = _dot(sc.astype(BF16), vr_s[rows, vcols])
    for c in range(tm // CHUNK):
        rows = pl.ds(c * CHUNK, CHUNK)
        for g in range(SGU_GROUPS):
            cols = slice(g * SGU_GROUP_DIM, (g + 1) * SGU_GROUP_DIM)
            mixed = _dot(ws_ref[g], vn_s[rows, cols]) + bs_ref[g]
            cat_s[rows, cols] = (u_s[rows, cols] * mixed).astype(BF16)
    for c in range(tm // rc):
        rows = pl.ds(c * rc, rc)
        for hd in range(RET_HEADS):
            kcols = slice(hd * RET_DK, (hd + 1) * RET_DK)
            vcols = slice(hd * RET_DV, (hd + 1) * RET_DV)
            qc = q_s[rows, kcols]
            kt = k_s[kcols, rows]
            vc = vr_s[rows, vcols]
            s_prev = state[hd]
            o = oi_s[rows, vcols] + _dot(qc, s_prev.astype(BF16)) * xi[hd]
            state[hd] = s_prev * cdec[hd] + _dot((kt * zeta[hd]).astype(BF16), vc)
            ocols = slice(D_MODEL + hd * RET_DV, D_MODEL + (hd + 1) * RET_DV)
            cat_s[rows, ocols] = _head_norm_gate(o, gate_s[rows, vcols]).astype(BF16)
    xo_ref[0] = x + _dot(cat_s[...], wout_ref[...])

    @pl.when(t == pl.num_programs(1) - 1)
    def _():
        sout_ref[0] = state[...]


def _even_prompt(x, j, g, w_in, ln_g, ln_b, w_s, b_s, w_out, cosf, sinf):
    b, l, d = x.shape
    tm = TM_PROMPT
    return pl.pallas_call(
        _even_prompt_kernel,
        grid=(b, l // tm),
        in_specs=[pl.BlockSpec((1, tm, d), lambda i, t: (i, t, 0)),
                  _const((1, d)), _layer(j, (d, EVEN_IN)), _const((1, d)), _const((1, d)),
                  _const((SGU_GROUPS, CHUNK, CHUNK)), _const((SGU_GROUPS, CHUNK, 1)),
                  _layer(j, (EVEN_OUT, d)),
                  pl.BlockSpec((tm, RET_DK), lambda i, t: (t, 0)),
                  pl.BlockSpec((tm, RET_DK), lambda i, t: (t, 0))],
        out_specs=[pl.BlockSpec((1, tm, d), lambda i, t: (i, t, 0)),
                   pl.BlockSpec((1, RET_HEADS, RET_DK, RET_DV), lambda i, t: (i, 0, 0, 0))],
        out_shape=[jax.ShapeDtypeStruct((b, l, d), F32),
                   jax.ShapeDtypeStruct((b, RET_HEADS, RET_DK, RET_DV), F32)],
        scratch_shapes=[pltpu.VMEM((RET_HEADS, RET_DK, RET_DV), F32),
                        pltpu.VMEM((tm, d), F32), pltpu.VMEM((tm, d), BF16),
                        pltpu.VMEM((tm, RET_HEADS * RET_DK), BF16),
                        pltpu.VMEM((RET_HEADS * RET_DK, tm), F32),
                        pltpu.VMEM((tm, d), BF16), pltpu.VMEM((tm, d), F32),
                        pltpu.VMEM((tm, EVEN_OUT), BF16), pltpu.VMEM((tm, d), F32)],
        compiler_params=_params(2),
        name="even_prompt",
    )(x, g, w_in, ln_g, ln_b, w_s, b_s, w_out, cosf, sinf)


def _even_sample_kernel(x_ref, s0_ref, g_ref, win_ref, lng_ref, lnb_ref, ws_ref, bs_ref, wout_ref,
                        cos_ref, sin_ref, xo_ref, sout_ref, vrow_ref,
                        q_s, kz_s, v_s, o_s):
    tt = x_ref.shape[0]
    ls = tt // s0_ref.shape[0]
    x = x_ref[...]
    u, vn, qs, ks, vr, gate = _even_front(x, g_ref, win_ref, lng_ref, lnb_ref, cos_ref, sin_ref)
    vrow_ref[...] = vn
    vn_b = vn.astype(BF16)
    vr_b = vr.astype(BF16)

    row = lax.broadcasted_iota(jnp.int32, (tt, tt), 0)
    col = lax.broadcasted_iota(jnp.int32, (tt, tt), 1)
    same = (row // ls) == (col // ls)
    diff = (row - col).astype(F32)
    rloc = (lax.broadcasted_iota(jnp.int32, (tt, 1), 0) % ls).astype(F32)

    cat = []
    for g in range(SGU_GROUPS):
        cols = slice(g * SGU_GROUP_DIM, (g + 1) * SGU_GROUP_DIM)
        mixed = _dot(ws_ref[g], vn_b[:, cols]) + bs_ref[g]
        cat.append((u[:, cols] * mixed).astype(BF16))

    xi, cdec = [], []
    for hd in range(RET_HEADS):
        lg = _log_gamma(hd)
        kcols = slice(hd * RET_DK, (hd + 1) * RET_DK)
        vcols = slice(hd * RET_DV, (hd + 1) * RET_DV)
        decay = jnp.where(same & (diff >= 0), jnp.exp(jnp.maximum(diff, 0.0) * lg), 0.0)
        xi.append(jnp.exp((rloc[:ls] + 1.0) * lg))
        cdec.append(math.exp(ls * lg))
        zeta = jnp.exp((ls - 1.0 - rloc) * lg)
        sc = _dot_nt(qs[hd].astype(BF16), ks[hd].astype(BF16)) * decay
        o_s[:, vcols] = _dot(sc.astype(BF16), vr_b[:, vcols])
        q_s[:, kcols] = qs[hd]
        kz_s[:, kcols] = ks[hd] * zeta
    v_s[...] = vr

    def per_batch(bi, carry):
        rows = pl.ds(pl.multiple_of(bi * ls, ls), ls)
        for hd in range(RET_HEADS):
            kcols = slice(hd * RET_DK, (hd + 1) * RET_DK)
            vcols = slice(hd * RET_DV, (hd + 1) * RET_DV)
            s_prev = s0_ref[bi, hd]
            qb = q_s[rows, kcols].astype(BF16)
            o_s[rows, vcols] += _dot(qb, s_prev.astype(BF16)) * xi[hd]
            sout_ref[bi, hd] = s_prev * cdec[hd] + _dot_tn(kz_s[rows, kcols].astype(BF16),
                                                          v_s[rows, vcols].astype(BF16))
        return carry

    lax.fori_loop(0, s0_ref.shape[0], per_batch, 0, unroll=True)
    for hd in range(RET_HEADS):
        vcols = slice(hd * RET_DV, (hd + 1) * RET_DV)
        cat.append(_head_norm_gate(o_s[:, vcols], gate[:, vcols]).astype(BF16))
    xo_ref[...] = x + _dot(jnp.concatenate(cat, axis=-1), wout_ref[...])


def _even_sample(x, s0, j, g, w_in, ln_g, ln_b, w_s_bd, b_s_bd, w_out, cosf, sinf):
    n, d = x.shape
    nb = s0.shape[0]
    ls = n // nb
    bt = BT_EVEN_SAMPLE
    tt = bt * ls
    state_spec = pl.BlockSpec((bt, RET_HEADS, RET_DK, RET_DV), lambda i: (i, 0, 0, 0))
    return pl.pallas_call(
        _even_sample_kernel,
        grid=(nb // bt,),
        in_specs=[pl.BlockSpec((tt, d), lambda i: (i, 0)), state_spec,
                  _const((1, d)), _layer(j, (d, EVEN_IN)), _const((1, d)), _const((1, d)),
                  _const((SGU_GROUPS, tt, tt)), _const((SGU_GROUPS, tt, 1)),
                  _layer(j, (EVEN_OUT, d)), _const((tt, RET_DK)), _const((tt, RET_DK))],
        out_specs=[pl.BlockSpec((tt, d), lambda i: (i, 0)), state_spec,
                   pl.BlockSpec((tt, d), lambda i: (i, 0))],
        out_shape=[jax.ShapeDtypeStruct((n, d), F32),
                   jax.ShapeDtypeStruct(s0.shape, F32),
                   jax.ShapeDtypeStruct((n, d), F32)],
        scratch_shapes=[pltpu.VMEM((tt, RET_HEADS * RET_DK), F32),
                        pltpu.VMEM((tt, RET_HEADS * RET_DK), F32),
                        pltpu.VMEM((tt, d), F32), pltpu.VMEM((tt, d), F32)],
        compiler_params=_params(1),
        name="even_sample",
    )(x, s0, g, w_in, ln_g, ln_b, w_s_bd, b_s_bd, w_out, cosf, sinf)


def _pad_even_odd(blk, kv_in_block):
    lane = lax.broadcasted_iota(jnp.int32, blk.shape, 1)
    rolled = pltpu.roll(blk, C_HEAD_DIM, 1)
    if kv_in_block == 0:
        even = jnp.where(lane < C_HEAD_DIM, blk, 0.0)
        odd = jnp.where(lane >= C_HEAD_DIM, rolled, 0.0)
    else:
        odd = jnp.where(lane >= C_HEAD_DIM, blk, 0.0)
        even = jnp.where(lane < C_HEAD_DIM, rolled, 0.0)
    return even.astype(BF16), odd.astype(BF16)


def _sink_softmax(sc, sink):
    m = jnp.maximum(jnp.max(sc, axis=-1, keepdims=True), sink)
    p = jnp.exp2(sc - m)
    den = jnp.sum(p, axis=-1, keepdims=True) + jnp.exp2(sink - m)
    return p, jnp.broadcast_to(1.0 / den, (sc.shape[0], LANES))


def _pair_combine(o_even, o_odd, inv_even, inv_odd):
    lane = lax.broadcasted_iota(jnp.int32, o_even.shape, 1)
    return (o_even + o_odd) * jnp.where(lane < C_HEAD_DIM, inv_even, inv_odd)


def _pair_sinks(sink_ref, kh, first_half):
    h0 = kh * (C_HEADS // C_KV_HEADS)
    return (jnp.where(first_half, sink_ref[h0], sink_ref[h0 + 2]) * LOG2E,
            jnp.where(first_half, sink_ref[h0 + 1], sink_ref[h0 + 3]) * LOG2E)


def _odd_front(x, g_ref, wqkv_ref, bqkv_ref, cos_ref, lo_ref, hi_ref):
    h = _rms(x, g_ref[...]).astype(BF16)
    qkv = _dot(h, wqkv_ref[...]) + bqkv_ref[...]
    cos, lo, hi = cos_ref[...], lo_ref[...], hi_ref[...]
    nq = C_HEADS * C_HEAD_DIM
    nk = C_KV_HEADS * C_HEAD_DIM
    qb = [_win_rotate(qkv[:, j * LANES:(j + 1) * LANES], cos, lo, hi) * (C_HEAD_DIM ** -0.5 * LOG2E)
          for j in range(nq // LANES)]
    kb = [_win_rotate(qkv[:, nq + j * LANES:nq + (j + 1) * LANES], cos, lo, hi)
          for j in range(nk // LANES)]
    v = qkv[:, nq + nk:]
    return qb, kb, v


def _odd_prompt_kernel(sink_ref, x_ref, g_ref, wqkv_ref, bqkv_ref, wout_ref, bout_ref,
                       cos_ref, lo_ref, hi_ref, xo_ref, wk_ref, wv_ref,
                       q_s, kpad, kprev, vpad, att_s, pp_s, inv_s):
    t = pl.program_id(1)
    tm = x_ref.shape[1]
    w = CHUNK
    x = x_ref[0]
    qb, kb, v = _odd_front(x, g_ref, wqkv_ref, bqkv_ref, cos_ref, lo_ref, hi_ref)
    for j, blk in enumerate(qb):
        q_s[:, j * LANES:(j + 1) * LANES] = blk.astype(BF16)

    @pl.when(t == 0)
    def _():
        kprev[...] = jnp.zeros((2 * C_KV_HEADS, LANES, w), BF16)
        vpad[:, 0:w, :] = jnp.zeros((2 * C_KV_HEADS, w, LANES), BF16)

    kt = [blk.T for blk in kb]
    k_tails = []
    for kh in range(C_KV_HEADS):
        j, r = divmod(kh, LANES // C_HEAD_DIM)
        head = kt[j][r * C_HEAD_DIM:(r + 1) * C_HEAD_DIM, :]
        zeros = jnp.zeros_like(head)
        ke_t = jnp.concatenate([head, zeros], axis=0).astype(BF16)
        ko_t = jnp.concatenate([zeros, head], axis=0).astype(BF16)
        kpad[2 * kh] = ke_t
        kpad[2 * kh + 1] = ko_t
        k_tails += [ke_t[:, tm - w:], ko_t[:, tm - w:]]
        ve, vo = _pad_even_odd(v[:, j * LANES:(j + 1) * LANES], r)
        vpad[2 * kh, w:, :] = ve
        vpad[2 * kh + 1, w:, :] = vo

    @pl.when(t == pl.num_programs(1) - 1)
    def _():
        for j, blk in enumerate(kb):
            wk_ref[0, :, j * LANES:(j + 1) * LANES] = blk[tm - w:, :]
        wv_ref[0] = v[tm - w:, :]

    qi = lax.broadcasted_iota(jnp.int32, (2 * w, w), 0) % w
    kj = lax.broadcasted_iota(jnp.int32, (2 * w, w), 1)
    upper = kj > qi
    first_half = lax.broadcasted_iota(jnp.int32, (2 * w, 1), 0) < w

    zero = jnp.zeros((), F32)
    for c in range(tm // w):
        rows = pl.ds(c * w, w)
        kkeys = pl.ds((c - 1) * w, 2 * w)
        prev_bias = jnp.where(t == 0, NEG, 0.0) if c == 0 else None
        for kh in range(C_KV_HEADS):
            if c == 0:
                ke = jnp.concatenate([kprev[2 * kh], kpad[2 * kh, :, 0:w]], axis=1)
                ko = jnp.concatenate([kprev[2 * kh + 1], kpad[2 * kh + 1, :, 0:w]], axis=1)
            else:
                ke, ko = kpad[2 * kh, :, kkeys], kpad[2 * kh + 1, :, kkeys]
            c0 = (2 * kh) * LANES
            q2 = jnp.concatenate([q_s[rows, c0:c0 + LANES], q_s[rows, c0 + LANES:c0 + 2 * LANES]], 0)
            for half, (kk, sink) in enumerate(zip((ke, ko), _pair_sinks(sink_ref, kh, first_half))):
                s2 = _dot(q2, kk)
                s_prev = s2[:, :w] if prev_bias is None else s2[:, :w] + prev_bias
                p, inv = _sink_softmax(jnp.where(upper, s_prev, s2[:, w:]), sink)
                u = (c * C_KV_HEADS + kh) * 2 + half
                pp_s[u] = jnp.concatenate([jnp.where(upper, p, zero), jnp.where(upper, zero, p)],
                                          axis=1).astype(BF16)
                inv_s[u] = inv
    for c in range(tm // w):
        rows = pl.ds(c * w, w)
        keys = pl.ds(c * w, 2 * w)
        for kh in range(C_KV_HEADS):
            c0 = (2 * kh) * LANES
            u = (c * C_KV_HEADS + kh) * 2
            o = _pair_combine(_dot(pp_s[u], vpad[2 * kh, keys, :]), _dot(pp_s[u + 1], vpad[2 * kh + 1, keys, :]),
                              inv_s[u], inv_s[u + 1])
            att_s[rows, c0:c0 + LANES] = o[:w].astype(BF16)
            att_s[rows, c0 + LANES:c0 + 2 * LANES] = o[w:].astype(BF16)
    xo_ref[0] = x + _dot(att_s[...], wout_ref[...]) + bout_ref[...]
    for i, tail in enumerate(k_tails):
        kprev[i] = tail
    vpad[:, 0:w, :] = vpad[:, tm:tm + w, :]


def _odd_prompt(x, j, sinks, g, w_qkv, b_qkv, w_out, b_out, cos, lo, hi):
    b, l, d = x.shape
    tm = TM_PROMPT_WIDE
    kvw = C_KV_HEADS * C_HEAD_DIM
    win_spec = pl.BlockSpec((1, CHUNK, kvw), lambda i, t: (i, 0, 0))
    tab = pl.BlockSpec((tm, LANES), lambda i, t: (t, 0))
    return pl.pallas_call(
        _odd_prompt_kernel,
        grid=(b, l // tm),
        in_specs=[pl.BlockSpec(memory_space=pltpu.SMEM),
                  pl.BlockSpec((1, tm, d), lambda i, t: (i, t, 0)),
                  _const((1, d)), _layer(j, (d, ODD_IN)), _const((1, ODD_IN)),
                  _layer(j, (d, d)), _const((1, d)), tab, tab, tab],
        out_specs=[pl.BlockSpec((1, tm, d), lambda i, t: (i, t, 0)), win_spec, win_spec],
        out_shape=[jax.ShapeDtypeStruct((b, l, d), F32),
                   jax.ShapeDtypeStruct((b, CHUNK, kvw), F32),
                   jax.ShapeDtypeStruct((b, CHUNK, kvw), F32)],
        scratch_shapes=[pltpu.VMEM((tm, d), BF16),
                        pltpu.VMEM((2 * C_KV_HEADS, LANES, tm), BF16),
                        pltpu.VMEM((2 * C_KV_HEADS, LANES, CHUNK), BF16),
                        pltpu.VMEM((2 * C_KV_HEADS, tm + CHUNK, LANES), BF16),
                        pltpu.VMEM((tm, d), BF16),
                        pltpu.VMEM((tm // CHUNK * C_KV_HEADS * 2, 2 * CHUNK, 2 * CHUNK), BF16),
                        pltpu.VMEM((tm // CHUNK * C_KV_HEADS * 2, 2 * CHUNK, LANES), F32)],
        compiler_params=_params(2),
        name="odd_prompt",
    )(sinks, x, g, w_qkv, b_qkv, w_out, b_out, cos, lo, hi)


def _odd_sample_kernel(sink_ref, x_ref, ck_ref, cv_ref, g_ref, wqkv_ref, bqkv_ref, wout_ref, bout_ref,
                       cos_ref, lo_ref, hi_ref, xo_ref, wk_ref, wv_ref, q_s, k_s, v_s, att_s, p_s, inv_s,
                       *, ls):
    i = pl.program_id(0)
    bt, kvw, w = ck_ref.shape
    tt = bt * ls

    @pl.when(i == 0)
    def _():
        qb, kb, v = _odd_front(x_ref[...], g_ref, wqkv_ref, bqkv_ref, cos_ref, lo_ref, hi_ref)
        for j, blk in enumerate(qb):
            q_s[:, j * LANES:(j + 1) * LANES] = blk.astype(BF16)
        for j, blk in enumerate(kb):
            k_s[:, j * LANES:(j + 1) * LANES] = blk
        v_s[...] = v

    rows = pl.ds(pl.multiple_of(i * tt, tt), tt)
    knew = k_s[rows, :]
    vnew = v_s[rows, :]
    ck = [ck_ref[bi].T for bi in range(bt)]
    cv = [cv_ref[bi].T for bi in range(bt)]
    for bi in range(bt):
        wk_ref[bi, 0:w - ls, :] = ck[bi][ls:, :]
        wk_ref[bi, w - ls:, :] = knew[bi * ls:(bi + 1) * ls, :]
        wv_ref[bi, 0:w - ls, :] = cv[bi][ls:, :]
        wv_ref[bi, w - ls:, :] = vnew[bi * ls:(bi + 1) * ls, :]
    kall = jnp.concatenate(ck + [knew], axis=0)
    vall = jnp.concatenate(cv + [vnew], axis=0)
    ns = bt * w + tt

    tok = lax.broadcasted_iota(jnp.int32, (2 * tt, ns), 0) % tt
    key = lax.broadcasted_iota(jnp.int32, (2 * tt, ns), 1)
    tb, tl = tok // ls, tok % ls
    cached = key < bt * w
    nkey = key - bt * w
    valid = ((cached & ((key // w) == tb) & ((key % w) > tl))
             | ((key >= bt * w) & ((nkey // ls) == tb) & ((nkey % ls) <= tl)))
    first_half = lax.broadcasted_iota(jnp.int32, (2 * tt, 1), 0) < tt

    for kh in range(C_KV_HEADS):
        j, r = divmod(kh, LANES // C_HEAD_DIM)
        c0 = (2 * kh) * LANES
        q2 = jnp.concatenate([q_s[rows, c0:c0 + LANES], q_s[rows, c0 + LANES:c0 + 2 * LANES]], axis=0)
        keys = _pad_even_odd(kall[:, j * LANES:(j + 1) * LANES], r)
        for half, (kk, sink) in enumerate(zip(keys, _pair_sinks(sink_ref, kh, first_half))):
            p, inv = _sink_softmax(jnp.where(valid, _dot_nt(q2, kk), NEG), sink)
            p_s[2 * kh + half] = p.astype(BF16)
            inv_s[2 * kh + half] = inv
    for kh in range(C_KV_HEADS):
        j, r = divmod(kh, LANES // C_HEAD_DIM)
        c0 = (2 * kh) * LANES
        ve, vo = _pad_even_odd(vall[:, j * LANES:(j + 1) * LANES], r)
        o = _pair_combine(_dot(p_s[2 * kh], ve), _dot(p_s[2 * kh + 1], vo), inv_s[2 * kh], inv_s[2 * kh + 1])
        att_s[rows, c0:c0 + LANES] = o[:tt].astype(BF16)
        att_s[rows, c0 + LANES:c0 + 2 * LANES] = o[tt:].astype(BF16)

    @pl.when(i == pl.num_programs(0) - 1)
    def _():
        xo_ref[...] = x_ref[...] + _dot(att_s[...], wout_ref[...]) + bout_ref[...]


def _odd_sample(x, ck, cv, j, sinks, g, w_qkv, b_qkv, w_out, b_out, cos, lo, hi):
    n, d = x.shape
    nb, kvw, w = ck.shape
    bt = BT_SAMPLE
    tt = bt * (n // nb)
    cache_in = pl.BlockSpec((bt, kvw, w), lambda i: (i, 0, 0))
    cache_spec = pl.BlockSpec((bt, w, kvw), lambda i: (i, 0, 0))
    return pl.pallas_call(
        functools.partial(_odd_sample_kernel, ls=n // nb),
        grid=(nb // bt,),
        in_specs=[pl.BlockSpec(memory_space=pltpu.SMEM),
                  _const((n, d)), cache_in, cache_in,
                  _const((1, d)), _layer(j, (d, ODD_IN)), _const((1, ODD_IN)),
                  _layer(j, (d, d)), _const((1, d)),
                  _const((n, LANES)), _const((n, LANES)), _const((n, LANES))],
        out_specs=[pl.BlockSpec((n, d), lambda i: (0, 0)), cache_spec, cache_spec],
        out_shape=[jax.ShapeDtypeStruct((n, d), F32),
                   jax.ShapeDtypeStruct((nb, w, kvw), F32), jax.ShapeDtypeStruct((nb, w, kvw), F32)],
        scratch_shapes=[pltpu.VMEM((n, d), BF16), pltpu.VMEM((n, kvw), F32), pltpu.VMEM((n, kvw), F32),
                        pltpu.VMEM((n, d), BF16),
                        pltpu.VMEM((2 * C_KV_HEADS, 2 * tt, bt * w + tt), BF16),
                        pltpu.VMEM((2 * C_KV_HEADS, 2 * tt, LANES), F32)],
        compiler_params=_params(1),
        name="odd_sample",
    )(sinks, x, ck, cv, g, w_qkv, b_qkv, w_out, b_out, cos, lo, hi)


def _cross_heads(q, mk, mv):
    outs = []
    for hd in range(MEM_HEADS):
        cols = slice(hd * MEM_HEAD_DIM, (hd + 1) * MEM_HEAD_DIM)
        s = _dot_nt(q[:, cols], mk[:, cols])
        p = jnp.exp(s - jnp.max(s, axis=-1, keepdims=True))
        den = jnp.sum(p, axis=-1, keepdims=True)
        outs.append(_dot(p.astype(BF16), mv[:, cols]) * (1.0 / den))
    return outs


def _mlp(x1, gf_ref, wup_ref, wdown_ref):
    h2 = _rms(x1, gf_ref[...]).astype(BF16)
    acc = x1
    for j in range(D_FF // D_MODEL):
        sl = slice(j * D_MODEL, (j + 1) * D_MODEL)
        a = jnp.square(jnp.maximum(_dot(h2, wup_ref[:, sl]), 0.0)).astype(BF16)
        acc = acc + _dot(a, wdown_ref[sl, :])
    return acc


def _cross_mlp_prompt_kernel(x_ref, mk_ref, mv_ref, gc_ref, wmq_ref, wmo_ref, gf_ref, wup_ref,
                             wdown_ref, gfin_ref, xo_ref, o_s, *, final):
    x = x_ref[0]
    h = _rms(x, gc_ref[...]).astype(BF16)
    q = (_dot(h, wmq_ref[...]) * (MEM_HEAD_DIM ** -0.5)).astype(BF16)
    outs = _cross_heads(q, mk_ref[0, 0], mv_ref[0, 0])
    for hd, o in enumerate(outs):
        o_s[:, hd * MEM_HEAD_DIM:(hd + 1) * MEM_HEAD_DIM] = o.astype(BF16)
    x1 = x + _dot(o_s[...], wmo_ref[...])
    y = _mlp(x1, gf_ref, wup_ref, wdown_ref)
    if final:
        y = _rms(y, gfin_ref[...])
    xo_ref[0] = y


def _cross_mlp_prompt(x, mk, mv, layer, g_cross, w_mq, w_mo, g_ffn, w_up, w_down, g_final, final):
    b, l, d = x.shape
    tm = TM_PROMPT_WIDE
    mem_spec = pl.BlockSpec((1, 1, N_MEM, d), lambda i, t: (layer, i, 0, 0))
    return pl.pallas_call(
        functools.partial(_cross_mlp_prompt_kernel, final=final),
        grid=(b, l // tm),
        in_specs=[pl.BlockSpec((1, tm, d), lambda i, t: (i, t, 0)), mem_spec, mem_spec,
                  _const((1, d)), _layer(layer, (d, d)), _layer(layer, (d, d)), _const((1, d)),
                  _const((d, D_FF)), _const((D_FF, d)), _const((1, d))],
        out_specs=pl.BlockSpec((1, tm, d), lambda i, t: (i, t, 0)),
        out_shape=jax.ShapeDtypeStruct((b, l, d), F32),
        scratch_shapes=[pltpu.VMEM((tm, d), BF16)],
        compiler_params=_params(2),
        name=f"cross_mlp_prompt_{layer}",
    )(x, mk, mv, g_cross, w_mq, w_mo, g_ffn, w_up, w_down, g_final)


def _cross_sample_kernel(x_ref, mk_hbm, mv_hbm, gc_ref, wmq_ref, wmo_ref, xo_ref,
                         q_s, o_s, p_s, inv_s, kbuf, vbuf, sem, *, ls, layer):
    i = pl.program_id(0)
    nsteps = pl.num_programs(0)
    nbuf, bt = kbuf.shape[0], kbuf.shape[1]
    tt = bt * ls
    nrow = N_MEM * MEM_HEADS

    def copies(tile, slot):
        src = pl.ds(tile * bt, bt)
        return (pltpu.make_async_copy(mk_hbm.at[layer, src], kbuf.at[slot], sem.at[0, slot]),
                pltpu.make_async_copy(mv_hbm.at[layer, src], vbuf.at[slot], sem.at[1, slot]))

    @pl.when(i == 0)
    def _():
        for s in range(nbuf - 1):
            for cp in copies(s, s):
                cp.start()
        h = _rms(x_ref[...], gc_ref[...]).astype(BF16)
        q_s[...] = _dot(h, wmq_ref[...]) * (MEM_HEAD_DIM ** -0.5)

    ahead = i + (nbuf - 1)

    @pl.when(ahead < nsteps)
    def _():
        for cp in copies(ahead, ahead % nbuf):
            cp.start()

    slot = i % nbuf
    for cp in copies(i, slot):
        cp.wait()

    qhead = lax.broadcasted_iota(jnp.int32, (MEM_HEADS * ls, nrow), 0) // ls
    khead = lax.broadcasted_iota(jnp.int32, (MEM_HEADS * ls, nrow), 1) % MEM_HEADS
    own = qhead == khead
    base = i * tt

    for bi in range(bt):
        qb = q_s[pl.ds(pl.multiple_of(base + bi * ls, ls), ls), :]
        q4 = jnp.concatenate([qb[:, hd * MEM_HEAD_DIM:(hd + 1) * MEM_HEAD_DIM]
                              for hd in range(MEM_HEADS)], axis=0).astype(BF16)
        k2 = kbuf[slot, bi].reshape(nrow, MEM_HEAD_DIM).astype(BF16)
        s = jnp.where(own, _dot_nt(q4, k2), NEG)
        p = jnp.exp(s - jnp.max(s, axis=-1, keepdims=True))
        p_s[bi] = p.astype(BF16)
        inv_s[bi] = jnp.broadcast_to(1.0 / jnp.sum(p, axis=-1, keepdims=True), (MEM_HEADS * ls, LANES))
    for bi in range(bt):
        v2 = vbuf[slot, bi].reshape(nrow, MEM_HEAD_DIM).astype(BF16)
        inv = inv_s[bi]
        o4 = _dot(p_s[bi], v2) * jnp.concatenate([inv] * (MEM_HEAD_DIM // LANES), axis=1)
        rows = pl.ds(pl.multiple_of(base + bi * ls, ls), ls)
        for hd in range(MEM_HEADS):
            o_s[rows, hd * MEM_HEAD_DIM:(hd + 1) * MEM_HEAD_DIM] = o4[hd * ls:(hd + 1) * ls]

    @pl.when(i == nsteps - 1)
    def _():
        xo_ref[...] = x_ref[...] + _dot(o_s[...].astype(BF16), wmo_ref[...])


def _cross_sample(x, mk, mv, layer, g_cross, w_mq, w_mo):
    n, d = x.shape
    nb = mk.shape[1]
    ls = n // nb
    bt = CROSS_BT
    slot = (CROSS_RING, bt, N_MEM, MEM_HEADS, MEM_HEAD_DIM)
    return pl.pallas_call(
        functools.partial(_cross_sample_kernel, ls=ls, layer=layer),
        grid=(nb // bt,),
        in_specs=[_const((n, d)), pl.BlockSpec(memory_space=pl.ANY), pl.BlockSpec(memory_space=pl.ANY),
                  _const((1, d)), _layer(layer, (d, d)), _layer(layer, (d, d))],
        out_specs=pl.BlockSpec((n, d), lambda i: (0, 0)),
        out_shape=jax.ShapeDtypeStruct((n, d), F32),
        scratch_shapes=[pltpu.VMEM((n, d), F32), pltpu.VMEM((n, d), F32),
                        pltpu.VMEM((bt, MEM_HEADS * ls, N_MEM * MEM_HEADS), BF16),
                        pltpu.VMEM((bt, MEM_HEADS * ls, LANES), F32),
                        pltpu.VMEM(slot, F32), pltpu.VMEM(slot, F32),
                        pltpu.SemaphoreType.DMA((2, CROSS_RING))],
        compiler_params=_params(1),
        name=f"cross_sample_{layer}",
    )(x, mk, mv, g_cross, w_mq, w_mo)


def _mlp_stream_kernel(x_ref, gf_ref, wup_ref, wdown_ref, gfin_ref, xo_ref, wup_b_ref, wdown_b_ref,
                       h_s, acc_s, *, final):
    j = pl.program_id(0)

    @pl.when(j == 0)
    def _():
        x = x_ref[...]
        h_s[...] = _rms(x, gf_ref[...]).astype(BF16)
        acc_s[...] = x

    wu = wup_ref[...].astype(BF16)
    wd = wdown_ref[...].astype(BF16)
    wup_b_ref[...] = wu
    wdown_b_ref[...] = wd
    a = jnp.square(jnp.maximum(_dot(h_s[...], wu), 0.0)).astype(BF16)
    acc_s[...] += _dot(a, wd)

    @pl.when(j == pl.num_programs(0) - 1)
    def _():
        y = acc_s[...]
        if final:
            y = _rms(y, gfin_ref[...])
        xo_ref[...] = y


def _mlp_sample(x, layer, g_ffn, w_up, w_down, g_final, final):
    n, d = x.shape
    fc = FF_CHUNK
    return pl.pallas_call(
        functools.partial(_mlp_stream_kernel, final=final),
        grid=(D_FF // fc,),
        in_specs=[_const((n, d)), _const((1, d)),
                  pl.BlockSpec((None, d, fc), lambda c: (layer, 0, c)),
                  pl.BlockSpec((None, fc, d), lambda c: (layer, c, 0)),
                  _const((1, d))],
        out_specs=[pl.BlockSpec((n, d), lambda c: (0, 0)),
                   pl.BlockSpec((d, fc), lambda c: (0, c)),
                   pl.BlockSpec((fc, d), lambda c: (c, 0))],
        out_shape=[jax.ShapeDtypeStruct((n, d), F32),
                   jax.ShapeDtypeStruct((d, D_FF), BF16),
                   jax.ShapeDtypeStruct((D_FF, d), BF16)],
        scratch_shapes=[pltpu.VMEM((n, d), BF16), pltpu.VMEM((n, d), F32)],
        compiler_params=_params(1),
        name=f"mlp_sample_{layer}",
    )(x, g_ffn, w_up, w_down, g_final)


def kernel(x_prompt, x_sample, state_ret, cache_win_k, cache_win_v, cache_mem_k, cache_mem_v, mem_prompt, g_mix, w_in_e, sgu_ln_g, sgu_ln_b, w_spatial, b_spatial, w_out_e, w_qkv_o, b_qkv_o, sinks, w_out_o, b_out_o, g_cross, g_mem, w_mq, w_mk, w_mv, w_mo, g_ffn, w_up, w_down, g_final):
    bp, lp, d = x_prompt.shape
    bs, ls, _ = x_sample.shape
    assert TM_PROMPT % RET_CHUNK_PROMPT == 0 and RET_CHUNK_PROMPT % CHUNK == 0
    assert d == D_MODEL and lp % TM_PROMPT == 0 and lp % TM_PROMPT_WIDE == 0 and bs % BT_SAMPLE == 0 and bs % BT_EVEN_SAMPLE == 0 and bs % CROSS_BT == 0
    assert bs // CROSS_BT >= CROSS_RING
    assert ls < CHUNK and CHUNK % ls == 0 and D_FF % FF_CHUNK == 0 and bp % MEMKV_BT == 0
    ns = bs * ls
    kvw = C_KV_HEADS * C_HEAD_DIM

    def row(v):
        return v.reshape(1, -1).astype(F32)

    def bf(w):
        return w.astype(BF16)

    pos_p = np.arange(lp)
    ret_tab_p = _ret_rope_tables(pos_p)
    tt_e = BT_EVEN_SAMPLE * ls
    ret_tab_s = _ret_rope_tables(PAST_LEN + (np.arange(tt_e) % ls))
    win_tab_p = _win_rope_tables(pos_p)
    win_tab_s = _win_rope_tables(PAST_LEN + (np.arange(ns) % ls))

    w_in_b, w_out_e_b, w_qkv_b, w_out_o_b = bf(w_in_e), bf(w_out_e), bf(w_qkv_o), bf(w_out_o)
    w_mq_b, w_mo_b = bf(w_mq), bf(w_mo)

    mk_p, mv_p, mk_pb, mv_pb = _memkv(mem_prompt, g_mem, w_mk, w_mv)
    mk_s, mv_s = cache_mem_k, cache_mem_v

    xp = x_prompt
    xs = x_sample.reshape(ns, d)
    final_g = row(g_final)
    outs = {}
    for layer in range(DEPTH):
        j = layer // 2
        g = row(g_mix[layer])
        if layer % 2 == 0:
            ln_g, ln_b = row(sgu_ln_g[j]), row(sgu_ln_b[j])
            tril = jnp.tril(jnp.ones((CHUNK, CHUNK), bool))
            w_s = bf(jnp.where(tril[None], w_spatial[j], 0.0))
            b_s = b_spatial[j].reshape(SGU_GROUPS, CHUNK, 1)
            xp, ret_p = _even_prompt(xp, j, g, w_in_b, ln_g, ln_b, w_s, b_s, w_out_e_b, *ret_tab_p)
            w_small = w_s[:, :ls, :ls]
            eye = jnp.eye(BT_EVEN_SAMPLE, dtype=BF16)
            w_bd = jnp.einsum("ab,gts->gatbs", eye, w_small).reshape(SGU_GROUPS, tt_e, tt_e)
            b_bd = jnp.tile(b_spatial[j][:, :ls], (1, BT_EVEN_SAMPLE)).reshape(SGU_GROUPS, tt_e, 1)
            xs, ret_s, v_rows = _even_sample(xs, state_ret[j], j, g, w_in_b, ln_g, ln_b, w_bd, b_bd,
                                             w_out_e_b, *ret_tab_s)
            outs["ret_p"], outs["ret_s"], outs["v_rows"] = ret_p, ret_s, v_rows
        else:
            b_qkv, b_out = row(b_qkv_o[j]), row(b_out_o[j])
            xp, wk_p, wv_p = _odd_prompt(xp, j, sinks[j], g, w_qkv_b, b_qkv, w_out_o_b, b_out,
                                         *win_tab_p)
            xs, wk_s, wv_s = _odd_sample(xs, cache_win_k[j].transpose(0, 2, 3, 1).reshape(bs, kvw, CHUNK),
                                         cache_win_v[j].transpose(0, 2, 3, 1).reshape(bs, kvw, CHUNK), j, sinks[j],
                                         g, w_qkv_b, b_qkv, w_out_o_b, b_out, *win_tab_s)
            outs["wk_p"], outs["wv_p"], outs["wk_s"], outs["wv_s"] = wk_p, wv_p, wk_s, wv_s
        final = layer == DEPTH - 1
        gc, gf = row(g_cross[layer]), row(g_ffn[layer])
        xs = _cross_sample(xs, mk_s, mv_s, layer, gc, w_mq_b, w_mo_b)
        xs, w_up_l, w_down_l = _mlp_sample(xs, layer, gf, w_up, w_down, final_g, final)
        xp = _cross_mlp_prompt(xp, mk_pb, mv_pb, layer, gc, w_mq_b, w_mo_b, gf, w_up_l, w_down_l,
                               final_g, final)

    n_even, n_odd = (DEPTH + 1) // 2, DEPTH // 2
    return (xp,
            xs.reshape(bs, ls, d),
            mk_p,
            mv_p,
            outs["ret_p"].reshape(n_even, bp, RET_HEADS, RET_DK, RET_DV),
            outs["ret_s"].reshape(n_even, bs, RET_HEADS, RET_DK, RET_DV),
            outs["v_rows"].reshape(n_even, bs, ls, SGU_GROUPS, SGU_GROUP_DIM),
            outs["wk_p"].reshape(n_odd, bp, CHUNK, C_KV_HEADS, C_HEAD_DIM),
            outs["wv_p"].reshape(n_odd, bp, CHUNK, C_KV_HEADS, C_HEAD_DIM),
            outs["wk_s"].reshape(n_odd, bs, CHUNK, C_KV_HEADS, C_HEAD_DIM),
            outs["wv_s"].reshape(n_odd, bs, CHUNK, C_KV_HEADS, C_HEAD_DIM))
```
